```python
import math
import jax, jax.numpy as jnp
from jax import lax
import numpy as np

D_MODEL = 2048
BATCH = 2
SEQ = 4096
DEPTH = 2

CHUNK = 64
Q_BLOCK = 128
N_BRANCH = 4
D_BRANCH = D_MODEL // 4
D_FF = 4 * D_MODEL
ROPE_THETA = 10000.0
EPS = 1e-6

DIFF_HEADS = 4
DIFF_HD = 64
DIFF_VD = 2 * DIFF_HD
RWKV_HD = 64
RWKV_HEADS = D_BRANCH // RWKV_HD
RWKV_W_RANK = 64
RWKV_A_RANK = 64
RWKV_G_RANK = 128
RWKV_GN_EPS = 64e-5
HGRN_HEADS = 4
HGRN_HD = D_BRANCH // HGRN_HEADS
DSA_HEADS = 8
DSA_HD = 64
IDX_HEADS = 8
IDX_HD = 64
TOPK_MAX = 256

A_WIDTHS = (DIFF_HEADS * 2 * DIFF_HD, DIFF_HEADS * 2 * DIFF_HD, DIFF_HEADS * DIFF_VD)
B_WIDTHS = (D_BRANCH, D_BRANCH, D_BRANCH, RWKV_W_RANK, RWKV_A_RANK, RWKV_G_RANK)
C_WIDTHS = (D_BRANCH, D_BRANCH, D_BRANCH, D_BRANCH)
D_WIDTHS = (DSA_HEADS * DSA_HD, DSA_HD, DSA_HD, IDX_HEADS * IDX_HD, IDX_HD, IDX_HEADS)
GROUP_WIDTHS = (sum(A_WIDTHS), sum(B_WIDTHS), sum(C_WIDTHS), sum(D_WIDTHS), N_BRANCH * D_MODEL)
D_IN = sum(GROUP_WIDTHS)

kernel_name = "chunk_causal_gated_hybrid_trunk"


def _split(t, widths):
    return jnp.split(t, [int(c) for c in np.cumsum(widths)[:-1]], axis=-1)


def rms_norm(x, g, eps=EPS):
    xf = x.astype(jnp.float32)
    y = xf * lax.rsqrt(jnp.mean(xf * xf, axis=-1, keepdims=True) + eps)
    return (y * g.astype(jnp.float32)).astype(x.dtype)


def rope(x, positions):
    d = x.shape[-1]
    inv = ROPE_THETA ** (-jnp.arange(0, d, 2, dtype=jnp.float32) / d)
    ang = positions.astype(jnp.float32)[..., None] * inv
    cos = jnp.cos(ang)[:, :, None, :]
    sin = jnp.sin(ang)[:, :, None, :]
    xf = x.astype(jnp.float32)
    x1, x2 = xf[..., : d // 2], xf[..., d // 2:]
    return jnp.concatenate([x1 * cos - x2 * sin, x2 * cos + x1 * sin], axis=-1).astype(x.dtype)


def chunk_visible(q_idx, k_idx):
    return (k_idx[None, :] // CHUNK) <= (q_idx[:, None] // CHUNK)


def sweep_query_blocks(fn, seq):
    nb = seq // Q_BLOCK
    out = lax.map(fn, jnp.arange(nb))
    out = jnp.moveaxis(out, 0, 1)
    return out.reshape(out.shape[0], nb * Q_BLOCK, *out.shape[3:])


def diff_attention(q, k, v, positions, lam_vecs, subln_g, layer):
    B, S, _ = q.shape
    q = rope(q.reshape(B, S, DIFF_HEADS * 2, DIFF_HD), positions).reshape(B, S, DIFF_HEADS, 2, DIFF_HD)
    k = rope(k.reshape(B, S, DIFF_HEADS * 2, DIFF_HD), positions).reshape(B, S, DIFF_HEADS, 2, DIFF_HD)
    v = v.reshape(B, S, DIFF_HEADS, DIFF_VD)
    lam_init = 0.8 - 0.6 * math.exp(-0.3 * layer)
    lv = lam_vecs.astype(jnp.float32)
    lam = jnp.exp(jnp.sum(lv[0] * lv[1])) - jnp.exp(jnp.sum(lv[2] * lv[3])) + lam_init
    scale = DIFF_HD ** -0.5
    k_idx = jnp.arange(S)

    def block(i):
        start = i * Q_BLOCK
        qb = lax.dynamic_slice_in_dim(q, start, Q_BLOCK, axis=1)
        s = jnp.einsum('bthmd,bshmd->bmhts', qb, k).astype(jnp.float32) * scale
        vis = chunk_visible(start + jnp.arange(Q_BLOCK), k_idx)
        p = jax.nn.softmax(jnp.where(vis, s, -jnp.inf), axis=-1)
        pd = p[:, 0] - lam * p[:, 1]
        return jnp.einsum('bhts,bshe->bthe', pd.astype(v.dtype), v)

    o = sweep_query_blocks(block, S)
    o = rms_norm(o, subln_g) * jnp.asarray(1.0 - lam_init, dtype=o.dtype)
    return o.reshape(B, S, DIFF_HEADS * DIFF_VD)


def token_shift(p, mu):
    prev = jnp.pad(p, ((0, 0), (1, 0), (0, 0)))[:, :-1]
    return p + (prev - p) * mu


def rwkv7(r, k, v, wd, ad, gd, w_up, a_up, g_up, w0, a0, kk_gain, ka_gain, r_k, gn_g, gn_b):
    B, S, _ = r.shape
    H, N = RWKV_HEADS, RWKV_HD
    f32 = lambda t: t.astype(jnp.float32)
    w_log = -jax.nn.softplus(-(f32(w0) + jnp.tanh(f32(wd)) @ f32(w_up))) - 0.5
    decay = jnp.exp(-jnp.exp(w_log))
    a = jax.nn.sigmoid(f32(a0) + f32(ad) @ f32(a_up))
    g = jax.nn.sigmoid(f32(gd)) @ f32(g_up)
    heads = lambda t: f32(t).reshape(B, S, H, N)
    rh, kh, vh, decay, a = heads(r), heads(k), heads(v), heads(decay), heads(a)
    kk = kh * f32(kk_gain).reshape(H, N)
    kk = kk / jnp.maximum(jnp.sqrt(jnp.sum(kk * kk, axis=-1, keepdims=True)), 1e-12)
    kh = kh * (1.0 + (a - 1.0) * f32(ka_gain).reshape(H, N))

    def step(state, inp):
        rt, wt, kt, vt, kkt, at = inp
        sa = jnp.einsum('bhvk,bhk->bhv', state, -kkt)
        state = (state * wt[:, :, None, :] + sa[..., None] * (kkt * at)[:, :, None, :]
                 + vt[..., None] * kt[:, :, None, :])
        return state, jnp.einsum('bhvk,bhk->bhv', state, rt)

    xs = tuple(jnp.moveaxis(t, 1, 0) for t in (rh, decay, kh, vh, kk, a))
    _, y = lax.scan(step, jnp.zeros((B, H, N, N), jnp.float32), xs)
    y = jnp.moveaxis(y, 0, 1)
    mu = jnp.mean(y, axis=-1, keepdims=True)
    var = jnp.mean(jnp.square(y - mu), axis=-1, keepdims=True)
    y = ((y - mu) * lax.rsqrt(var + RWKV_GN_EPS)).reshape(B, S, H * N) * f32(gn_g) + f32(gn_b)
    bonus = jnp.sum(rh * kh * f32(r_k), axis=-1, keepdims=True) * vh
    y = (y + bonus.reshape(B, S, H * N)) * g
    return y.astype(r.dtype)


def hgrn2(q, f_logit, i, g, lb, norm_g):
    B, S, _ = q.shape
    H, dk = HGRN_HEADS, HGRN_HD
    nc = S // CHUNK
    qf = jax.nn.silu(q.astype(jnp.float32))
    fgate = lb.astype(jnp.float32) + (1.0 - lb.astype(jnp.float32)) * jax.nn.sigmoid(f_logit.astype(jnp.float32))
    log_f = jnp.log(fgate)
    kf = 1.0 - fgate

    def chunks(t):
        return t.reshape(B, nc, CHUNK, H, -1).transpose(1, 0, 3, 2, 4)

    causal = jnp.tril(jnp.ones((CHUNK, CHUNK), dtype=bool))

    def step(state, inp):
        qc, kc, ic, lfc = inp
        b = jnp.cumsum(lfc, axis=2)
        diff = b[:, :, :, None, :] - b[:, :, None, :, :]
        dec = jnp.exp(jnp.where(causal[:, :, None], diff, -jnp.inf))
        attn = jnp.einsum('bhtc,bhsc,bhtsc->bhts', qc, kc, dec)
        o = attn @ ic + jnp.einsum('bhtc,bhcv->bhtv', qc * jnp.exp(b), state)
        b_last = b[:, :, -1:, :]
        state = (jnp.exp(b_last)[:, :, 0, :, None] * state
                 + jnp.einsum('bhsc,bhsv->bhcv', kc * jnp.exp(b_last - b), ic))
        return state, o

    xs = (chunks(qf), chunks(kf), chunks(i.astype(jnp.float32)), chunks(log_f))
    _, o = lax.scan(step, jnp.zeros((B, H, dk, HGRN_HD), jnp.float32), xs)
    o = o.transpose(1, 0, 3, 2, 4).reshape(B, S, H, HGRN_HD)
    o = rms_norm(o, norm_g.reshape(H, HGRN_HD)).reshape(B, S, H * HGRN_HD)
    return (o * jax.nn.silu(g.astype(jnp.float32))).astype(q.dtype)


def dsa_attention(q, k, v, iq, ik, iw, positions):
    B, S, _ = q.shape
    top_k = min(TOPK_MAX, S // 4)
    q = rope(q.reshape(B, S, DSA_HEADS, DSA_HD), positions)
    k = rope(k[:, :, None, :], positions)[:, :, 0]
    iq = rope(iq.reshape(B, S, IDX_HEADS, IDX_HD), positions)
    ik = rope(ik[:, :, None, :], positions)[:, :, 0]
    w = iw.astype(jnp.float32) * (IDX_HEADS ** -0.5) * (IDX_HD ** -0.5)
    k_idx = jnp.arange(S)
    gather = jax.vmap(lambda t, idx: t[idx])

    def block(i):
        start = i * Q_BLOCK
        q_idx = start + jnp.arange(Q_BLOCK)
        qb = lax.dynamic_slice_in_dim(q, start, Q_BLOCK, axis=1)
        iqb = lax.dynamic_slice_in_dim(iq, start, Q_BLOCK, axis=1)
        wb = lax.dynamic_slice_in_dim(w, start, Q_BLOCK, axis=1)
        idx_s = jnp.einsum('bthd,bsd->bths', iqb, ik).astype(jnp.float32)
        score = jnp.einsum('bths,bth->bts', jax.nn.relu(idx_s), wb)
        score = jnp.where(chunk_visible(q_idx, k_idx)[None], score, -jnp.inf)
        _, sel = lax.top_k(score, top_k)
        ok = (sel // CHUNK) <= (q_idx[None, :, None] // CHUNK)
        kg, vg = gather(k, sel), gather(v, sel)
        s = jnp.einsum('bthd,btkd->bhtk', qb, kg).astype(jnp.float32) * (DSA_HD ** -0.5)
        p = jax.nn.softmax(jnp.where(ok[:, None], s, -jnp.inf), axis=-1)
        return jnp.einsum('bhtk,btkd->bthd', p.astype(v.dtype), vg)

    o = sweep_query_blocks(block, S)
    return o.reshape(B, S, DSA_HEADS * DSA_HD)


def hybrid_layer(x, positions, layer, norm_g, w_in, diff_lambda, diff_subln_g, rwkv_mu, rwkv_w_up,
                 rwkv_a_up, rwkv_g_up, rwkv_w0, rwkv_a0, rwkv_k_k, rwkv_k_a, rwkv_r_k, rwkv_gn_g,
                 rwkv_gn_b, hgrn_lb, hgrn_norm_g, w_branch, w_out, mlp_w1, mlp_w2):
    B, S, D = x.shape
    h = rms_norm(x, norm_g[0])
    proj = h @ w_in
    pa, pb, pc, pd, gate = _split(proj, GROUP_WIDTHS)
    a_q, a_k, a_v = _split(pa, A_WIDTHS)
    b_r, b_k, b_v, b_wd, b_ad, b_gd = _split(token_shift(pb, rwkv_mu), B_WIDTHS)
    c_q, c_f, c_i, c_g = _split(pc, C_WIDTHS)
    d_q, d_k, d_v, d_iq, d_ik, d_iw = _split(pd, D_WIDTHS)

    y_a = diff_attention(a_q, a_k, a_v, positions, diff_lambda, diff_subln_g, layer)
    y_b = rwkv7(b_r, b_k, b_v, b_wd, b_ad, b_gd, rwkv_w_up, rwkv_a_up, rwkv_g_up, rwkv_w0, rwkv_a0,
                rwkv_k_k, rwkv_k_a, rwkv_r_k, rwkv_gn_g, rwkv_gn_b)
    y_c = hgrn2(c_q, c_f, c_i, c_g, hgrn_lb, hgrn_norm_g)
    y_d = dsa_attention(d_q, d_k, d_v, d_iq, d_ik, d_iw, positions)

    ys = (y_a, y_b, y_c, y_d)
    gates = _split(gate, (D_MODEL,) * N_BRANCH)
    merged = jnp.zeros_like(x)
    for n in range(N_BRANCH):
        gn = jax.nn.sigmoid(gates[n].astype(jnp.float32)).astype(x.dtype)
        merged = merged + gn * (ys[n] @ w_branch[n])
    x = x + rms_norm(merged @ w_out, norm_g[1])

    h2 = rms_norm(x, norm_g[2])
    ff = jnp.square(jax.nn.relu(h2 @ mlp_w1)) @ mlp_w2
    return x + rms_norm(ff, norm_g[3])


def setup_inputs(seed: int = 0) -> dict:
    key = jax.random.key(seed)
    ks = jax.random.split(key, 24)
    nrm = lambda k, shape, s: jax.random.normal(k, shape, jnp.float32) * s
    offset = jax.random.randint(ks[1], (BATCH,), 0, 64, dtype=jnp.int32) * CHUNK
    positions = offset[:, None] + jnp.arange(SEQ, dtype=jnp.int32)[None, :]
    return {
        "x": nrm(ks[0], (BATCH, SEQ, D_MODEL), 1.0),
        "positions": positions,
        "norm_g": 1.0 + nrm(ks[2], (DEPTH, 4, D_MODEL), 0.02),
        "w_in": nrm(ks[3], (DEPTH, D_MODEL, D_IN), D_MODEL ** -0.5),
        "diff_lambda": nrm(ks[4], (DEPTH, 4, DIFF_HD), 0.1),
        "diff_subln_g": 1.0 + nrm(ks[5], (DEPTH, DIFF_VD), 0.02),
        "rwkv_mu": jax.random.uniform(ks[6], (DEPTH, sum(B_WIDTHS)), jnp.float32),
        "rwkv_w_up": nrm(ks[7], (DEPTH, RWKV_W_RANK, D_BRANCH), 0.5 * RWKV_W_RANK ** -0.5),
        "rwkv_a_up": nrm(ks[8], (DEPTH, RWKV_A_RANK, D_BRANCH), 0.5 * RWKV_A_RANK ** -0.5),
        "rwkv_g_up": nrm(ks[9], (DEPTH, RWKV_G_RANK, D_BRANCH), RWKV_G_RANK ** -0.5),
        "rwkv_w0": jax.random.uniform(ks[10], (DEPTH, D_BRANCH), jnp.float32, -6.0, -1.0),
        "rwkv_a0": nrm(ks[11], (DEPTH, D_BRANCH), 0.1),
        "rwkv_k_k": 0.85 + nrm(ks[12], (DEPTH, D_BRANCH), 0.02),
        "rwkv_k_a": 1.0 + nrm(ks[13], (DEPTH, D_BRANCH), 0.02),
        "rwkv_r_k": nrm(ks[14], (DEPTH, RWKV_HEADS, RWKV_HD), 0.1),
        "rwkv_gn_g": 1.0 + nrm(ks[15], (DEPTH, D_BRANCH), 0.02),
        "rwkv_gn_b": nrm(ks[16], (DEPTH, D_BRANCH), 0.01),
        "hgrn_lb_logits": nrm(ks[17], (DEPTH, D_BRANCH), 0.1),
        "hgrn_norm_g": 1.0 + nrm(ks[18], (DEPTH, D_BRANCH), 0.02),
        "w_branch": nrm(ks[19], (DEPTH, N_BRANCH, D_BRANCH, D_MODEL), D_BRANCH ** -0.5),
        "w_out": nrm(ks[20], (DEPTH, D_MODEL, D_MODEL), D_MODEL ** -0.5),
        "mlp_w1": nrm(ks[21], (DEPTH, D_MODEL, D_FF), D_MODEL ** -0.5),
        "mlp_w2": nrm(ks[22], (DEPTH, D_FF, D_MODEL), D_FF ** -0.5),
    }


def reference(x, positions, norm_g, w_in, diff_lambda, diff_subln_g, rwkv_mu, rwkv_w_up, rwkv_a_up,
              rwkv_g_up, rwkv_w0, rwkv_a0, rwkv_k_k, rwkv_k_a, rwkv_r_k, rwkv_gn_g, rwkv_gn_b,
              hgrn_lb_logits, hgrn_norm_g, w_branch, w_out, mlp_w1, mlp_w2):
    lb = jax.nn.softmax(hgrn_lb_logits.astype(jnp.float32), axis=0)
    lb = jnp.cumsum(lb, axis=0) - lb[0]
    for l in range(DEPTH):
        x = hybrid_layer(x, positions, l, norm_g[l], w_in[l], diff_lambda[l], diff_subln_g[l], rwkv_mu[l],
                         rwkv_w_up[l], rwkv_a_up[l], rwkv_g_up[l], rwkv_w0[l], rwkv_a0[l], rwkv_k_k[l],
                         rwkv_k_a[l], rwkv_r_k[l], rwkv_gn_g[l], rwkv_gn_b[l], lb[l], hgrn_norm_g[l],
                         w_branch[l], w_out[l], mlp_w1[l], mlp_w2[l])
    return x
```

```python
import functools
import math

import numpy as np
import jax
import jax.numpy as jnp
from jax import lax
from jax.experimental import pallas as pl
from jax.experimental.pallas import tpu as pltpu

F32 = jnp.float32
BF16 = jnp.bfloat16

D_MODEL = 2048
D_BRANCH = 512
D_FF = 8192
N_BRANCH = 4
CHUNK = 64
EPS = 1e-6
ROPE_THETA = 10000.0
HEAD_DIM = 64
DIFF_HEADS = 4
RWKV_GN_EPS = 64e-5
DSA_HEADS = 8
IDX_HEADS = 8
TOPK_MAX = 256

LANE = 128
VMEM_LIMIT = 56 * 1024 * 1024

U_B, U_DKV, U_DIKW = 0, 14, 15
U_AQ, U_AK, U_AV = 16, 20, 24
U_CQ, U_CF, U_CI, U_CG = 28, 32, 36, 40
U_DQ, U_DIQ = 44, 48
N_MAIN = 52 * LANE
B_WIDTH = 1792

NEG = -1e30
INT_MIN = -2147483648

NT_DIMS = (((1,), (1,)), ((), ()))
TN_DIMS = (((0,), (0,)), ((), ()))


def _cp(sem, vmem=VMEM_LIMIT):
    return pltpu.CompilerParams(dimension_semantics=sem, vmem_limit_bytes=vmem)


def _dot(a, b, dims=None):
    if dims is None:
        return jnp.dot(a, b, preferred_element_type=F32)
    return lax.dot_general(a, b, dims, preferred_element_type=F32)


def _dot_hi(a, b, dims=(((1,), (0,)), ((), ()))):
    return lax.dot_general(a, b, dims, preferred_element_type=F32,
                           precision=lax.Precision.HIGHEST)


def _rmsnorm_kernel(x_ref, g_ref, o_ref):
    x = x_ref[...]
    ms = jnp.mean(x * x, axis=-1, keepdims=True)
    o_ref[...] = (x * lax.rsqrt(ms + EPS) * g_ref[...]).astype(o_ref.dtype)


def rmsnorm_bf16(x, g, tm=512):
    m, d = x.shape
    return pl.pallas_call(
        _rmsnorm_kernel,
        grid=(m // tm,),
        in_specs=[pl.BlockSpec((tm, d), lambda i: (i, 0)),
                  pl.BlockSpec((1, d), lambda i: (0, 0))],
        out_specs=pl.BlockSpec((tm, d), lambda i: (i, 0)),
        out_shape=jax.ShapeDtypeStruct((m, d), BF16),
        compiler_params=_cp(("parallel",)),
        name="rmsnorm",
    )(x, g.reshape(1, d))


def _mm_kernel(x_ref, w_ref, o_ref, *, act):
    acc = _dot(x_ref[...], w_ref[...])
    if act == "sigmoid":
        acc = jax.nn.sigmoid(acc)
    elif act == "relu2":
        acc = jnp.square(jnp.maximum(acc, 0.0))
    o_ref[...] = acc.astype(o_ref.dtype)


def matmul(x, w, act, out_dtype, tm=1024, tn=512, name="mm"):
    m, k = x.shape
    n = w.shape[1]
    return pl.pallas_call(
        functools.partial(_mm_kernel, act=act),
        grid=(n // tn, m // tm),
        in_specs=[pl.BlockSpec((tm, k), lambda j, i: (i, 0)),
                  pl.BlockSpec((k, tn), lambda j, i: (0, j))],
        out_specs=pl.BlockSpec((tm, tn), lambda j, i: (i, j)),
        out_shape=jax.ShapeDtypeStruct((m, n), out_dtype),
        compiler_params=_cp(("parallel", "parallel")),
        name=name,
    )(x, w)


def _mm_norm_res_kernel(a_ref, w_ref, x_ref, g_ref, o_ref, acc_ref):
    kk = pl.program_id(1)

    @pl.when(kk == 0)
    def _():
        acc_ref[...] = jnp.zeros_like(acc_ref)

    acc_ref[...] += _dot(a_ref[...], w_ref[...])

    @pl.when(kk == pl.num_programs(1) - 1)
    def _():
        y = acc_ref[...]
        ms = jnp.mean(y * y, axis=-1, keepdims=True)
        o_ref[...] = x_ref[...] + y * lax.rsqrt(ms + EPS) * g_ref[...]


def matmul_norm_residual(a, w, x, g, tm=512, tk=1024, name="mm_norm_res"):
    m, k = a.shape
    n = w.shape[1]
    return pl.pallas_call(
        _mm_norm_res_kernel,
        grid=(m // tm, k // tk),
        in_specs=[pl.BlockSpec((tm, tk), lambda i, kk: (i, kk)),
                  pl.BlockSpec((tk, n), lambda i, kk: (kk, 0)),
                  pl.BlockSpec((tm, n), lambda i, kk: (i, 0)),
                  pl.BlockSpec((1, n), lambda i, kk: (0, 0))],
        out_specs=pl.BlockSpec((tm, n), lambda i, kk: (i, 0)),
        out_shape=jax.ShapeDtypeStruct((m, n), F32),
        scratch_shapes=[pltpu.VMEM((tm, n), F32)],
        compiler_params=_cp(("parallel", "arbitrary")),
        name=name,
    )(a, w, x, g.reshape(1, n))


def _merge_kernel(ya, yb, yc, yd, wb, ga, gb, gc, gd, o_ref):
    acc = None
    for n, (y, g) in enumerate(((ya, ga), (yb, gb), (yc, gc), (yd, gd))):
        t = g[...].astype(F32) * _dot(y[...], wb[n])
        acc = t if acc is None else acc + t
    o_ref[...] = acc.astype(o_ref.dtype)


def gated_merge(ys, wb, gate, tm=512, tn=1024):
    m = ys[0].shape[0]
    nj = D_MODEL // tn
    y_spec = pl.BlockSpec((tm, D_BRANCH), lambda j, i: (i, 0))
    g_specs = [pl.BlockSpec((tm, tn), functools.partial(lambda j, i, n: (i, n * nj + j), n=n))
               for n in range(N_BRANCH)]
    return pl.pallas_call(
        _merge_kernel,
        grid=(nj, m // tm),
        in_specs=[y_spec] * 4 + [pl.BlockSpec((N_BRANCH, D_BRANCH, tn), lambda j, i: (0, 0, j))] + g_specs,
        out_specs=pl.BlockSpec((tm, tn), lambda j, i: (i, j)),
        out_shape=jax.ShapeDtypeStruct((m, D_MODEL), BF16),
        compiler_params=_cp(("parallel", "parallel")),
        name="gated_merge",
    )(*ys, wb, gate, gate, gate, gate)


def _rope_table_kernel(pos_ref, inv_ref, sgn_ref, cos_ref, sin_ref):
    ang = pos_ref[...].astype(F32) * inv_ref[...]
    cos_ref[...] = jnp.cos(ang)
    sin_ref[...] = jnp.sin(ang) * sgn_ref[...]


def rope_tables(positions, tm=512):
    m = positions.size
    half = HEAD_DIM // 2
    inv = ROPE_THETA ** (-np.arange(0, HEAD_DIM, 2, dtype=np.float32) / HEAD_DIM)
    inv = np.tile(inv.astype(np.float32), 4).reshape(1, LANE)
    sgn = np.tile(np.concatenate([-np.ones(half, np.float32), np.ones(half, np.float32)]), 2).reshape(1, LANE)
    spec = pl.BlockSpec((tm, LANE), lambda i: (i, 0))
    cst = pl.BlockSpec((1, LANE), lambda i: (0, 0))
    return pl.pallas_call(
        _rope_table_kernel,
        grid=(m // tm,),
        in_specs=[pl.BlockSpec((tm, 1), lambda i: (i, 0)), cst, cst],
        out_specs=[spec, spec],
        out_shape=[jax.ShapeDtypeStruct((m, LANE), F32)] * 2,
        compiler_params=_cp(("parallel",)),
        name="rope_tables",
    )(positions.reshape(m, 1), jnp.asarray(inv), jnp.asarray(sgn))


def _rope(x, cos, sin):
    w = x.shape[1]
    n = w // LANE
    if n > 1:
        cos = jnp.concatenate([cos] * n, axis=1)
        sin = jnp.concatenate([sin] * n, axis=1)
    lane = lax.broadcasted_iota(jnp.int32, x.shape, 1)
    up = pltpu.roll(x, w - HEAD_DIM // 2, axis=1)
    dn = pltpu.roll(x, HEAD_DIM // 2, axis=1)
    rot = jnp.where((lane & (HEAD_DIM // 2)) == 0, up, dn)
    return x * cos + rot * sin


def _rope_qk_kernel(q_ref, k_ref, cos_ref, sin_ref, qo_ref, ko_ref, *, scale):
    cos = cos_ref[0]
    sin = sin_ref[0]
    qo_ref[0] = (_rope(q_ref[0], cos, sin) * scale).astype(qo_ref.dtype)
    ko_ref[0] = _rope(k_ref[0], cos, sin).astype(ko_ref.dtype)


def rope_qk(pm, cos, sin, uq, uk, scale, ts=512):
    b, s, _ = pm.shape
    wq = D_BRANCH
    tab = pl.BlockSpec((1, ts, LANE), lambda bi, i: (bi, i, 0))
    out = pl.BlockSpec((1, ts, wq), lambda bi, i: (bi, i, 0))
    return pl.pallas_call(
        functools.partial(_rope_qk_kernel, scale=scale),
        grid=(b, s // ts),
        in_specs=[pl.BlockSpec((1, ts, wq), lambda bi, i: (bi, i, uq * LANE // wq)),
                  pl.BlockSpec((1, ts, wq), lambda bi, i: (bi, i, uk * LANE // wq)),
                  tab, tab],
        out_specs=[out, out],
        out_shape=[jax.ShapeDtypeStruct((b, s, wq), BF16)] * 2,
        compiler_params=_cp(("parallel", "parallel")),
        name="rope_qk",
    )(pm, pm, cos, sin)


def _diff_attn_kernel(q_ref, k_ref, v_ref, lam_ref, g_ref, o_ref, m_ref, l_ref, acc_ref, *, tq, tk, lam_init):
    i = pl.program_id(2)
    j = pl.program_id(3)
    hd = HEAD_DIM

    @pl.when(j == 0)
    def _():
        m_ref[...] = jnp.full_like(m_ref, NEG)
        l_ref[...] = jnp.zeros_like(l_ref)
        acc_ref[...] = jnp.zeros_like(acc_ref)

    def step(masked):
        q = q_ref[0]
        k = k_ref[0]
        v = v_ref[0].astype(BF16)
        if masked:
            row = lax.broadcasted_iota(jnp.int32, (tq, tk), 0)
            col = lax.broadcasted_iota(jnp.int32, (tq, tk), 1)
            vis = (col // CHUNK) <= (row // CHUNK)
        for mm in range(2):
            s = _dot(q[:, mm * hd:(mm + 1) * hd], k[:, mm * hd:(mm + 1) * hd], NT_DIMS)
            if masked:
                s = jnp.where(vis, s, NEG)
            m_prev = m_ref[mm]
            m_new = jnp.maximum(m_prev, jnp.max(s, axis=1, keepdims=True))
            alpha = jnp.exp(m_prev - m_new)
            p = jnp.exp(s - m_new)
            l_ref[mm] = alpha * l_ref[mm] + jnp.sum(p, axis=1, keepdims=True)
            acc_ref[mm] = alpha * acc_ref[mm] + _dot(p.astype(BF16), v)
            m_ref[mm] = m_new

    @pl.when(j < i)
    def _():
        step(False)

    @pl.when(j == i)
    def _():
        step(True)
        lv = lam_ref[...]
        lam = (jnp.exp(jnp.sum(lv[0:1] * lv[1:2], keepdims=True))
               - jnp.exp(jnp.sum(lv[2:3] * lv[3:4], keepdims=True)) + lam_init)
        o = acc_ref[0] / l_ref[0] - lam * (acc_ref[1] / l_ref[1])
        ms = jnp.mean(o * o, axis=-1, keepdims=True)
        o = o * lax.rsqrt(ms + EPS) * g_ref[...] * (1.0 - lam_init)
        o_ref[0] = o.astype(o_ref.dtype)


def diff_attention(pm, cos, sin, lam_vecs, subln_g, layer, tq=512):
    b, s, _ = pm.shape
    tk = tq
    qa, ka = rope_qk(pm, cos, sin, U_AQ, U_AK, HEAD_DIM ** -0.5)
    lam_init = 0.8 - 0.6 * math.exp(-0.3 * layer)
    n = s // tq
    return pl.pallas_call(
        functools.partial(_diff_attn_kernel, tq=tq, tk=tk, lam_init=lam_init),
        grid=(b, DIFF_HEADS, n, n),
        in_specs=[pl.BlockSpec((1, tq, LANE), lambda bi, h, i, j: (bi, i, h)),
                  pl.BlockSpec((1, tk, LANE), lambda bi, h, i, j: (bi, jnp.minimum(j, i), h)),
                  pl.BlockSpec((1, tk, LANE), lambda bi, h, i, j: (bi, jnp.minimum(j, i), U_AV + h)),
                  pl.BlockSpec((4, HEAD_DIM), lambda bi, h, i, j: (0, 0)),
                  pl.BlockSpec((1, LANE), lambda bi, h, i, j: (0, 0))],
        out_specs=pl.BlockSpec((1, tq, LANE), lambda bi, h, i, j: (bi, i, h)),
        out_shape=jax.ShapeDtypeStruct((b, s, D_BRANCH), BF16),
        scratch_shapes=[pltpu.VMEM((2, tq, 1), F32), pltpu.VMEM((2, tq, 1), F32),
                        pltpu.VMEM((2, tq, LANE), F32)],
        compiler_params=_cp(("parallel", "parallel", "parallel", "arbitrary")),
        name="diff_attention",
    )(qa, ka, pm, lam_vecs, subln_g.reshape(1, LANE))


def _softplus(z):
    return jnp.maximum(z, 0.0) + jnp.log(1.0 + jnp.exp(-jnp.abs(z)))


def _rwkv_prep_kernel(x_ref, prev_ref, mu_ref, wup_ref, aup_ref, gup_ref, w0_ref, a0_ref, kkg_ref, kag_ref,
                      seg_ref, r_ref, k_ref, v_ref, al_ref, be_ref, lw_ref, g_ref):
    i = pl.program_id(1)
    p = x_ref[0]
    t = p.shape[0]
    row = lax.broadcasted_iota(jnp.int32, p.shape, 0)
    last = prev_ref[0][7:8, :]
    last = jnp.where(i == 0, jnp.zeros_like(last), last)
    prev = jnp.where(row == 0, jnp.broadcast_to(last, p.shape), pltpu.roll(p, 1, axis=0))
    ps = p + (prev - p) * mu_ref[...]
    d = D_BRANCH
    r, k, v = ps[:, 0:d], ps[:, d:2 * d], ps[:, 2 * d:3 * d]
    wd = ps[:, 3 * d:3 * d + 64]
    ad = ps[:, 3 * d + 64:3 * d + 128]
    gd = ps[:, 3 * d + 128:3 * d + 256]
    w_log = -_softplus(-(w0_ref[...] + _dot_hi(jnp.tanh(wd), wup_ref[...]))) - 0.5
    a = jax.nn.sigmoid(a0_ref[...] + _dot_hi(ad, aup_ref[...]))
    g = _dot_hi(jax.nn.sigmoid(gd), gup_ref[...])
    kk = k * kkg_ref[...]
    nrm = jnp.sqrt(_dot_hi(kk * kk, seg_ref[...]))
    kk = kk / jnp.maximum(nrm, 1e-12)
    r_ref[0] = r
    k_ref[0] = k * (1.0 + (a - 1.0) * kag_ref[...])
    v_ref[0] = v
    al_ref[0] = -kk
    be_ref[0] = kk * a
    lw_ref[0] = -jnp.exp(w_log)
    g_ref[0] = g


def _rwkv_scan_kernel(r_ref, k_ref, v_ref, al_ref, be_ref, lw_ref, g_ref, rk_ref, gng_ref, gnb_ref,
                      o_ref, st_ref, *, rows):
    t = pl.program_id(2)
    hd = HEAD_DIM
    nch = rows // CHUNK

    @pl.when(t == 0)
    def _():
        st_ref[...] = jnp.zeros_like(st_ref)

    lw = lw_ref[0]
    rowc = lax.broadcasted_iota(jnp.int32, lw.shape, 0) & (CHUNK - 1)
    cum = lw
    for dd in (1, 2, 4, 8, 16, 32):
        cum = cum + jnp.where(rowc >= dd, pltpu.roll(cum, dd, axis=0), 0.0)
    cl = jnp.concatenate(
        [jnp.broadcast_to(cum[(c + 1) * CHUNK - 1:(c + 1) * CHUNK, :], (CHUNK, LANE)) for c in range(nch)], axis=0)
    e_in = jnp.exp(cum)
    e_out = jnp.exp(-cum)
    e_end = jnp.exp(cl - cum)
    r2, k2, v2, al2, be2 = r_ref[0], k_ref[0], v_ref[0], al_ref[0], be_ref[0]
    at2 = al2 * jnp.exp(cum - lw)
    rt2 = r2 * e_in
    bt2 = be2 * e_out
    kt2 = k2 * e_out
    bh2 = be2 * e_end
    kh2 = k2 * e_end
    wl2 = jnp.exp(cl)

    row = lax.broadcasted_iota(jnp.int32, (rows, rows), 0)
    col = lax.broadcasted_iota(jnp.int32, (rows, rows), 1)
    same = (row >> 6) == (col >> 6)
    m_strict = same & (row > col)
    m_incl = same & (row >= col)
    eye = (row == col).astype(F32)
    r64 = lax.broadcasted_iota(jnp.int32, (hd, hd), 0)
    c64 = lax.broadcasted_iota(jnp.int32, (hd, hd), 1)
    g2 = g_ref[0]
    outs = []
    for hh in range(2):
        sl = slice(hh * hd, (hh + 1) * hd)
        at, rt, bt, kt = at2[:, sl], rt2[:, sl], bt2[:, sl], kt2[:, sl]
        bh, kh, v, r, kp = bh2[:, sl], kh2[:, sl], v2[:, sl], r2[:, sl], k2[:, sl]
        lhs = jnp.concatenate([at, rt], axis=0).astype(BF16)
        rhs = jnp.concatenate([bt, kt], axis=0).astype(BF16)
        gm = _dot(lhs, rhs, NT_DIMS)
        n_ab = jnp.where(m_strict, gm[:rows, :rows], 0.0)
        a_ak = jnp.where(m_strict, gm[:rows, rows:], 0.0).astype(BF16)
        a_rb = jnp.where(m_incl, gm[rows:, :rows], 0.0).astype(BF16)
        a_rk = jnp.where(m_incl, gm[rows:, rows:], 0.0).astype(BF16)
        n8 = jnp.where((row >> 3) == (col >> 3), n_ab, 0.0)
        n8b = n8.astype(BF16)
        n_2 = _dot(n8b, n8b)
        n2b = n_2.astype(BF16)
        n_3 = _dot(n2b, n8b)
        n_4 = _dot(n2b, n2b)
        tinv = eye + n8 + n_2 + n_3
        tinv = tinv + _dot(tinv.astype(BF16), n_4.astype(BF16))
        for sh in (3, 4, 5):
            nl = jnp.where(((row >> (sh + 1)) == (col >> (sh + 1))) & ((row >> sh) != (col >> sh)),
                           n_ab, 0.0).astype(BF16)
            tb = tinv.astype(BF16)
            tinv = tinv + _dot(_dot(tb, nl).astype(BF16), tb)
        tb = tinv.astype(BF16)
        vb = v.astype(BF16)
        akv = _dot(a_ak, vb)
        pmat = _dot(tb, jnp.concatenate([at, akv], axis=1).astype(BF16))
        qmat = _dot(a_rb, pmat.astype(BF16)) + jnp.concatenate([rt, _dot(a_rk, vb)], axis=1)
        st = st_ref[hh]
        ys = []
        for c in range(nch):
            cs = slice(c * CHUNK, (c + 1) * CHUNK)
            y = _dot_hi(qmat[cs, :hd], st, NT_DIMS) + qmat[cs, hd:]
            ys.append(y)
            mn = _dot(pmat[cs].astype(BF16), bh[cs].astype(BF16), TN_DIMS)
            wl = wl2[c * CHUNK:c * CHUNK + 1, sl]
            m_c = jnp.where(r64 == c64, jnp.broadcast_to(wl, (hd, hd)), 0.0) + mn[:hd]
            n_c = mn[hd:] + _dot(vb[cs], kh[cs].astype(BF16), TN_DIMS)
            st = _dot_hi(st, m_c) + n_c
        st_ref[hh] = st
        y = jnp.concatenate(ys, axis=0)
        mu = jnp.mean(y, axis=-1, keepdims=True)
        var = jnp.mean(jnp.square(y - mu), axis=-1, keepdims=True)
        yn = (y - mu) * lax.rsqrt(var + RWKV_GN_EPS) * gng_ref[:, sl] + gnb_ref[:, sl]
        bonus = jnp.sum(r * kp * rk_ref[0, hh:hh + 1, :], axis=-1, keepdims=True) * v
        outs.append((yn + bonus) * g2[:, sl])
    o_ref[0] = jnp.concatenate(outs, axis=1).astype(o_ref.dtype)


def rwkv7(pm, mu, w_up, a_up, g_up, w0, a0, kk_gain, ka_gain, r_k, gn_g, gn_b, ts=256, rows=256):
    b, s, _ = pm.shape
    d = D_BRANCH
    seg = np.kron(np.eye(d // HEAD_DIM, dtype=np.float32), np.ones((HEAD_DIM, HEAD_DIM), np.float32))
    row1 = lambda a: a.reshape(1, -1)
    cst = lambda shape: pl.BlockSpec(shape, lambda bi, i: (0,) * len(shape))
    blk = pl.BlockSpec((1, ts, d), lambda bi, i: (bi, i, 0))
    r, k, v, al, be, lw, g = pl.pallas_call(
        _rwkv_prep_kernel,
        grid=(b, s // ts),
        in_specs=[pl.BlockSpec((1, ts, B_WIDTH), lambda bi, i: (bi, i, 0)),
                  pl.BlockSpec((1, 8, B_WIDTH), lambda bi, i: (bi, jnp.maximum(i * (ts // 8) - 1, 0), 0)),
                  cst((1, B_WIDTH)), cst((64, d)), cst((64, d)), cst((128, d)),
                  cst((1, d)), cst((1, d)), cst((1, d)), cst((1, d)), cst((d, d))],
        out_specs=[blk] * 7,
        out_shape=[jax.ShapeDtypeStruct((b, s, d), F32)] * 7,
        compiler_params=_cp(("parallel", "arbitrary")),
        name="rwkv_prep",
    )(pm, pm, row1(mu), w_up, a_up, g_up, row1(w0), row1(a0), row1(kk_gain), row1(ka_gain), jnp.asarray(seg))

    hp = pl.BlockSpec((1, rows, LANE), lambda bi, h, t: (bi, t, h))
    par = pl.BlockSpec((1, LANE), lambda bi, h, t: (0, h))
    return pl.pallas_call(
        functools.partial(_rwkv_scan_kernel, rows=rows),
        grid=(b, d // LANE, s // rows),
        in_specs=[hp] * 7 + [pl.BlockSpec((1, 2, HEAD_DIM), lambda bi, h, t: (h, 0, 0)), par, par],
        out_specs=hp,
        out_shape=jax.ShapeDtypeStruct((b, s, d), BF16),
        scratch_shapes=[pltpu.VMEM((2, HEAD_DIM, HEAD_DIM), F32)],
        compiler_params=_cp(("parallel", "parallel", "arbitrary")),
        name="rwkv_scan",
    )(r, k, v, al, be, lw, g, r_k.reshape(-1, 2, HEAD_DIM), row1(gn_g), row1(gn_b))


def _hgrn_kernel(q_ref, f_ref, i_ref, g_ref, lb_ref, ng_ref, o_ref, st_ref, *, rows):
    t = pl.program_id(2)
    sub = 16

    @pl.when(t == 0)
    def _():
        st_ref[...] = jnp.zeros_like(st_ref)

    q = jax.nn.silu(q_ref[0])
    lb = lb_ref[...]
    fg = lb + (1.0 - lb) * jax.nn.sigmoid(f_ref[0])
    lf = jnp.log(fg)
    kf = 1.0 - fg
    iv = i_ref[0]
    rowc = lax.broadcasted_iota(jnp.int32, lf.shape, 0) & (CHUNK - 1)
    bcum = lf
    for dd in (1, 2, 4, 8, 16, 32):
        bcum = bcum + jnp.where(rowc >= dd, pltpu.roll(bcum, dd, axis=0), 0.0)
    ivb = iv.astype(BF16)
    rsub = lax.broadcasted_iota(jnp.int32, (sub, LANE), 0)
    st = st_ref[...]
    outs = []
    for c in range(rows // CHUNK):
        c0 = c * CHUNK
        bc = bcum[c0:c0 + CHUNK]
        qc = q[c0:c0 + CHUNK]
        kc = kf[c0:c0 + CHUNK]
        ic = iv[c0:c0 + CHUNK]
        icb = ivb[c0:c0 + CHUNK]
        o_inter = _dot((qc * jnp.exp(bc)).astype(BF16), st.astype(BF16), NT_DIMS)
        blocks = []
        for ib in range(CHUNK // sub):
            r0 = ib * sub
            bi = bc[r0:r0 + sub]
            qi = qc[r0:r0 + sub]
            ki = kc[r0:r0 + sub]
            ii = ic[r0:r0 + sub]
            acc = jnp.zeros((sub, LANE), F32)
            for s_ in range(sub):
                e = jnp.where(rsub >= s_, jnp.exp(jnp.minimum(bi - bi[s_:s_ + 1], 0.0)), 0.0)
                a = jnp.sum(qi * (ki[s_:s_ + 1] * e), axis=-1, keepdims=True)
                acc = acc + a * ii[s_:s_ + 1]
            if ib > 0:
                ref_row = bc[r0 - 1:r0]
                qt = (qi * jnp.exp(bi - ref_row)).astype(BF16)
                kt = (kc[0:r0] * jnp.exp(ref_row - bc[0:r0])).astype(BF16)
                a_off = _dot(qt, kt, NT_DIMS).astype(BF16)
                acc = acc + _dot(a_off, icb[0:r0])
            blocks.append(acc)
        outs.append(o_inter + jnp.concatenate(blocks, axis=0))
        bl = bc[CHUNK - 1:CHUNK]
        kdec = (kc * jnp.exp(bl - bc)).astype(BF16)
        st = st * jnp.exp(bl) + _dot(icb, kdec, TN_DIMS)
    st_ref[...] = st
    o = jnp.concatenate(outs, axis=0)
    ms = jnp.mean(o * o, axis=-1, keepdims=True)
    o = o * lax.rsqrt(ms + EPS) * ng_ref[...]
    o_ref[0] = (o * jax.nn.silu(g_ref[0])).astype(o_ref.dtype)


def hgrn2(pm, lb, norm_g, rows=256):
    b, s, _ = pm.shape
    heads = D_BRANCH // LANE
    spec = lambda u: pl.BlockSpec((1, rows, LANE), lambda bi, h, t: (bi, t, u + h))
    par = pl.BlockSpec((1, LANE), lambda bi, h, t: (0, h))
    return pl.pallas_call(
        functools.partial(_hgrn_kernel, rows=rows),
        grid=(b, heads, s // rows),
        in_specs=[spec(U_CQ), spec(U_CF), spec(U_CI), spec(U_CG), par, par],
        out_specs=spec(0),
        out_shape=jax.ShapeDtypeStruct((b, s, D_BRANCH), BF16),
        scratch_shapes=[pltpu.VMEM((LANE, LANE), F32)],
        compiler_params=_cp(("parallel", "parallel", "arbitrary")),
        name="hgrn2",
    )(pm, pm, pm, pm, lb.reshape(1, -1), norm_g.reshape(1, -1))


def _dsa_prep_kernel(q_ref, iq_ref, kv_ref, ikw_ref, cos_ref, sin_ref,
                     qo_ref, iqo_ref, ko_ref, vo_ref, iko_ref, wo_ref, *, scale, wscale):
    cos = cos_ref[0]
    sin = sin_ref[0]
    hd = HEAD_DIM
    q = (_rope(q_ref[0], cos, sin) * scale).astype(BF16)
    iq = _rope(iq_ref[0], cos, sin).astype(BF16)
    for h in range(DSA_HEADS):
        qo_ref[0, h] = q[:, h * hd:(h + 1) * hd]
        iqo_ref[0, h] = iq[:, h * hd:(h + 1) * hd]
    kv = kv_ref[0]
    ko_ref[0] = _rope(kv, cos, sin)[:, :hd].astype(BF16)
    vo_ref[0] = kv[:, hd:].astype(BF16)
    ikw = ikw_ref[0]
    iko_ref[0] = _rope(ikw, cos, sin)[:, :hd].astype(BF16)
    wo_ref[0] = ikw[:, hd:hd + IDX_HEADS] * wscale


def _to_key(x):
    x = jnp.where(x == 0.0, 0.0, x)
    bits = pltpu.bitcast(x, jnp.int32)
    return jnp.where(bits < 0, bits ^ jnp.int32(0x7FFFFFFF), bits)


def _dsa_index_kernel(iq_ref, ik_ref, w_ref, mask_ref, key_ref, *, tq, tk, top_k):
    i = pl.program_id(1)
    nk = key_ref.shape[0]
    nvis = (i * tq + tq - 1) // tk + 1
    row_g = i * tq + lax.broadcasted_iota(jnp.int32, (tq, tk), 0)
    col_l = lax.broadcasted_iota(jnp.int32, (tq, tk), 1)
    w = w_ref[0]
    nsub = tk // LANE

    def score_block(jb, carry):
        ikb = ik_ref[0, pl.ds(pl.multiple_of(jb * tk, tk), tk), :]
        acc = jnp.zeros((tq, tk), F32)
        for h in range(IDX_HEADS):
            s = _dot(iq_ref[0, h], ikb, NT_DIMS)
            acc = acc + jnp.maximum(s, 0.0) * w[:, h:h + 1]
        vis = ((jb * tk + col_l) // CHUNK) <= (row_g // CHUNK)
        key_ref[jb] = jnp.where(vis, _to_key(acc), INT_MIN)
        return carry

    lax.fori_loop(0, nvis, score_block, 0)

    def lane_fold(x):
        out = x[:, 0:LANE]
        for c in range(1, nsub):
            out = out + x[:, c * LANE:(c + 1) * LANE]
        return out

    def count(pred_fn):
        def blk(jb, cnt):
            return cnt + lane_fold(jnp.where(pred_fn(key_ref[jb], jb), 1, 0))
        cnt = lax.fori_loop(0, nvis, blk, jnp.zeros((tq, LANE), jnp.int32))
        return jnp.sum(cnt, axis=1, keepdims=True)

    def bit_step(bi, prefix):
        cand = prefix | lax.shift_left(jnp.int32(1), 31 - bi)
        cand_s = cand ^ jnp.int32(INT_MIN)
        c = count(lambda kb, jb: kb >= cand_s)
        return jnp.where(c >= top_k, cand, prefix)

    prefix = lax.fori_loop(0, 32, bit_step, jnp.zeros((tq, 1), jnp.int32))
    tau = prefix ^ jnp.int32(INT_MIN)
    n_gt = count(lambda kb, jb: kb > tau)
    n_eq = count(lambda kb, jb: kb == tau)
    need = top_k - n_gt
    tied = (n_eq > need) & (tau != INT_MIN)
    n_cols = nk * tk

    idx_bits = int(n_cols).bit_length()

    def tie_break():
        def idx_step(bi, pre):
            cand = pre | lax.shift_left(jnp.int32(1), idx_bits - 1 - bi)
            c = count(lambda kb, jb: (kb == tau) & ((jb * tk + col_l) < cand))
            return jnp.where(c < need, cand, pre)
        cut = lax.fori_loop(0, idx_bits, idx_step, jnp.zeros((tq, 1), jnp.int32))
        return jnp.where(tied, cut, n_cols)

    cut = lax.cond(jnp.max(tied.astype(jnp.int32)) > 0, tie_break,
                   lambda: jnp.full((tq, 1), n_cols, jnp.int32))

    def write_block(jb, carry):
        kb = key_ref[jb]
        sel = (kb > tau) | ((kb == tau) & ((jb * tk + col_l) <= cut))
        sel = sel & (kb != INT_MIN)
        mask_ref[0, jb] = jnp.where(sel, 1.0, 0.0).astype(mask_ref.dtype)
        return carry

    def zero_block(jb, carry):
        mask_ref[0, jb] = jnp.zeros((tq, tk), mask_ref.dtype)
        return carry

    lax.fori_loop(0, nvis, write_block, 0)
    lax.fori_loop(nvis, nk, zero_block, 0)


def _dsa_attn_kernel(q_ref, k_ref, v_ref, mask_ref, o_ref, m_ref, l_ref, acc_ref, *, tq, tk):
    i = pl.program_id(1)
    j = pl.program_id(2)
    nh = DSA_HEADS
    hd = HEAD_DIM
    j_last = (i * tq + tq - 1) // tk

    @pl.when(j == 0)
    def _():
        m_ref[...] = jnp.full_like(m_ref, NEG)
        l_ref[...] = jnp.zeros_like(l_ref)
        acc_ref[...] = jnp.zeros_like(acc_ref)

    @pl.when(j <= j_last)
    def _():
        q = q_ref[0].reshape(nh * tq, hd)
        s = _dot(q, k_ref[0], NT_DIMS).reshape(nh, tq, tk)
        sel = (mask_ref[0, 0] > 0)[None]
        s = jnp.where(sel, s, NEG)
        m_prev = m_ref[...]
        m_new = jnp.maximum(m_prev, jnp.max(s, axis=2, keepdims=True))
        alpha = jnp.exp(m_prev - m_new)
        p = jnp.where(sel, jnp.exp(s - m_new), 0.0)
        l_ref[...] = alpha * l_ref[...] + jnp.sum(p, axis=2, keepdims=True)
        pv = _dot(p.astype(BF16).reshape(nh * tq, tk), v_ref[0]).reshape(nh, tq, hd)
        acc_ref[...] = alpha * acc_ref[...] + pv
        m_ref[...] = m_new

    @pl.when(j == j_last)
    def _():
        o = acc_ref[...] / l_ref[...]
        o_ref[0] = jnp.concatenate([o[h] for h in range(nh)], axis=1).astype(o_ref.dtype)


def dsa_attention(pm, cos, sin, ts=512, tq_i=128, tq=256, tk=512):
    b, s, _ = pm.shape
    tk_i = tk
    top_k = min(TOPK_MAX, s // 4)
    hd = HEAD_DIM
    wide = lambda u: pl.BlockSpec((1, ts, D_BRANCH), lambda bi, i: (bi, i, u * LANE // D_BRANCH))
    one = lambda u: pl.BlockSpec((1, ts, LANE), lambda bi, i: (bi, i, u))
    heads_out = pl.BlockSpec((1, DSA_HEADS, ts, hd), lambda bi, i: (bi, 0, i, 0))
    narrow = lambda w: pl.BlockSpec((1, ts, w), lambda bi, i: (bi, i, 0))
    q, iq, k, v, ik, w = pl.pallas_call(
        functools.partial(_dsa_prep_kernel, scale=hd ** -0.5, wscale=(IDX_HEADS ** -0.5) * (hd ** -0.5)),
        grid=(b, s // ts),
        in_specs=[wide(U_DQ), wide(U_DIQ), one(U_DKV), one(U_DIKW), one(0), one(0)],
        out_specs=[heads_out, heads_out, narrow(hd), narrow(hd), narrow(hd), narrow(IDX_HEADS)],
        out_shape=[jax.ShapeDtypeStruct((b, DSA_HEADS, s, hd), BF16)] * 2
                  + [jax.ShapeDtypeStruct((b, s, hd), BF16)] * 3
                  + [jax.ShapeDtypeStruct((b, s, IDX_HEADS), F32)],
        compiler_params=_cp(("parallel", "parallel")),
        name="dsa_prep",
    )(pm, pm, pm, pm, cos, sin)

    mask = pl.pallas_call(
        functools.partial(_dsa_index_kernel, tq=tq_i, tk=tk_i, top_k=top_k),
        grid=(b, s // tq_i),
        in_specs=[pl.BlockSpec((1, IDX_HEADS, tq_i, hd), lambda bi, i: (bi, 0, i, 0)),
                  pl.BlockSpec((1, s, hd), lambda bi, i: (bi, 0, 0)),
                  pl.BlockSpec((1, tq_i, IDX_HEADS), lambda bi, i: (bi, i, 0))],
        out_specs=pl.BlockSpec((1, s // tk_i, tq_i, tk_i), lambda bi, i: (bi, 0, i, 0)),
        out_shape=jax.ShapeDtypeStruct((b, s // tk_i, s, tk_i), BF16),
        scratch_shapes=[pltpu.VMEM((s // tk_i, tq_i, tk_i), jnp.int32)],
        compiler_params=_cp(("parallel", "parallel")),
        name="dsa_index",
    )(iq, ik, w)

    last = lambda i: (i * tq + tq - 1) // tk
    return pl.pallas_call(
        functools.partial(_dsa_attn_kernel, tq=tq, tk=tk),
        grid=(b, s // tq, s // tk),
        in_specs=[pl.BlockSpec((1, DSA_HEADS, tq, hd), lambda bi, i, j: (bi, 0, i, 0)),
                  pl.BlockSpec((1, tk, hd), lambda bi, i, j: (bi, jnp.minimum(j, last(i)), 0)),
                  pl.BlockSpec((1, tk, hd), lambda bi, i, j: (bi, jnp.minimum(j, last(i)), 0)),
                  pl.BlockSpec((1, 1, tq, tk), lambda bi, i, j: (bi, jnp.minimum(j, last(i)), i, 0))],
        out_specs=pl.BlockSpec((1, tq, D_BRANCH), lambda bi, i, j: (bi, i, 0)),
        out_shape=jax.ShapeDtypeStruct((b, s, D_BRANCH), BF16),
        scratch_shapes=[pltpu.VMEM((DSA_HEADS, tq, 1), F32), pltpu.VMEM((DSA_HEADS, tq, 1), F32),
                        pltpu.VMEM((DSA_HEADS, tq, hd), F32)],
        compiler_params=_cp(("parallel", "parallel", "arbitrary")),
        name="dsa_attention",
    )(q, k, v, mask)


def _split_w_in(w_in):
    c = lambda a, b_: w_in[:, a:b_]
    pad = jnp.zeros((w_in.shape[0], LANE - 64 - IDX_HEADS), w_in.dtype)
    main = jnp.concatenate([
        c(1536, 3328),
        c(5888, 6016),
        c(6528, 6600), pad,
        c(0, 1536),
        c(3328, 5376),
        c(5376, 5888), c(6016, 6528),
    ], axis=1).astype(BF16)
    gate = c(6600, 6600 + N_BRANCH * D_MODEL).astype(BF16)
    return main, gate


def _layer(x2, b, s, cos, sin, layer, norm_g, w_in, diff_lambda, diff_subln_g, rwkv_mu, rwkv_w_up, rwkv_a_up,
           rwkv_g_up, rwkv_w0, rwkv_a0, rwkv_k_k, rwkv_k_a, rwkv_r_k, rwkv_gn_g, rwkv_gn_b, hgrn_lb, hgrn_norm_g,
           w_branch, w_out, mlp_w1, mlp_w2):
    w_main, w_gate = _split_w_in(w_in)
    h = rmsnorm_bf16(x2, norm_g[0])
    pm = matmul(h, w_main, None, F32, name="proj_main").reshape(b, s, N_MAIN)
    gate = matmul(h, w_gate, "sigmoid", BF16, name="proj_gate")
    cos3 = cos.reshape(b, s, LANE)
    sin3 = sin.reshape(b, s, LANE)
    y_a = diff_attention(pm, cos3, sin3, diff_lambda, diff_subln_g, layer)
    y_b = rwkv7(pm, rwkv_mu, rwkv_w_up, rwkv_a_up, rwkv_g_up, rwkv_w0, rwkv_a0, rwkv_k_k, rwkv_k_a,
                rwkv_r_k, rwkv_gn_g, rwkv_gn_b)
    y_c = hgrn2(pm, hgrn_lb, hgrn_norm_g)
    y_d = dsa_attention(pm, cos3, sin3)
    m = b * s
    ys = [y.reshape(m, D_BRANCH) for y in (y_a, y_b, y_c, y_d)]
    merged = gated_merge(ys, w_branch.astype(BF16), gate)
    x2 = matmul_norm_residual(merged, w_out.astype(BF16), x2, norm_g[1], name="out_proj")
    h2 = rmsnorm_bf16(x2, norm_g[2])
    ff = matmul(h2, mlp_w1.astype(BF16), "relu2", BF16, name="mlp_up")
    return matmul_norm_residual(ff, mlp_w2.astype(BF16), x2, norm_g[3], name="mlp_down")


def kernel(x, positions, norm_g, w_in, diff_lambda, diff_subln_g, rwkv_mu, rwkv_w_up, rwkv_a_up, rwkv_g_up, rwkv_w0, rwkv_a0, rwkv_k_k, rwkv_k_a, rwkv_r_k, rwkv_gn_g, rwkv_gn_b, hgrn_lb_logits, hgrn_norm_g, w_branch, w_out, mlp_w1, mlp_w2):
    b, s, d = x.shape
    depth = w_in.shape[0]
    lb = jax.nn.softmax(hgrn_lb_logits.astype(F32), axis=0)
    lb = jnp.cumsum(lb, axis=0) - lb[0]
    cos, sin = rope_tables(positions)
    x2 = x.reshape(b * s, d)
    for l in range(depth):
        x2 = _layer(x2, b, s, cos, sin, l, norm_g[l], w_in[l], diff_lambda[l], diff_subln_g[l], rwkv_mu[l],
                    rwkv_w_up[l], rwkv_a_up[l], rwkv_g_up[l], rwkv_w0[l], rwkv_a0[l], rwkv_k_k[l], rwkv_k_a[l],
                    rwkv_r_k[l], rwkv_gn_g[l], rwkv_gn_b[l], lb[l], hgrn_norm_g[l], w_branch[l], w_out[l],
                    mlp_w1[l], mlp_w2[l])
    return x2.reshape(b, s, d)
```

```python
import functools
import math

import numpy as np
import jax
import jax.numpy as jnp
from jax import lax
from jax.experimental import pallas as pl
from jax.experimental.pallas import tpu as pltpu

F32 = jnp.float32
BF16 = jnp.bfloat16

D_MODEL = 2048
D_BRANCH = 512
D_FF = 8192
N_BRANCH = 4
CHUNK = 64
EPS = 1e-6
ROPE_THETA = 10000.0
HEAD_DIM = 64
DIFF_HEADS = 4
RWKV_GN_EPS = 64e-5
DSA_HEADS = 8
IDX_HEADS = 8
TOPK_MAX = 256

LANE = 128
VMEM_LIMIT = 56 * 1024 * 1024

U_B, U_DKV, U_DIKW = 0, 14, 15
U_AQ, U_AK, U_AV = 16, 20, 24
U_CQ, U_CF, U_CI, U_CG = 28, 32, 36, 40
U_DQ, U_DIQ = 44, 48
N_MAIN = 52 * LANE
B_WIDTH = 1792

NEG = -1e30
INT_MIN = -2147483648
LOG2E = 1.4426950408889634

NT_DIMS = (((1,), (1,)), ((), ()))
TN_DIMS = (((0,), (0,)), ((), ()))


def _cp(sem, vmem=VMEM_LIMIT):
    return pltpu.CompilerParams(dimension_semantics=sem, vmem_limit_bytes=vmem)


def _dot(a, b, dims=None):
    if dims is None:
        return jnp.dot(a, b, preferred_element_type=F32)
    return lax.dot_general(a, b, dims, preferred_element_type=F32)


def _dot_hi(a, b, dims=(((1,), (0,)), ((), ()))):
    return lax.dot_general(a, b, dims, preferred_element_type=F32,
                           precision=lax.Precision.HIGHEST)


def _rmsnorm_kernel(x_ref, g_ref, o_ref):
    x = x_ref[...]
    ms = jnp.mean(x * x, axis=-1, keepdims=True)
    o_ref[...] = (x * lax.rsqrt(ms + EPS) * g_ref[...]).astype(o_ref.dtype)


def rmsnorm_bf16(x, g, tm=512):
    m, d = x.shape
    return pl.pallas_call(
        _rmsnorm_kernel,
        grid=(m // tm,),
        in_specs=[pl.BlockSpec((tm, d), lambda i: (i, 0)),
                  pl.BlockSpec((1, d), lambda i: (0, 0))],
        out_specs=pl.BlockSpec((tm, d), lambda i: (i, 0)),
        out_shape=jax.ShapeDtypeStruct((m, d), BF16),
        compiler_params=_cp(("parallel",)),
        name="rmsnorm",
    )(x, g.reshape(1, d))


def _mm_kernel(x_ref, w_ref, o_ref, *, act):
    acc = _dot(x_ref[...], w_ref[...])
    if act == "sigmoid":
        acc = jax.nn.sigmoid(acc)
    elif act == "relu2":
        acc = jnp.square(jnp.maximum(acc, 0.0))
    o_ref[...] = acc.astype(o_ref.dtype)


def matmul(x, w, act, out_dtype, tm=1024, tn=512, name="mm"):
    m, k = x.shape
    n = w.shape[1]
    return pl.pallas_call(
        functools.partial(_mm_kernel, act=act),
        grid=(n // tn, m // tm),
        in_specs=[pl.BlockSpec((tm, k), lambda j, i: (i, 0)),
                  pl.BlockSpec((k, tn), lambda j, i: (0, j))],
        out_specs=pl.BlockSpec((tm, tn), lambda j, i: (i, j)),
        out_shape=jax.ShapeDtypeStruct((m, n), out_dtype),
        compiler_params=_cp(("parallel", "parallel")),
        name=name,
    )(x, w)


def _mm_norm_res_kernel(a_ref, w_ref, x_ref, g_ref, o_ref, acc_ref):
    kk = pl.program_id(1)

    @pl.when(kk == 0)
    def _():
        acc_ref[...] = jnp.zeros_like(acc_ref)

    acc_ref[...] += _dot(a_ref[...], w_ref[...])

    @pl.when(kk == pl.num_programs(1) - 1)
    def _():
        y = acc_ref[...]
        ms = jnp.mean(y * y, axis=-1, keepdims=True)
        o_ref[...] = x_ref[...] + y * lax.rsqrt(ms + EPS) * g_ref[...]


def matmul_norm_residual(a, w, x, g, tm=512, tk=1024, name="mm_norm_res"):
    m, k = a.shape
    n = w.shape[1]
    return pl.pallas_call(
        _mm_norm_res_kernel,
        grid=(m // tm, k // tk),
        in_specs=[pl.BlockSpec((tm, tk), lambda i, kk: (i, kk)),
                  pl.BlockSpec((tk, n), lambda i, kk: (kk, 0)),
                  pl.BlockSpec((tm, n), lambda i, kk: (i, 0)),
                  pl.BlockSpec((1, n), lambda i, kk: (0, 0))],
        out_specs=pl.BlockSpec((tm, n), lambda i, kk: (i, 0)),
        out_shape=jax.ShapeDtypeStruct((m, n), F32),
        scratch_shapes=[pltpu.VMEM((tm, n), F32)],
        compiler_params=_cp(("parallel", "arbitrary")),
        name=name,
    )(a, w, x, g.reshape(1, n))


def _merge_kernel(ya, yb, yc, yd, wb, ga, gb, gc, gd, o_ref):
    acc = None
    for n, (y, g) in enumerate(((ya, ga), (yb, gb), (yc, gc), (yd, gd))):
        t = g[...].astype(F32) * _dot(y[...], wb[n])
        acc = t if acc is None else acc + t
    o_ref[...] = acc.astype(o_ref.dtype)


def gated_merge(ys, wb, gate, tm=512, tn=1024):
    m = ys[0].shape[0]
    nj = D_MODEL // tn
    y_spec = pl.BlockSpec((tm, D_BRANCH), lambda j, i: (i, 0))
    g_specs = [pl.BlockSpec((tm, tn), functools.partial(lambda j, i, n: (i, n * nj + j), n=n))
               for n in range(N_BRANCH)]
    return pl.pallas_call(
        _merge_kernel,
        grid=(nj, m // tm),
        in_specs=[y_spec] * 4 + [pl.BlockSpec((N_BRANCH, D_BRANCH, tn), lambda j, i: (0, 0, j))] + g_specs,
        out_specs=pl.BlockSpec((tm, tn), lambda j, i: (i, j)),
        out_shape=jax.ShapeDtypeStruct((m, D_MODEL), BF16),
        compiler_params=_cp(("parallel", "parallel")),
        name="gated_merge",
    )(*ys, wb, gate, gate, gate, gate)


def _rope_table_kernel(pos_ref, inv_ref, sgn_ref, cos_ref, sin_ref):
    ang = pos_ref[...].astype(F32) * inv_ref[...]
    cos_ref[...] = jnp.cos(ang)
    sin_ref[...] = jnp.sin(ang) * sgn_ref[...]


def rope_tables(positions, tm=512):
    m = positions.size
    half = HEAD_DIM // 2
    inv = ROPE_THETA ** (-np.arange(0, HEAD_DIM, 2, dtype=np.float32) / HEAD_DIM)
    inv = np.tile(inv.astype(np.float32), 4).reshape(1, LANE)
    sgn = np.tile(np.concatenate([-np.ones(half, np.float32), np.ones(half, np.float32)]), 2).reshape(1, LANE)
    spec = pl.BlockSpec((tm, LANE), lambda i: (i, 0))
    cst = pl.BlockSpec((1, LANE), lambda i: (0, 0))
    return pl.pallas_call(
        _rope_table_kernel,
        grid=(m // tm,),
        in_specs=[pl.BlockSpec((tm, 1), lambda i: (i, 0)), cst, cst],
        out_specs=[spec, spec],
        out_shape=[jax.ShapeDtypeStruct((m, LANE), F32)] * 2,
        compiler_params=_cp(("parallel",)),
        name="rope_tables",
    )(positions.reshape(m, 1), jnp.asarray(inv), jnp.asarray(sgn))


def _rope(x, cos, sin):
    w = x.shape[1]
    n = w // LANE
    if n > 1:
        cos = jnp.concatenate([cos] * n, axis=1)
        sin = jnp.concatenate([sin] * n, axis=1)
    lane = lax.broadcasted_iota(jnp.int32, x.shape, 1)
    up = pltpu.roll(x, w - HEAD_DIM // 2, axis=1)
    dn = pltpu.roll(x, HEAD_DIM // 2, axis=1)
    rot = jnp.where((lane & (HEAD_DIM // 2)) == 0, up, dn)
    return x * cos + rot * sin


def _rope_qkv_kernel(q_ref, k_ref, v_ref, cos_ref, sin_ref, qo_ref, ko_ref, vo_ref, *, scale):
    cos = cos_ref[0]
    sin = sin_ref[0]
    hd = HEAD_DIM
    q = (_rope(q_ref[0], cos, sin) * scale).astype(qo_ref.dtype)
    k = _rope(k_ref[0], cos, sin).astype(ko_ref.dtype)
    for h in range(q.shape[1] // hd):
        qo_ref[0, h] = q[:, h * hd:(h + 1) * hd]
        ko_ref[0, h] = k[:, h * hd:(h + 1) * hd]
    vo_ref[0] = v_ref[0].astype(vo_ref.dtype)


def rope_qkv(pm, cos, sin, uq, uk, uv, scale, ts=512):
    b, s, _ = pm.shape
    wq = D_BRANCH
    nh = wq // HEAD_DIM
    tab = pl.BlockSpec((1, ts, LANE), lambda bi, i: (bi, i, 0))
    col = lambda u: pl.BlockSpec((1, ts, wq), lambda bi, i: (bi, i, u * LANE // wq))
    heads = pl.BlockSpec((1, nh, ts, HEAD_DIM), lambda bi, i: (bi, 0, i, 0))
    return pl.pallas_call(
        functools.partial(_rope_qkv_kernel, scale=scale),
        grid=(b, s // ts),
        in_specs=[col(uq), col(uk), col(uv), tab, tab],
        out_specs=[heads, heads, pl.BlockSpec((1, ts, wq), lambda bi, i: (bi, i, 0))],
        out_shape=[jax.ShapeDtypeStruct((b, nh, s, HEAD_DIM), BF16)] * 2
                  + [jax.ShapeDtypeStruct((b, s, wq), BF16)],
        compiler_params=_cp(("parallel", "parallel")),
        name="rope_qkv",
    )(pm, pm, pm, cos, sin)


def _lane_fold(x, op):
    out = x[..., 0:LANE]
    for c in range(1, x.shape[-1] // LANE):
        out = op(out, x[..., c * LANE:(c + 1) * LANE])
    return out


def _diff_attn_kernel(q_ref, k_ref, v_ref, lam_ref, g_ref, o_ref, m_ref, l_ref, acc_ref, *, tq, tk, lam_init):
    i = pl.program_id(2)
    jd = (i * tq) // tk
    row = i * tq + lax.broadcasted_iota(jnp.int32, (tq, tk), 0)
    col = jd * tk + lax.broadcasted_iota(jnp.int32, (tq, tk), 1)
    vis = (col >> 6) <= (row >> 6)

    def scores(j, mm, masked):
        kb = k_ref[0, mm, pl.ds(pl.multiple_of(j * tk, tk), tk), :]
        s = _dot(q_ref[0, mm], kb, NT_DIMS)
        return jnp.where(vis, s, NEG) if masked else s

    m_ref[...] = jnp.full_like(m_ref, NEG)

    def max_step(j, masked):
        for mm in range(2):
            m_ref[mm] = jnp.maximum(m_ref[mm], _lane_fold(scores(j, mm, masked), jnp.maximum))

    lax.fori_loop(0, jd, lambda j, c: (max_step(j, False), c)[1], 0)
    max_step(jd, True)
    for mm in range(2):
        m_ref[mm] = jnp.broadcast_to(jnp.max(m_ref[mm], axis=1, keepdims=True), (tq, LANE))
    l_ref[...] = jnp.zeros_like(l_ref)
    acc_ref[...] = jnp.zeros_like(acc_ref)

    def acc_step(j, masked):
        vb = v_ref[0, pl.ds(pl.multiple_of(j * tk, tk), tk), :]
        for mm in range(2):
            s = scores(j, mm, masked)
            mb = m_ref[mm]
            ps = [jnp.exp2(s[:, c * LANE:(c + 1) * LANE] - mb) for c in range(tk // LANE)]
            l_ref[mm] += functools.reduce(jnp.add, ps)
            acc_ref[mm] += _dot(jnp.concatenate(ps, axis=1).astype(BF16), vb)

    lax.fori_loop(0, jd, lambda j, c: (acc_step(j, False), c)[1], 0)
    acc_step(jd, True)

    lv = lam_ref[...]
    lam = (jnp.exp(jnp.sum(lv[0:1] * lv[1:2], keepdims=True))
           - jnp.exp(jnp.sum(lv[2:3] * lv[3:4], keepdims=True)) + lam_init)
    l0 = jnp.sum(l_ref[0], axis=1, keepdims=True)
    l1 = jnp.sum(l_ref[1], axis=1, keepdims=True)
    o = acc_ref[0] / l0 - lam * (acc_ref[1] / l1)
    ms = jnp.mean(o * o, axis=-1, keepdims=True)
    o = o * lax.rsqrt(ms + EPS) * g_ref[...] * (1.0 - lam_init)
    o_ref[0] = o.astype(o_ref.dtype)


def diff_attention(pm, cos, sin, lam_vecs, subln_g, layer, tq=256, tk=512):
    b, s, _ = pm.shape
    qa, ka, va = rope_qkv(pm, cos, sin, U_AQ, U_AK, U_AV, HEAD_DIM ** -0.5 * LOG2E)
    lam_init = 0.8 - 0.6 * math.exp(-0.3 * layer)
    return pl.pallas_call(
        functools.partial(_diff_attn_kernel, tq=tq, tk=tk, lam_init=lam_init),
        grid=(b, DIFF_HEADS, s // tq),
        in_specs=[pl.BlockSpec((1, 2, tq, HEAD_DIM), lambda bi, h, i: (bi, h, i, 0)),
                  pl.BlockSpec((1, 2, s, HEAD_DIM), lambda bi, h, i: (bi, h, 0, 0)),
                  pl.BlockSpec((1, s, LANE), lambda bi, h, i: (bi, 0, h)),
                  pl.BlockSpec((4, HEAD_DIM), lambda bi, h, i: (0, 0)),
                  pl.BlockSpec((1, LANE), lambda bi, h, i: (0, 0))],
        out_specs=pl.BlockSpec((1, tq, LANE), lambda bi, h, i: (bi, i, h)),
        out_shape=jax.ShapeDtypeStruct((b, s, D_BRANCH), BF16),
        scratch_shapes=[pltpu.VMEM((2, tq, LANE), F32), pltpu.VMEM((2, tq, LANE), F32),
                        pltpu.VMEM((2, tq, LANE), F32)],
        compiler_params=_cp(("parallel", "parallel", "parallel")),
        name="diff_attention",
    )(qa, ka, va, lam_vecs, subln_g.reshape(1, LANE))


def _softplus(z):
    return jnp.maximum(z, 0.0) + jnp.log(1.0 + jnp.exp(-jnp.abs(z)))


def _rwkv_prep_kernel(x_ref, prev_ref, mu_ref, wup_ref, aup_ref, gup_ref, w0_ref, a0_ref, kkg_ref, kag_ref,
                      seg_ref, r_ref, k_ref, v_ref, al_ref, be_ref, lw_ref, g_ref):
    i = pl.program_id(1)
    p = x_ref[0]
    t = p.shape[0]
    row = lax.broadcasted_iota(jnp.int32, p.shape, 0)
    last = prev_ref[0][7:8, :]
    last = jnp.where(i == 0, jnp.zeros_like(last), last)
    prev = jnp.where(row == 0, jnp.broadcast_to(last, p.shape), pltpu.roll(p, 1, axis=0))
    ps = p + (prev - p) * mu_ref[...]
    d = D_BRANCH
    r, k, v = ps[:, 0:d], ps[:, d:2 * d], ps[:, 2 * d:3 * d]
    wd = ps[:, 3 * d:3 * d + 64]
    ad = ps[:, 3 * d + 64:3 * d + 128]
    gd = ps[:, 3 * d + 128:3 * d + 256]
    w_log = -_softplus(-(w0_ref[...] + _dot_hi(jnp.tanh(wd), wup_ref[...]))) - 0.5
    a = jax.nn.sigmoid(a0_ref[...] + _dot_hi(ad, aup_ref[...]))
    g = _dot_hi(jax.nn.sigmoid(gd), gup_ref[...])
    kk = k * kkg_ref[...]
    nrm = jnp.sqrt(_dot_hi(kk * kk, seg_ref[...]))
    kk = kk / jnp.maximum(nrm, 1e-12)
    r_ref[0] = r
    k_ref[0] = k * (1.0 + (a - 1.0) * kag_ref[...])
    v_ref[0] = v
    al_ref[0] = -kk
    be_ref[0] = kk * a
    lw_ref[0] = -jnp.exp(w_log)
    g_ref[0] = g


def _rwkv_scan_kernel(r_ref, k_ref, v_ref, al_ref, be_ref, lw_ref, g_ref, rk_ref, gng_ref, gnb_ref,
                      o_ref, st_ref, *, rows):
    t = pl.program_id(2)
    hd = HEAD_DIM
    nch = rows // CHUNK

    @pl.when(t == 0)
    def _():
        st_ref[...] = jnp.zeros_like(st_ref)

    lw = lw_ref[0]
    rowc = lax.broadcasted_iota(jnp.int32, lw.shape, 0) & (CHUNK - 1)
    cum = lw
    for dd in (1, 2, 4, 8, 16, 32):
        cum = cum + jnp.where(rowc >= dd, pltpu.roll(cum, dd, axis=0), 0.0)
    cl = jnp.concatenate(
        [jnp.broadcast_to(cum[(c + 1) * CHUNK - 1:(c + 1) * CHUNK, :], (CHUNK, LANE)) for c in range(nch)], axis=0)
    e_in = jnp.exp(cum)
    e_out = jnp.exp(-cum)
    e_end = jnp.exp(cl - cum)
    r2, k2, v2, al2, be2 = r_ref[0], k_ref[0], v_ref[0], al_ref[0], be_ref[0]
    at2 = al2 * jnp.exp(cum - lw)
    rt2 = r2 * e_in
    bt2 = be2 * e_out
    kt2 = k2 * e_out
    bh2 = be2 * e_end
    kh2 = k2 * e_end
    wl2 = jnp.exp(cl)

    row = lax.broadcasted_iota(jnp.int32, (rows, rows), 0)
    col = lax.broadcasted_iota(jnp.int32, (rows, rows), 1)
    same = (row >> 6) == (col >> 6)
    m_strict = same & (row > col)
    m_incl = same & (row >= col)
    eye = (row == col).astype(F32)
    r64 = lax.broadcasted_iota(jnp.int32, (hd, hd), 0)
    c64 = lax.broadcasted_iota(jnp.int32, (hd, hd), 1)
    g2 = g_ref[0]
    outs = []
    for hh in range(2):
        sl = slice(hh * hd, (hh + 1) * hd)
        at, rt, bt, kt = at2[:, sl], rt2[:, sl], bt2[:, sl], kt2[:, sl]
        bh, kh, v, r, kp = bh2[:, sl], kh2[:, sl], v2[:, sl], r2[:, sl], k2[:, sl]
        lhs = jnp.concatenate([at, rt], axis=0).astype(BF16)
        rhs = jnp.concatenate([bt, kt], axis=0).astype(BF16)
        gm = _dot(lhs, rhs, NT_DIMS)
        n_ab = jnp.where(m_strict, gm[:rows, :rows], 0.0)
        a_ak = jnp.where(m_strict, gm[:rows, rows:], 0.0).astype(BF16)
        a_rb = jnp.where(m_incl, gm[rows:, :rows], 0.0).astype(BF16)
        a_rk = jnp.where(m_incl, gm[rows:, rows:], 0.0).astype(BF16)
        n8 = jnp.where((row >> 3) == (col >> 3), n_ab, 0.0)
        n8b = n8.astype(BF16)
        n_2 = _dot(n8b, n8b)
        n2b = n_2.astype(BF16)
        n_3 = _dot(n2b, n8b)
        n_4 = _dot(n2b, n2b)
        tinv = eye + n8 + n_2 + n_3
        tinv = tinv + _dot(tinv.astype(BF16), n_4.astype(BF16))
        for sh in (3, 4, 5):
            nl = jnp.where(((row >> (sh + 1)) == (col >> (sh + 1))) & ((row >> sh) != (col >> sh)),
                           n_ab, 0.0).astype(BF16)
            tb = tinv.astype(BF16)
            tinv = tinv + _dot(_dot(tb, nl).astype(BF16), tb)
        tb = tinv.astype(BF16)
        vb = v.astype(BF16)
        akv = _dot(a_ak, vb)
        pmat = _dot(tb, jnp.concatenate([at, akv], axis=1).astype(BF16))
        qmat = _dot(a_rb, pmat.astype(BF16)) + jnp.concatenate([rt, _dot(a_rk, vb)], axis=1)
        st = st_ref[hh]
        ys = []
        for c in range(nch):
            cs = slice(c * CHUNK, (c + 1) * CHUNK)
            y = _dot_hi(qmat[cs, :hd], st, NT_DIMS) + qmat[cs, hd:]
            ys.append(y)
            mn = _dot(pmat[cs].astype(BF16), bh[cs].astype(BF16), TN_DIMS)
            wl = wl2[c * CHUNK:c * CHUNK + 1, sl]
            m_c = jnp.where(r64 == c64, jnp.broadcast_to(wl, (hd, hd)), 0.0) + mn[:hd]
            n_c = mn[hd:] + _dot(vb[cs], kh[cs].astype(BF16), TN_DIMS)
            st = _dot_hi(st, m_c) + n_c
        st_ref[hh] = st
        y = jnp.concatenate(ys, axis=0)
        mu = jnp.mean(y, axis=-1, keepdims=True)
        var = jnp.mean(jnp.square(y - mu), axis=-1, keepdims=True)
        yn = (y - mu) * lax.rsqrt(var + RWKV_GN_EPS) * gng_ref[:, sl] + gnb_ref[:, sl]
        bonus = jnp.sum(r * kp * rk_ref[0, hh:hh + 1, :], axis=-1, keepdims=True) * v
        outs.append((yn + bonus) * g2[:, sl])
    o_ref[0] = jnp.concatenate(outs, axis=1).astype(o_ref.dtype)


def rwkv7(pm, mu, w_up, a_up, g_up, w0, a0, kk_gain, ka_gain, r_k, gn_g, gn_b, ts=256, rows=256):
    b, s, _ = pm.shape
    d = D_BRANCH
    seg = np.kron(np.eye(d // HEAD_DIM, dtype=np.float32), np.ones((HEAD_DIM, HEAD_DIM), np.float32))
    row1 = lambda a: a.reshape(1, -1)
    cst = lambda shape: pl.BlockSpec(shape, lambda bi, i: (0,) * len(shape))
    blk = pl.BlockSpec((1, ts, d), lambda bi, i: (bi, i, 0))
    r, k, v, al, be, lw, g = pl.pallas_call(
        _rwkv_prep_kernel,
        grid=(b, s // ts),
        in_specs=[pl.BlockSpec((1, ts, B_WIDTH), lambda bi, i: (bi, i, 0)),
                  pl.BlockSpec((1, 8, B_WIDTH), lambda bi, i: (bi, jnp.maximum(i * (ts // 8) - 1, 0), 0)),
                  cst((1, B_WIDTH)), cst((64, d)), cst((64, d)), cst((128, d)),
                  cst((1, d)), cst((1, d)), cst((1, d)), cst((1, d)), cst((d, d))],
        out_specs=[blk] * 7,
        out_shape=[jax.ShapeDtypeStruct((b, s, d), F32)] * 7,
        compiler_params=_cp(("parallel", "arbitrary")),
        name="rwkv_prep",
    )(pm, pm, row1(mu), w_up, a_up, g_up, row1(w0), row1(a0), row1(kk_gain), row1(ka_gain), jnp.asarray(seg))

    hp = pl.BlockSpec((1, rows, LANE), lambda bi, h, t: (bi, t, h))
    par = pl.BlockSpec((1, LANE), lambda bi, h, t: (0, h))
    return pl.pallas_call(
        functools.partial(_rwkv_scan_kernel, rows=rows),
        grid=(b, d // LANE, s // rows),
        in_specs=[hp] * 7 + [pl.BlockSpec((1, 2, HEAD_DIM), lambda bi, h, t: (h, 0, 0)), par, par],
        out_specs=hp,
        out_shape=jax.ShapeDtypeStruct((b, s, d), BF16),
        scratch_shapes=[pltpu.VMEM((2, HEAD_DIM, HEAD_DIM), F32)],
        compiler_params=_cp(("parallel", "parallel", "arbitrary")),
        name="rwkv_scan",
    )(r, k, v, al, be, lw, g, r_k.reshape(-1, 2, HEAD_DIM), row1(gn_g), row1(gn_b))


def _hgrn_kernel(q_ref, f_ref, i_ref, g_ref, lb_ref, ng_ref, o_ref, st_ref, *, rows):
    t = pl.program_id(2)
    sub = 16

    @pl.when(t == 0)
    def _():
        st_ref[...] = jnp.zeros_like(st_ref)

    q = jax.nn.silu(q_ref[0])
    lb = lb_ref[...]
    fg = lb + (1.0 - lb) * jax.nn.sigmoid(f_ref[0])
    lf = jnp.log(fg)
    kf = 1.0 - fg
    iv = i_ref[0]
    rowc = lax.broadcasted_iota(jnp.int32, lf.shape, 0) & (CHUNK - 1)
    bcum = lf
    for dd in (1, 2, 4, 8, 16, 32):
        bcum = bcum + jnp.where(rowc >= dd, pltpu.roll(bcum, dd, axis=0), 0.0)
    ivb = iv.astype(BF16)
    rsub = lax.broadcasted_iota(jnp.int32, (sub, LANE), 0)
    st = st_ref[...]
    outs = []
    for c in range(rows // CHUNK):
        c0 = c * CHUNK
        bc = bcum[c0:c0 + CHUNK]
        qc = q[c0:c0 + CHUNK]
        kc = kf[c0:c0 + CHUNK]
        ic = iv[c0:c0 + CHUNK]
        icb = ivb[c0:c0 + CHUNK]
        o_inter = _dot((qc * jnp.exp(bc)).astype(BF16), st.astype(BF16), NT_DIMS)
        blocks = []
        for ib in range(CHUNK // sub):
            r0 = ib * sub
            bi = bc[r0:r0 + sub]
            qi = qc[r0:r0 + sub]
            ki = kc[r0:r0 + sub]
            ii = ic[r0:r0 + sub]
            acc = jnp.zeros((sub, LANE), F32)
            for s_ in range(sub):
                e = jnp.where(rsub >= s_, jnp.exp(jnp.minimum(bi - bi[s_:s_ + 1], 0.0)), 0.0)
                a = jnp.sum(qi * (ki[s_:s_ + 1] * e), axis=-1, keepdims=True)
                acc = acc + a * ii[s_:s_ + 1]
            if ib > 0:
                ref_row = bc[r0 - 1:r0]
                qt = (qi * jnp.exp(bi - ref_row)).astype(BF16)
                kt = (kc[0:r0] * jnp.exp(ref_row - bc[0:r0])).astype(BF16)
                a_off = _dot(qt, kt, NT_DIMS).astype(BF16)
                acc = acc + _dot(a_off, icb[0:r0])
            blocks.append(acc)
        outs.append(o_inter + jnp.concatenate(blocks, axis=0))
        bl = bc[CHUNK - 1:CHUNK]
        kdec = (kc * jnp.exp(bl - bc)).astype(BF16)
        st = st * jnp.exp(bl) + _dot(icb, kdec, TN_DIMS)
    st_ref[...] = st
    o = jnp.concatenate(outs, axis=0)
    ms = jnp.mean(o * o, axis=-1, keepdims=True)
    o = o * lax.rsqrt(ms + EPS) * ng_ref[...]
    o_ref[0] = (o * jax.nn.silu(g_ref[0])).astype(o_ref.dtype)


def hgrn2(pm, lb, norm_g, rows=256):
    b, s, _ = pm.shape
    heads = D_BRANCH // LANE
    spec = lambda u: pl.BlockSpec((1, rows, LANE), lambda bi, h, t: (bi, t, u + h))
    par = pl.BlockSpec((1, LANE), lambda bi, h, t: (0, h))
    return pl.pallas_call(
        functools.partial(_hgrn_kernel, rows=rows),
        grid=(b, heads, s // rows),
        in_specs=[spec(U_CQ), spec(U_CF), spec(U_CI), spec(U_CG), par, par],
        out_specs=spec(0),
        out_shape=jax.ShapeDtypeStruct((b, s, D_BRANCH), BF16),
        scratch_shapes=[pltpu.VMEM((LANE, LANE), F32)],
        compiler_params=_cp(("parallel", "parallel", "arbitrary")),
        name="hgrn2",
    )(pm, pm, pm, pm, lb.reshape(1, -1), norm_g.reshape(1, -1))


def _dsa_prep_kernel(q_ref, iq_ref, kv_ref, ikw_ref, cos_ref, sin_ref,
                     qo_ref, iqo_ref, ko_ref, vo_ref, iko_ref, wo_ref, *, scale, wscale):
    cos = cos_ref[0]
    sin = sin_ref[0]
    hd = HEAD_DIM
    q = (_rope(q_ref[0], cos, sin) * scale).astype(BF16)
    iq = _rope(iq_ref[0], cos, sin).astype(BF16)
    for h in range(DSA_HEADS):
        qo_ref[0, h] = q[:, h * hd:(h + 1) * hd]
        iqo_ref[0, h] = iq[:, h * hd:(h + 1) * hd]
    kv = kv_ref[0]
    ko_ref[0] = _rope(kv, cos, sin)[:, :hd].astype(BF16)
    vo_ref[0] = kv[:, hd:].astype(BF16)
    ikw = ikw_ref[0]
    iko_ref[0] = _rope(ikw, cos, sin)[:, :hd].astype(BF16)
    wo_ref[0] = ikw[:, hd:hd + IDX_HEADS] * wscale


def _to_key(x):
    x = jnp.where(x == 0.0, 0.0, x)
    bits = pltpu.bitcast(x, jnp.int32)
    return jnp.where(bits < 0, bits ^ jnp.int32(0x7FFFFFFF), bits)


def _dsa_index_kernel(iq_ref, ik_ref, wt_ref, mask_ref, key_ref, *, tq, tk, top_k):
    i = pl.program_id(1)
    nk = key_ref.shape[0]
    nvis = (i * tq + tq - 1) // tk + 1
    krow = lax.broadcasted_iota(jnp.int32, (tk, tq), 0)
    qcol = i * tq + lax.broadcasted_iota(jnp.int32, (tk, tq), 1)
    wt = wt_ref[0]

    def score_block(jb, carry):
        ikb = ik_ref[0, pl.ds(pl.multiple_of(jb * tk, tk), tk), :]
        acc = jnp.zeros((tk, tq), F32)
        for h in range(IDX_HEADS):
            s = _dot(ikb, iq_ref[0, h], NT_DIMS)
            acc = acc + jnp.maximum(s, 0.0) * wt[h:h + 1, :]
        vis = ((jb * tk + krow) >> 6) <= (qcol >> 6)
        key_ref[jb] = jnp.where(vis, _to_key(acc), INT_MIN)
        return carry

    lax.fori_loop(0, nvis, score_block, 0)

    def count(pred_fn):
        def blk(jb, cnt):
            ind = jnp.where(pred_fn(key_ref[jb], jb), 1, 0)
            return cnt + jnp.sum(ind.reshape(tk // 8, 8, tq), axis=0)
        cnt = lax.fori_loop(0, nvis, blk, jnp.zeros((8, tq), jnp.int32))
        return jnp.sum(cnt, axis=0, keepdims=True)

    def bit_step(bi, prefix):
        cand = prefix | lax.shift_left(jnp.int32(1), 31 - bi)
        cand_s = cand ^ jnp.int32(INT_MIN)
        c = count(lambda kb, jb: kb >= cand_s)
        return jnp.where(c >= top_k, cand, prefix)

    prefix = lax.fori_loop(0, 32, bit_step, jnp.zeros((1, tq), jnp.int32))
    tau = prefix ^ jnp.int32(INT_MIN)
    n_gt = count(lambda kb, jb: kb > tau)
    n_eq = count(lambda kb, jb: kb == tau)
    need = top_k - n_gt
    tied = (n_eq > need) & (tau != INT_MIN)
    n_cols = nk * tk

    idx_bits = int(n_cols).bit_length()

    def tie_break():
        def idx_step(bi, pre):
            cand = pre | lax.shift_left(jnp.int32(1), idx_bits - 1 - bi)
            c = count(lambda kb, jb: (kb == tau) & ((jb * tk + krow) < cand))
            return jnp.where(c < need, cand, pre)
        cut = lax.fori_loop(0, idx_bits, idx_step, jnp.zeros((1, tq), jnp.int32))
        return jnp.where(tied, cut, n_cols)

    cut = lax.cond(jnp.max(tied.astype(jnp.int32)) > 0, tie_break,
                   lambda: jnp.full((1, tq), n_cols, jnp.int32))

    def write_block(jb, carry):
        kb = key_ref[jb]
        sel = (kb > tau) | ((kb == tau) & ((jb * tk + krow) <= cut))
        sel = sel & (kb != INT_MIN)
        mask_ref[0, jb] = jnp.where(sel, 1.0, 0.0).T.astype(mask_ref.dtype)
        return carry

    def zero_block(jb, carry):
        mask_ref[0, jb] = jnp.zeros((tq, tk), mask_ref.dtype)
        return carry

    lax.fori_loop(0, nvis, write_block, 0)
    lax.fori_loop(nvis, nk, zero_block, 0)


def _dsa_attn_kernel(q_ref, k_ref, v_ref, mask_ref, o_ref, m_ref, l_ref, acc_ref, *, tq, tk):
    i = pl.program_id(1)
    nh = DSA_HEADS
    hd = HEAD_DIM
    nvis = (i * tq + tq - 1) // tk + 1
    nchunk = tk // LANE

    def scores(j):
        kb = k_ref[0, pl.ds(pl.multiple_of(j * tk, tk), tk), :]
        s = _dot(q_ref[0].reshape(nh * tq, hd), kb, NT_DIMS).reshape(nh, tq, tk)
        return s, (mask_ref[0, j] > 0)[None]

    m_ref[...] = jnp.full_like(m_ref, NEG)

    def max_step(j, carry):
        s, sel = scores(j)
        m_ref[...] = jnp.maximum(m_ref[...], _lane_fold(jnp.where(sel, s, NEG), jnp.maximum))
        return carry

    lax.fori_loop(0, nvis, max_step, 0)
    m_ref[...] = jnp.broadcast_to(jnp.max(m_ref[...], axis=2, keepdims=True), m_ref.shape)
    l_ref[...] = jnp.zeros_like(l_ref)
    acc_ref[...] = jnp.zeros_like(acc_ref)

    def acc_step(j, carry):
        s, sel = scores(j)
        mb = m_ref[...]
        ps = [jnp.where(sel[:, :, c * LANE:(c + 1) * LANE], jnp.exp2(s[:, :, c * LANE:(c + 1) * LANE] - mb), 0.0)
              for c in range(nchunk)]
        l_ref[...] += functools.reduce(jnp.add, ps)
        p = jnp.concatenate(ps, axis=2).astype(BF16).reshape(nh * tq, tk)
        vb = v_ref[0, pl.ds(pl.multiple_of(j * tk, tk), tk), :]
        acc_ref[...] += _dot(p, vb).reshape(nh, tq, hd)
        return carry

    lax.fori_loop(0, nvis, acc_step, 0)
    o = acc_ref[...] / jnp.sum(l_ref[...], axis=2, keepdims=True)
    o_ref[0] = jnp.concatenate([o[h] for h in range(nh)], axis=1).astype(o_ref.dtype)


def dsa_attention(pm, cos, sin, ts=512, tq=128, tk=512):
    b, s, _ = pm.shape
    top_k = min(TOPK_MAX, s // 4)
    hd = HEAD_DIM
    nk = s // tk
    wide = lambda u: pl.BlockSpec((1, ts, D_BRANCH), lambda bi, i: (bi, i, u * LANE // D_BRANCH))
    one = lambda u: pl.BlockSpec((1, ts, LANE), lambda bi, i: (bi, i, u))
    heads_out = pl.BlockSpec((1, DSA_HEADS, ts, hd), lambda bi, i: (bi, 0, i, 0))
    narrow = lambda w: pl.BlockSpec((1, ts, w), lambda bi, i: (bi, i, 0))
    q, iq, k, v, ik, w = pl.pallas_call(
        functools.partial(_dsa_prep_kernel, scale=hd ** -0.5 * LOG2E, wscale=(IDX_HEADS ** -0.5) * (hd ** -0.5)),
        grid=(b, s // ts),
        in_specs=[wide(U_DQ), wide(U_DIQ), one(U_DKV), one(U_DIKW), one(0), one(0)],
        out_specs=[heads_out, heads_out, narrow(hd), narrow(hd), narrow(hd), narrow(IDX_HEADS)],
        out_shape=[jax.ShapeDtypeStruct((b, DSA_HEADS, s, hd), BF16)] * 2
                  + [jax.ShapeDtypeStruct((b, s, hd), BF16)] * 3
                  + [jax.ShapeDtypeStruct((b, s, IDX_HEADS), F32)],
        compiler_params=_cp(("parallel", "parallel")),
        name="dsa_prep",
    )(pm, pm, pm, pm, cos, sin)

    mask_spec = pl.BlockSpec((1, nk, tq, tk), lambda bi, i: (bi, 0, i, 0))
    whole = pl.BlockSpec((1, s, hd), lambda bi, i: (bi, 0, 0))
    mask = pl.pallas_call(
        functools.partial(_dsa_index_kernel, tq=tq, tk=tk, top_k=top_k),
        grid=(b, s // tq),
        in_specs=[pl.BlockSpec((1, IDX_HEADS, tq, hd), lambda bi, i: (bi, 0, i, 0)),
                  whole,
                  pl.BlockSpec((1, IDX_HEADS, tq), lambda bi, i: (bi, 0, i))],
        out_specs=mask_spec,
        out_shape=jax.ShapeDtypeStruct((b, nk, s, tk), BF16),
        scratch_shapes=[pltpu.VMEM((nk, tk, tq), jnp.int32)],
        compiler_params=_cp(("parallel", "parallel")),
        name="dsa_index",
    )(iq, ik, jnp.swapaxes(w, 1, 2))

    return pl.pallas_call(
        functools.partial(_dsa_attn_kernel, tq=tq, tk=tk),
        grid=(b, s // tq),
        in_specs=[pl.BlockSpec((1, DSA_HEADS, tq, hd), lambda bi, i: (bi, 0, i, 0)),
                  whole, whole, mask_spec],
        out_specs=pl.BlockSpec((1, tq, D_BRANCH), lambda bi, i: (bi, i, 0)),
        out_shape=jax.ShapeDtypeStruct((b, s, D_BRANCH), BF16),
        scratch_shapes=[pltpu.VMEM((DSA_HEADS, tq, LANE), F32), pltpu.VMEM((DSA_HEADS, tq, LANE), F32),
                        pltpu.VMEM((DSA_HEADS, tq, hd), F32)],
        compiler_params=_cp(("parallel", "parallel")),
        name="dsa_attention",
    )(q, k, v, mask)


def _split_w_in(w_in):
    c = lambda a, b_: w_in[:, a:b_]
    pad = jnp.zeros((w_in.shape[0], LANE - 64 - IDX_HEADS), w_in.dtype)
    main = jnp.concatenate([
        c(1536, 3328),
        c(5888, 6016),
        c(6528, 6600), pad,
        c(0, 1536),
        c(3328, 5376),
        c(5376, 5888), c(6016, 6528),
    ], axis=1).astype(BF16)
    gate = c(6600, 6600 + N_BRANCH * D_MODEL).astype(BF16)
    return main, gate


def _layer(x2, b, s, cos, sin, layer, norm_g, w_in, diff_lambda, diff_subln_g, rwkv_mu, rwkv_w_up, rwkv_a_up,
           rwkv_g_up, rwkv_w0, rwkv_a0, rwkv_k_k, rwkv_k_a, rwkv_r_k, rwkv_gn_g, rwkv_gn_b, hgrn_lb, hgrn_norm_g,
           w_branch, w_out, mlp_w1, mlp_w2):
    w_main, w_gate = _split_w_in(w_in)
    h = rmsnorm_bf16(x2, norm_g[0])
    pm = matmul(h, w_main, None, F32, name="proj_main").reshape(b, s, N_MAIN)
    gate = matmul(h, w_gate, "sigmoid", BF16, name="proj_gate")
    cos3 = cos.reshape(b, s, LANE)
    sin3 = sin.reshape(b, s, LANE)
    y_a = diff_attention(pm, cos3, sin3, diff_lambda, diff_subln_g, layer)
    y_b = rwkv7(pm, rwkv_mu, rwkv_w_up, rwkv_a_up, rwkv_g_up, rwkv_w0, rwkv_a0, rwkv_k_k, rwkv_k_a,
                rwkv_r_k, rwkv_gn_g, rwkv_gn_b)
    y_c = hgrn2(pm, hgrn_lb, hgrn_norm_g)
    y_d = dsa_attention(pm, cos3, sin3)
    m = b * s
    ys = [y.reshape(m, D_BRANCH) for y in (y_a, y_b, y_c, y_d)]
    merged = gated_merge(ys, w_branch.astype(BF16), gate)
    x2 = matmul_norm_residual(merged, w_out.astype(BF16), x2, norm_g[1], name="out_proj")
    h2 = rmsnorm_bf16(x2, norm_g[2])
    ff = matmul(h2, mlp_w1.astype(BF16), "relu2", BF16, name="mlp_up")
    return matmul_norm_residual(ff, mlp_w2.astype(BF16), x2, norm_g[3], name="mlp_down")


def kernel(x, positions, norm_g, w_in, diff_lambda, diff_subln_g, rwkv_mu, rwkv_w_up, rwkv_a_up, rwkv_g_up, rwkv_w0, rwkv_a0, rwkv_k_k, rwkv_k_a, rwkv_r_k, rwkv_gn_g, rwkv_gn_b, hgrn_lb_logits, hgrn_norm_g, w_branch, w_out, mlp_w1, mlp_w2):
    b, s, d = x.shape
    depth = w_in.shape[0]
    lb = jax.nn.softmax(hgrn_lb_logits.astype(F32), axis=0)
    lb = jnp.cumsum(lb, axis=0) - lb[0]
    cos, sin = rope_tables(positions)
    x2 = x.reshape(b * s, d)
    for l in range(depth):
        x2 = _layer(x2, b, s, cos, sin, l, norm_g[l], w_in[l], diff_lambda[l], diff_subln_g[l], rwkv_mu[l],
                    rwkv_w_up[l], rwkv_a_up[l], rwkv_g_up[l], rwkv_w0[l], rwkv_a0[l], rwkv_k_k[l], rwkv_k_a[l],
                    rwkv_r_k[l], rwkv_gn_g[l], rwkv_gn_b[l], lb[l], hgrn_norm_g[l], w_branch[l], w_out[l],
                    mlp_w1[l], mlp_w2[l])
    return x2.reshape(b, s, d)
```

```python
import functools
import math

import numpy as np
import jax
import jax.numpy as jnp
from jax import lax
from jax.experimental import pallas as pl
from jax.experimental.pallas import tpu as pltpu

F32 = jnp.float32
BF16 = jnp.bfloat16

D_MODEL = 2048
D_BRANCH = 512
D_FF = 8192
N_BRANCH = 4
CHUNK = 64
EPS = 1e-6
ROPE_THETA = 10000.0
HEAD_DIM = 64
DIFF_HEADS = 4
RWKV_GN_EPS = 64e-5
DSA_HEADS = 8
IDX_HEADS = 8
TOPK_MAX = 256

LANE = 128
VMEM_LIMIT = 56 * 1024 * 1024

U_B, U_DKV, U_DIKW = 0, 14, 15
U_AQ, U_AK, U_AV = 16, 20, 24
U_CQ, U_CF, U_CI, U_CG = 28, 32, 36, 40
U_DQ, U_DIQ = 44, 48
N_MAIN = 52 * LANE
B_WIDTH = 1792

NEG = -1e30
INT_MIN = -2147483648
LOG2E = 1.4426950408889634

NT_DIMS = (((1,), (1,)), ((), ()))
TN_DIMS = (((0,), (0,)), ((), ()))


def _cp(sem, vmem=VMEM_LIMIT):
    return pltpu.CompilerParams(dimension_semantics=sem, vmem_limit_bytes=vmem)


def _dot(a, b, dims=None):
    if dims is None:
        return jnp.dot(a, b, preferred_element_type=F32)
    return lax.dot_general(a, b, dims, preferred_element_type=F32)


def _dot_hi(a, b, dims=(((1,), (0,)), ((), ()))):
    return lax.dot_general(a, b, dims, preferred_element_type=F32,
                           precision=lax.Precision.HIGHEST)


def _rmsnorm_kernel(x_ref, g_ref, o_ref):
    x = x_ref[...]
    ms = jnp.mean(x * x, axis=-1, keepdims=True)
    o_ref[...] = (x * lax.rsqrt(ms + EPS) * g_ref[...]).astype(o_ref.dtype)


def rmsnorm_bf16(x, g, tm=512):
    m, d = x.shape
    return pl.pallas_call(
        _rmsnorm_kernel,
        grid=(m // tm,),
        in_specs=[pl.BlockSpec((tm, d), lambda i: (i, 0)),
                  pl.BlockSpec((1, d), lambda i: (0, 0))],
        out_specs=pl.BlockSpec((tm, d), lambda i: (i, 0)),
        out_shape=jax.ShapeDtypeStruct((m, d), BF16),
        compiler_params=_cp(("parallel",)),
        name="rmsnorm",
    )(x, g.reshape(1, d))


def _mm_kernel(x_ref, w_ref, o_ref, *, act):
    acc = _dot(x_ref[...], w_ref[...])
    if act == "sigmoid":
        acc = jax.nn.sigmoid(acc)
    elif act == "relu2":
        acc = jnp.square(jnp.maximum(acc, 0.0))
    o_ref[...] = acc.astype(o_ref.dtype)


def matmul(x, w, act, out_dtype, tm=1024, tn=512, name="mm"):
    m, k = x.shape
    n = w.shape[1]
    return pl.pallas_call(
        functools.partial(_mm_kernel, act=act),
        grid=(n // tn, m // tm),
        in_specs=[pl.BlockSpec((tm, k), lambda j, i: (i, 0)),
                  pl.BlockSpec((k, tn), lambda j, i: (0, j))],
        out_specs=pl.BlockSpec((tm, tn), lambda j, i: (i, j)),
        out_shape=jax.ShapeDtypeStruct((m, n), out_dtype),
        compiler_params=_cp(("parallel", "parallel")),
        name=name,
    )(x, w)


def _mm_norm_res_kernel(a_ref, w_ref, x_ref, g_ref, o_ref, acc_ref):
    kk = pl.program_id(1)

    @pl.when(kk == 0)
    def _():
        acc_ref[...] = jnp.zeros_like(acc_ref)

    acc_ref[...] += _dot(a_ref[...], w_ref[...])

    @pl.when(kk == pl.num_programs(1) - 1)
    def _():
        y = acc_ref[...]
        ms = jnp.mean(y * y, axis=-1, keepdims=True)
        o_ref[...] = x_ref[...] + y * lax.rsqrt(ms + EPS) * g_ref[...]


def matmul_norm_residual(a, w, x, g, tm=512, tk=1024, name="mm_norm_res"):
    m, k = a.shape
    n = w.shape[1]
    return pl.pallas_call(
        _mm_norm_res_kernel,
        grid=(m // tm, k // tk),
        in_specs=[pl.BlockSpec((tm, tk), lambda i, kk: (i, kk)),
                  pl.BlockSpec((tk, n), lambda i, kk: (kk, 0)),
                  pl.BlockSpec((tm, n), lambda i, kk: (i, 0)),
                  pl.BlockSpec((1, n), lambda i, kk: (0, 0))],
        out_specs=pl.BlockSpec((tm, n), lambda i, kk: (i, 0)),
        out_shape=jax.ShapeDtypeStruct((m, n), F32),
        scratch_shapes=[pltpu.VMEM((tm, n), F32)],
        compiler_params=_cp(("parallel", "arbitrary")),
        name=name,
    )(a, w, x, g.reshape(1, n))


def _merge_kernel(ya, yb, yc, yd, wb, ga, gb, gc, gd, o_ref):
    acc = None
    for n, (y, g) in enumerate(((ya, ga), (yb, gb), (yc, gc), (yd, gd))):
        t = g[...].astype(F32) * _dot(y[...], wb[n])
        acc = t if acc is None else acc + t
    o_ref[...] = acc.astype(o_ref.dtype)


def gated_merge(ys, wb, gate, tm=512, tn=1024):
    m = ys[0].shape[0]
    nj = D_MODEL // tn
    y_spec = pl.BlockSpec((tm, D_BRANCH), lambda j, i: (i, 0))
    g_specs = [pl.BlockSpec((tm, tn), functools.partial(lambda j, i, n: (i, n * nj + j), n=n))
               for n in range(N_BRANCH)]
    return pl.pallas_call(
        _merge_kernel,
        grid=(nj, m // tm),
        in_specs=[y_spec] * 4 + [pl.BlockSpec((N_BRANCH, D_BRANCH, tn), lambda j, i: (0, 0, j))] + g_specs,
        out_specs=pl.BlockSpec((tm, tn), lambda j, i: (i, j)),
        out_shape=jax.ShapeDtypeStruct((m, D_MODEL), BF16),
        compiler_params=_cp(("parallel", "parallel")),
        name="gated_merge",
    )(*ys, wb, gate, gate, gate, gate)


def _rope_table_kernel(pos_ref, inv_ref, sgn_ref, cos_ref, sin_ref):
    ang = pos_ref[...].astype(F32) * inv_ref[...]
    cos_ref[...] = jnp.cos(ang)
    sin_ref[...] = jnp.sin(ang) * sgn_ref[...]


def rope_tables(positions, tm=512):
    m = positions.size
    half = HEAD_DIM // 2
    inv = ROPE_THETA ** (-np.arange(0, HEAD_DIM, 2, dtype=np.float32) / HEAD_DIM)
    inv = np.tile(inv.astype(np.float32), 4).reshape(1, LANE)
    sgn = np.tile(np.concatenate([-np.ones(half, np.float32), np.ones(half, np.float32)]), 2).reshape(1, LANE)
    spec = pl.BlockSpec((tm, LANE), lambda i: (i, 0))
    cst = pl.BlockSpec((1, LANE), lambda i: (0, 0))
    return pl.pallas_call(
        _rope_table_kernel,
        grid=(m // tm,),
        in_specs=[pl.BlockSpec((tm, 1), lambda i: (i, 0)), cst, cst],
        out_specs=[spec, spec],
        out_shape=[jax.ShapeDtypeStruct((m, LANE), F32)] * 2,
        compiler_params=_cp(("parallel",)),
        name="rope_tables",
    )(positions.reshape(m, 1), jnp.asarray(inv), jnp.asarray(sgn))


def _rope(x, cos, sin):
    w = x.shape[1]
    n = w // LANE
    if n > 1:
        cos = jnp.concatenate([cos] * n, axis=1)
        sin = jnp.concatenate([sin] * n, axis=1)
    lane = lax.broadcasted_iota(jnp.int32, x.shape, 1)
    up = pltpu.roll(x, w - HEAD_DIM // 2, axis=1)
    dn = pltpu.roll(x, HEAD_DIM // 2, axis=1)
    rot = jnp.where((lane & (HEAD_DIM // 2)) == 0, up, dn)
    return x * cos + rot * sin


def _rope_qkv_kernel(q_ref, k_ref, v_ref, cos_ref, sin_ref, qo_ref, ko_ref, vo_ref, *, scale):
    cos = cos_ref[0]
    sin = sin_ref[0]
    hd = HEAD_DIM
    q = (_rope(q_ref[0], cos, sin) * scale).astype(qo_ref.dtype)
    k = _rope(k_ref[0], cos, sin).astype(ko_ref.dtype)
    for h in range(q.shape[1] // hd):
        qo_ref[0, h] = q[:, h * hd:(h + 1) * hd]
        ko_ref[0, h] = k[:, h * hd:(h + 1) * hd]
    vo_ref[0] = v_ref[0].astype(vo_ref.dtype)


def rope_qkv(pm, cos, sin, uq, uk, uv, scale, ts=512):
    b, s, _ = pm.shape
    wq = D_BRANCH
    nh = wq // HEAD_DIM
    tab = pl.BlockSpec((1, ts, LANE), lambda bi, i: (bi, i, 0))
    col = lambda u: pl.BlockSpec((1, ts, wq), lambda bi, i: (bi, i, u * LANE // wq))
    heads = pl.BlockSpec((1, nh, ts, HEAD_DIM), lambda bi, i: (bi, 0, i, 0))
    return pl.pallas_call(
        functools.partial(_rope_qkv_kernel, scale=scale),
        grid=(b, s // ts),
        in_specs=[col(uq), col(uk), col(uv), tab, tab],
        out_specs=[heads, heads, pl.BlockSpec((1, ts, wq), lambda bi, i: (bi, i, 0))],
        out_shape=[jax.ShapeDtypeStruct((b, nh, s, HEAD_DIM), BF16)] * 2
                  + [jax.ShapeDtypeStruct((b, s, wq), BF16)],
        compiler_params=_cp(("parallel", "parallel")),
        name="rope_qkv",
    )(pm, pm, pm, cos, sin)


def _lane_fold(x, op):
    out = x[..., 0:LANE]
    for c in range(1, x.shape[-1] // LANE):
        out = op(out, x[..., c * LANE:(c + 1) * LANE])
    return out


def _paired_loop(n, step):
    def body(jj, carry):
        step(2 * jj)
        step(2 * jj + 1)
        return carry
    lax.fori_loop(0, n // 2, body, 0)

    @pl.when(n % 2 == 1)
    def _():
        step(n - 1)


def _diff_attn_kernel(q_ref, k_ref, v_ref, lam_ref, g_ref, o_ref, s_ref, m_ref, l_ref, acc_ref, *, tq, tk, lam_init):
    i = pl.program_id(2)
    nfull = (i * tq) // tk
    row = lax.broadcasted_iota(jnp.int32, (tq, tk), 0)
    col = lax.broadcasted_iota(jnp.int32, (tq, tk), 1)
    vis = (col >> 6) <= (row >> 6)

    m_ref[...] = jnp.full_like(m_ref, NEG)

    def max_step(j, masked=False):
        for mm in range(2):
            kb = k_ref[0, mm, pl.ds(pl.multiple_of(j * tk, tk), tk), :]
            s = _dot(q_ref[0, mm], kb, NT_DIMS)
            if masked:
                s = jnp.where(vis, s, NEG)
            s_ref[j, mm] = s
            m_ref[mm] = jnp.maximum(m_ref[mm], _lane_fold(s, jnp.maximum))

    _paired_loop(nfull, max_step)
    max_step(nfull, True)
    for mm in range(2):
        m_ref[mm] = jnp.broadcast_to(jnp.max(m_ref[mm], axis=1, keepdims=True), (tq, LANE))
    l_ref[...] = jnp.zeros_like(l_ref)
    acc_ref[...] = jnp.zeros_like(acc_ref)

    def acc_step(j):
        vb = v_ref[0, pl.ds(pl.multiple_of(j * tk, tk), tk), :]
        for mm in range(2):
            s = s_ref[j, mm]
            mb = m_ref[mm]
            ps = [jnp.exp2(s[:, c * LANE:(c + 1) * LANE] - mb) for c in range(tk // LANE)]
            l_ref[mm] += functools.reduce(jnp.add, ps)
            acc_ref[mm] += _dot(jnp.concatenate(ps, axis=1).astype(BF16), vb)

    _paired_loop(nfull + 1, acc_step)

    lv = lam_ref[...]
    lam = (jnp.exp(jnp.sum(lv[0:1] * lv[1:2], keepdims=True))
           - jnp.exp(jnp.sum(lv[2:3] * lv[3:4], keepdims=True)) + lam_init)
    l0 = jnp.sum(l_ref[0], axis=1, keepdims=True)
    l1 = jnp.sum(l_ref[1], axis=1, keepdims=True)
    o = acc_ref[0] / l0 - lam * (acc_ref[1] / l1)
    ms = jnp.mean(o * o, axis=-1, keepdims=True)
    o = o * lax.rsqrt(ms + EPS) * g_ref[...] * (1.0 - lam_init)
    o_ref[0] = o.astype(o_ref.dtype)


def diff_attention(pm, cos, sin, lam_vecs, subln_g, layer, tq=256):
    b, s, _ = pm.shape
    tk = tq
    qa, ka, va = rope_qkv(pm, cos, sin, U_AQ, U_AK, U_AV, HEAD_DIM ** -0.5 * LOG2E)
    lam_init = 0.8 - 0.6 * math.exp(-0.3 * layer)
    return pl.pallas_call(
        functools.partial(_diff_attn_kernel, tq=tq, tk=tk, lam_init=lam_init),
        grid=(b, DIFF_HEADS, s // tq),
        in_specs=[pl.BlockSpec((1, 2, tq, HEAD_DIM), lambda bi, h, i: (bi, h, i, 0)),
                  pl.BlockSpec((1, 2, s, HEAD_DIM), lambda bi, h, i: (bi, h, 0, 0)),
                  pl.BlockSpec((1, s, LANE), lambda bi, h, i: (bi, 0, h)),
                  pl.BlockSpec((4, HEAD_DIM), lambda bi, h, i: (0, 0)),
                  pl.BlockSpec((1, LANE), lambda bi, h, i: (0, 0))],
        out_specs=pl.BlockSpec((1, tq, LANE), lambda bi, h, i: (bi, i, h)),
        out_shape=jax.ShapeDtypeStruct((b, s, D_BRANCH), BF16),
        scratch_shapes=[pltpu.VMEM((s // tk, 2, tq, tk), F32),
                        pltpu.VMEM((2, tq, LANE), F32), pltpu.VMEM((2, tq, LANE), F32),
                        pltpu.VMEM((2, tq, LANE), F32)],
        compiler_params=_cp(("parallel", "parallel", "parallel")),
        name="diff_attention",
    )(qa, ka, va, lam_vecs, subln_g.reshape(1, LANE))


def _softplus(z):
    return jnp.maximum(z, 0.0) + jnp.log(1.0 + jnp.exp(-jnp.abs(z)))


def _rwkv_prep_kernel(x_ref, prev_ref, mu_ref, wup_ref, aup_ref, gup_ref, w0_ref, a0_ref, kkg_ref, kag_ref,
                      seg_ref, r_ref, k_ref, v_ref, al_ref, be_ref, lw_ref, g_ref):
    i = pl.program_id(1)
    p = x_ref[0]
    t = p.shape[0]
    row = lax.broadcasted_iota(jnp.int32, p.shape, 0)
    last = prev_ref[0][7:8, :]
    last = jnp.where(i == 0, jnp.zeros_like(last), last)
    prev = jnp.where(row == 0, jnp.broadcast_to(last, p.shape), pltpu.roll(p, 1, axis=0))
    ps = p + (prev - p) * mu_ref[...]
    d = D_BRANCH
    r, k, v = ps[:, 0:d], ps[:, d:2 * d], ps[:, 2 * d:3 * d]
    wd = ps[:, 3 * d:3 * d + 64]
    ad = ps[:, 3 * d + 64:3 * d + 128]
    gd = ps[:, 3 * d + 128:3 * d + 256]
    w_log = -_softplus(-(w0_ref[...] + _dot_hi(jnp.tanh(wd), wup_ref[...]))) - 0.5
    a = jax.nn.sigmoid(a0_ref[...] + _dot_hi(ad, aup_ref[...]))
    g = _dot_hi(jax.nn.sigmoid(gd), gup_ref[...])
    kk = k * kkg_ref[...]
    nrm = jnp.sqrt(_dot_hi(kk * kk, seg_ref[...]))
    kk = kk / jnp.maximum(nrm, 1e-12)
    r_ref[0] = r
    k_ref[0] = k * (1.0 + (a - 1.0) * kag_ref[...])
    v_ref[0] = v
    al_ref[0] = -kk
    be_ref[0] = kk * a
    lw_ref[0] = -jnp.exp(w_log)
    g_ref[0] = g


def _rwkv_scan_kernel(r_ref, k_ref, v_ref, al_ref, be_ref, lw_ref, g_ref, rk_ref, gng_ref, gnb_ref,
                      o_ref, st_ref, *, rows, sub):
    t = pl.program_id(2)
    hd = HEAD_DIM
    nch = rows // CHUNK

    @pl.when(t == 0)
    def _():
        st_ref[...] = jnp.zeros_like(st_ref)

    lw = lw_ref[0]
    rowc = lax.broadcasted_iota(jnp.int32, lw.shape, 0) & (CHUNK - 1)
    cum = lw
    for dd in (1, 2, 4, 8, 16, 32):
        cum = cum + jnp.where(rowc >= dd, pltpu.roll(cum, dd, axis=0), 0.0)
    cl = jnp.concatenate(
        [jnp.broadcast_to(cum[(c + 1) * CHUNK - 1:(c + 1) * CHUNK, :], (CHUNK, LANE)) for c in range(nch)], axis=0)
    e_in = jnp.exp(cum)
    e_out = jnp.exp(-cum)
    e_end = jnp.exp(cl - cum)
    r2, k2, v2, al2, be2 = r_ref[0], k_ref[0], v_ref[0], al_ref[0], be_ref[0]
    at2 = al2 * jnp.exp(cum - lw)
    rt2 = r2 * e_in
    bt2 = be2 * e_out
    kt2 = k2 * e_out
    bh2 = be2 * e_end
    kh2 = k2 * e_end
    wl2 = jnp.exp(cl)

    row = lax.broadcasted_iota(jnp.int32, (sub, sub), 0)
    col = lax.broadcasted_iota(jnp.int32, (sub, sub), 1)
    same = (row >> 6) == (col >> 6)
    m_strict = same & (row > col)
    m_incl = same & (row >= col)
    eye = (row == col).astype(F32)
    m_blk8 = (row >> 3) == (col >> 3)
    m_lvls = [((row >> (sh + 1)) == (col >> (sh + 1))) & ((row >> sh) != (col >> sh)) for sh in (3, 4, 5)]
    g2 = g_ref[0]

    def summarize(rs, sl):
        at, rt, vb = at2[rs, sl], rt2[rs, sl], v2[rs, sl].astype(BF16)
        lhs = jnp.concatenate([at, rt], axis=0).astype(BF16)
        rhs = jnp.concatenate([bt2[rs, sl], kt2[rs, sl]], axis=0).astype(BF16)
        gm = _dot(lhs, rhs, NT_DIMS)
        n_ab = jnp.where(m_strict, gm[:sub, :sub], 0.0)
        a_ak = jnp.where(m_strict, gm[:sub, sub:], 0.0).astype(BF16)
        a_rb = jnp.where(m_incl, gm[sub:, :sub], 0.0).astype(BF16)
        a_rk = jnp.where(m_incl, gm[sub:, sub:], 0.0).astype(BF16)
        n8 = jnp.where(m_blk8, n_ab, 0.0)
        n8b = n8.astype(BF16)
        n_2 = _dot(n8b, n8b)
        n2b = n_2.astype(BF16)
        n_3 = _dot(n2b, n8b)
        n_4 = _dot(n2b, n2b)
        tinv = eye + n8 + n_2 + n_3
        tinv = tinv + _dot(tinv.astype(BF16), n_4.astype(BF16))
        for m_lvl in m_lvls:
            nl = jnp.where(m_lvl, n_ab, 0.0).astype(BF16)
            tb = tinv.astype(BF16)
            tinv = tinv + _dot(_dot(tb, nl).astype(BF16), tb)
        tb = tinv.astype(BF16)
        akv = _dot(a_ak, vb)
        pmat = _dot(tb, jnp.concatenate([at, akv], axis=1).astype(BF16))
        qmat = _dot(a_rb, pmat.astype(BF16)) + jnp.concatenate([rt, _dot(a_rk, vb)], axis=1)
        return pmat.astype(BF16), qmat

    outs = []
    for hh in range(2):
        sl = slice(hh * hd, (hh + 1) * hd)
        v, r, kp = v2[:, sl], r2[:, sl], k2[:, sl]
        parts = [summarize(slice(sb * sub, (sb + 1) * sub), sl) for sb in range(rows // sub)]
        st = st_ref[hh]
        ys = []
        for c in range(nch):
            pmat, qmat = parts[(c * CHUNK) // sub]
            lo = (c * CHUNK) % sub
            cs = slice(c * CHUNK, (c + 1) * CHUNK)
            stb = st.astype(BF16)
            ys.append(_dot(qmat[lo:lo + CHUNK, :hd].astype(BF16), stb, NT_DIMS) + qmat[lo:lo + CHUNK, hd:])
            mn = _dot(pmat[lo:lo + CHUNK], bh2[cs, sl].astype(BF16), TN_DIMS)
            n_c = mn[hd:] + _dot(v[cs].astype(BF16), kh2[cs, sl].astype(BF16), TN_DIMS)
            st = st * wl2[c * CHUNK:c * CHUNK + 1, sl] + _dot(stb, mn[:hd].astype(BF16)) + n_c
        st_ref[hh] = st
        y = jnp.concatenate(ys, axis=0)
        mu = jnp.mean(y, axis=-1, keepdims=True)
        var = jnp.mean(jnp.square(y - mu), axis=-1, keepdims=True)
        yn = (y - mu) * lax.rsqrt(var + RWKV_GN_EPS) * gng_ref[:, sl] + gnb_ref[:, sl]
        bonus = jnp.sum(r * kp * rk_ref[0, hh:hh + 1, :], axis=-1, keepdims=True) * v
        outs.append((yn + bonus) * g2[:, sl])
    o_ref[0] = jnp.concatenate(outs, axis=1).astype(o_ref.dtype)


def rwkv7(pm, mu, w_up, a_up, g_up, w0, a0, kk_gain, ka_gain, r_k, gn_g, gn_b, ts=256, rows=512, sub=256):
    b, s, _ = pm.shape
    d = D_BRANCH
    seg = np.kron(np.eye(d // HEAD_DIM, dtype=np.float32), np.ones((HEAD_DIM, HEAD_DIM), np.float32))
    row1 = lambda a: a.reshape(1, -1)
    cst = lambda shape: pl.BlockSpec(shape, lambda bi, i: (0,) * len(shape))
    blk = pl.BlockSpec((1, ts, d), lambda bi, i: (bi, i, 0))
    r, k, v, al, be, lw, g = pl.pallas_call(
        _rwkv_prep_kernel,
        grid=(b, s // ts),
        in_specs=[pl.BlockSpec((1, ts, B_WIDTH), lambda bi, i: (bi, i, 0)),
                  pl.BlockSpec((1, 8, B_WIDTH), lambda bi, i: (bi, jnp.maximum(i * (ts // 8) - 1, 0), 0)),
                  cst((1, B_WIDTH)), cst((64, d)), cst((64, d)), cst((128, d)),
                  cst((1, d)), cst((1, d)), cst((1, d)), cst((1, d)), cst((d, d))],
        out_specs=[blk] * 7,
        out_shape=[jax.ShapeDtypeStruct((b, s, d), F32)] * 7,
        compiler_params=_cp(("parallel", "arbitrary")),
        name="rwkv_prep",
    )(pm, pm, row1(mu), w_up, a_up, g_up, row1(w0), row1(a0), row1(kk_gain), row1(ka_gain), jnp.asarray(seg))

    hp = pl.BlockSpec((1, rows, LANE), lambda bi, h, t: (bi, t, h))
    par = pl.BlockSpec((1, LANE), lambda bi, h, t: (0, h))
    return pl.pallas_call(
        functools.partial(_rwkv_scan_kernel, rows=rows, sub=sub),
        grid=(b, d // LANE, s // rows),
        in_specs=[hp] * 7 + [pl.BlockSpec((1, 2, HEAD_DIM), lambda bi, h, t: (h, 0, 0)), par, par],
        out_specs=hp,
        out_shape=jax.ShapeDtypeStruct((b, s, d), BF16),
        scratch_shapes=[pltpu.VMEM((2, HEAD_DIM, HEAD_DIM), F32)],
        compiler_params=_cp(("parallel", "parallel", "arbitrary")),
        name="rwkv_scan",
    )(r, k, v, al, be, lw, g, r_k.reshape(-1, 2, HEAD_DIM), row1(gn_g), row1(gn_b))


def _hgrn_kernel(q_ref, f_ref, i_ref, g_ref, lb_ref, ng_ref, o_ref, st_ref, *, rows):
    t = pl.program_id(2)
    sub = 16

    @pl.when(t == 0)
    def _():
        st_ref[...] = jnp.zeros_like(st_ref)

    q = jax.nn.silu(q_ref[0])
    lb = lb_ref[...]
    fg = lb + (1.0 - lb) * jax.nn.sigmoid(f_ref[0])
    lf = jnp.log(fg)
    kf = 1.0 - fg
    iv = i_ref[0]
    rowc = lax.broadcasted_iota(jnp.int32, lf.shape, 0) & (CHUNK - 1)
    bcum = lf
    for dd in (1, 2, 4, 8, 16, 32):
        bcum = bcum + jnp.where(rowc >= dd, pltpu.roll(bcum, dd, axis=0), 0.0)
    ivb = iv.astype(BF16)
    rsub = lax.broadcasted_iota(jnp.int32, (sub, LANE), 0)
    st = st_ref[...]
    outs = []
    for c in range(rows // CHUNK):
        c0 = c * CHUNK
        bc = bcum[c0:c0 + CHUNK]
        qc = q[c0:c0 + CHUNK]
        kc = kf[c0:c0 + CHUNK]
        ic = iv[c0:c0 + CHUNK]
        icb = ivb[c0:c0 + CHUNK]
        o_inter = _dot((qc * jnp.exp(bc)).astype(BF16), st.astype(BF16), NT_DIMS)
        blocks = []
        for ib in range(CHUNK // sub):
            r0 = ib * sub
            bi = bc[r0:r0 + sub]
            qi = qc[r0:r0 + sub]
            ki = kc[r0:r0 + sub]
            ii = ic[r0:r0 + sub]
            acc = jnp.zeros((sub, LANE), F32)
            for s_ in range(sub):
                e = jnp.where(rsub >= s_, jnp.exp(jnp.minimum(bi - bi[s_:s_ + 1], 0.0)), 0.0)
                a = jnp.sum(qi * (ki[s_:s_ + 1] * e), axis=-1, keepdims=True)
                acc = acc + a * ii[s_:s_ + 1]
            if ib > 0:
                ref_row = bc[r0 - 1:r0]
                qt = (qi * jnp.exp(bi - ref_row)).astype(BF16)
                kt = (kc[0:r0] * jnp.exp(ref_row - bc[0:r0])).astype(BF16)
                a_off = _dot(qt, kt, NT_DIMS).astype(BF16)
                acc = acc + _dot(a_off, icb[0:r0])
            blocks.append(acc)
        outs.append(o_inter + jnp.concatenate(blocks, axis=0))
        bl = bc[CHUNK - 1:CHUNK]
        kdec = (kc * jnp.exp(bl - bc)).astype(BF16)
        st = st * jnp.exp(bl) + _dot(icb, kdec, TN_DIMS)
    st_ref[...] = st
    o = jnp.concatenate(outs, axis=0)
    ms = jnp.mean(o * o, axis=-1, keepdims=True)
    o = o * lax.rsqrt(ms + EPS) * ng_ref[...]
    o_ref[0] = (o * jax.nn.silu(g_ref[0])).astype(o_ref.dtype)


def hgrn2(pm, lb, norm_g, rows=256):
    b, s, _ = pm.shape
    heads = D_BRANCH // LANE
    spec = lambda u: pl.BlockSpec((1, rows, LANE), lambda bi, h, t: (bi, t, u + h))
    par = pl.BlockSpec((1, LANE), lambda bi, h, t: (0, h))
    return pl.pallas_call(
        functools.partial(_hgrn_kernel, rows=rows),
        grid=(b, heads, s // rows),
        in_specs=[spec(U_CQ), spec(U_CF), spec(U_CI), spec(U_CG), par, par],
        out_specs=spec(0),
        out_shape=jax.ShapeDtypeStruct((b, s, D_BRANCH), BF16),
        scratch_shapes=[pltpu.VMEM((LANE, LANE), F32)],
        compiler_params=_cp(("parallel", "parallel", "arbitrary")),
        name="hgrn2",
    )(pm, pm, pm, pm, lb.reshape(1, -1), norm_g.reshape(1, -1))


def _dsa_prep_kernel(q_ref, iq_ref, kv_ref, ikw_ref, cos_ref, sin_ref,
                     qo_ref, iqo_ref, ko_ref, vo_ref, iko_ref, wo_ref, *, scale, wscale):
    cos = cos_ref[0]
    sin = sin_ref[0]
    hd = HEAD_DIM
    q = (_rope(q_ref[0], cos, sin) * scale).astype(BF16)
    iq = _rope(iq_ref[0], cos, sin).astype(BF16)
    for h in range(DSA_HEADS):
        qo_ref[0, h] = q[:, h * hd:(h + 1) * hd]
        iqo_ref[0, h] = iq[:, h * hd:(h + 1) * hd]
    kv = kv_ref[0]
    ko_ref[0] = _rope(kv, cos, sin)[:, :hd].astype(BF16)
    vo_ref[0] = kv[:, hd:].astype(BF16)
    ikw = ikw_ref[0]
    iko_ref[0] = _rope(ikw, cos, sin)[:, :hd].astype(BF16)
    wo_ref[0] = ikw[:, hd:hd + IDX_HEADS] * wscale


def _to_key(x):
    x = jnp.where(x == 0.0, 0.0, x)
    bits = pltpu.bitcast(x, jnp.int32)
    return jnp.where(bits < 0, bits ^ jnp.int32(0x7FFFFFFF), bits)


def _dsa_index_kernel(iq_ref, ik_ref, wt_ref, mask_ref, key_ref, *, tq, tk, top_k):
    i = pl.program_id(1)
    nk = key_ref.shape[0]
    nvis = (i * tq + tq - 1) // tk + 1
    krow = lax.broadcasted_iota(jnp.int32, (tk, tq), 0)
    qcol = i * tq + lax.broadcasted_iota(jnp.int32, (tk, tq), 1)
    wt = wt_ref[0]

    def score_block(jb, carry):
        ikb = ik_ref[0, pl.ds(pl.multiple_of(jb * tk, tk), tk), :]
        s_all = _dot(ikb, iq_ref[0].reshape(IDX_HEADS * tq, HEAD_DIM), NT_DIMS)
        acc = jnp.zeros((tk, tq), F32)
        for h in range(IDX_HEADS):
            acc = acc + jnp.maximum(s_all[:, h * tq:(h + 1) * tq], 0.0) * wt[h:h + 1, :]
        vis = ((jb * tk + krow) >> 6) <= (qcol >> 6)
        key_ref[jb] = jnp.where(vis, _to_key(acc), INT_MIN)
        return carry

    lax.fori_loop(0, nvis, score_block, 0)

    def count(pred_fn):
        def blk(jb, cnt):
            ind = jnp.where(pred_fn(key_ref[jb], jb), 1, 0)
            return cnt + jnp.sum(ind.reshape(tk // 8, 8, tq), axis=0)
        cnt = lax.fori_loop(0, nvis, blk, jnp.zeros((8, tq), jnp.int32))
        return jnp.sum(cnt, axis=0, keepdims=True)

    def bit_step(bi, prefix):
        cand = prefix | lax.shift_left(jnp.int32(1), 31 - bi)
        cand_s = cand ^ jnp.int32(INT_MIN)
        c = count(lambda kb, jb: kb >= cand_s)
        return jnp.where(c >= top_k, cand, prefix)

    prefix = lax.fori_loop(0, 32, bit_step, jnp.zeros((1, tq), jnp.int32))
    tau = prefix ^ jnp.int32(INT_MIN)
    n_gt = count(lambda kb, jb: kb > tau)
    n_eq = count(lambda kb, jb: kb == tau)
    need = top_k - n_gt
    tied = (n_eq > need) & (tau != INT_MIN)
    n_cols = nk * tk

    idx_bits = int(n_cols).bit_length()

    def tie_break():
        def idx_step(bi, pre):
            cand = pre | lax.shift_left(jnp.int32(1), idx_bits - 1 - bi)
            c = count(lambda kb, jb: (kb == tau) & ((jb * tk + krow) < cand))
            return jnp.where(c < need, cand, pre)
        cut = lax.fori_loop(0, idx_bits, idx_step, jnp.zeros((1, tq), jnp.int32))
        return jnp.where(tied, cut, n_cols)

    cut = lax.cond(jnp.max(tied.astype(jnp.int32)) > 0, tie_break,
                   lambda: jnp.full((1, tq), n_cols, jnp.int32))

    def write_block(jb, carry):
        kb = key_ref[jb]
        sel = (kb > tau) | ((kb == tau) & ((jb * tk + krow) <= cut))
        sel = sel & (kb != INT_MIN)
        mask_ref[0, jb] = jnp.where(sel, 1.0, 0.0).T.astype(mask_ref.dtype)
        return carry

    def zero_block(jb, carry):
        mask_ref[0, jb] = jnp.zeros((tq, tk), mask_ref.dtype)
        return carry

    lax.fori_loop(0, nvis, write_block, 0)
    lax.fori_loop(nvis, nk, zero_block, 0)


def _dsa_attn_kernel(q_ref, k_ref, v_ref, mask_ref, o_ref, s_ref, m_ref, l_ref, acc_ref, *, tq, tk):
    i = pl.program_id(1)
    nh = DSA_HEADS
    hd = HEAD_DIM
    nvis = (i * tq + tq - 1) // tk + 1
    nchunk = tk // LANE

    m_ref[...] = jnp.full_like(m_ref, NEG)

    def max_step(j):
        kb = k_ref[0, pl.ds(pl.multiple_of(j * tk, tk), tk), :]
        s = _dot(q_ref[0].reshape(nh * tq, hd), kb, NT_DIMS).reshape(nh, tq, tk)
        s = jnp.where((mask_ref[0, j] > 0)[None], s, NEG)
        s_ref[j] = s
        m_ref[...] = jnp.maximum(m_ref[...], _lane_fold(s, jnp.maximum))

    _paired_loop(nvis, max_step)
    m_ref[...] = jnp.broadcast_to(jnp.max(m_ref[...], axis=2, keepdims=True), m_ref.shape)
    l_ref[...] = jnp.zeros_like(l_ref)
    acc_ref[...] = jnp.zeros_like(acc_ref)

    def acc_step(j):
        s = s_ref[j]
        mb = m_ref[...]
        ps = [jnp.exp2(s[:, :, c * LANE:(c + 1) * LANE] - mb) for c in range(nchunk)]
        l_ref[...] += functools.reduce(jnp.add, ps)
        p = jnp.concatenate(ps, axis=2).astype(BF16).reshape(nh * tq, tk)
        vb = v_ref[0, pl.ds(pl.multiple_of(j * tk, tk), tk), :]
        acc_ref[...] += _dot(p, vb).reshape(nh, tq, hd)

    _paired_loop(nvis, acc_step)
    o = acc_ref[...] / jnp.sum(l_ref[...], axis=2, keepdims=True)
    o_ref[0] = jnp.concatenate([o[h] for h in range(nh)], axis=1).astype(o_ref.dtype)


def dsa_attention(pm, cos, sin, ts=512, tq=128, tk=512):
    b, s, _ = pm.shape
    top_k = min(TOPK_MAX, s // 4)
    hd = HEAD_DIM
    nk = s // tk
    wide = lambda u: pl.BlockSpec((1, ts, D_BRANCH), lambda bi, i: (bi, i, u * LANE // D_BRANCH))
    one = lambda u: pl.BlockSpec((1, ts, LANE), lambda bi, i: (bi, i, u))
    heads_out = pl.BlockSpec((1, DSA_HEADS, ts, hd), lambda bi, i: (bi, 0, i, 0))
    narrow = lambda w: pl.BlockSpec((1, ts, w), lambda bi, i: (bi, i, 0))
    q, iq, k, v, ik, w = pl.pallas_call(
        functools.partial(_dsa_prep_kernel, scale=hd ** -0.5 * LOG2E, wscale=(IDX_HEADS ** -0.5) * (hd ** -0.5)),
        grid=(b, s // ts),
        in_specs=[wide(U_DQ), wide(U_DIQ), one(U_DKV), one(U_DIKW), one(0), one(0)],
        out_specs=[heads_out, heads_out, narrow(hd), narrow(hd), narrow(hd), narrow(IDX_HEADS)],
        out_shape=[jax.ShapeDtypeStruct((b, DSA_HEADS, s, hd), BF16)] * 2
                  + [jax.ShapeDtypeStruct((b, s, hd), BF16)] * 3
                  + [jax.ShapeDtypeStruct((b, s, IDX_HEADS), F32)],
        compiler_params=_cp(("parallel", "parallel")),
        name="dsa_prep",
    )(pm, pm, pm, pm, cos, sin)

    mask_spec = pl.BlockSpec((1, nk, tq, tk), lambda bi, i: (bi, 0, i, 0))
    whole = pl.BlockSpec((1, s, hd), lambda bi, i: (bi, 0, 0))
    mask = pl.pallas_call(
        functools.partial(_dsa_index_kernel, tq=tq, tk=tk, top_k=top_k),
        grid=(b, s // tq),
        in_specs=[pl.BlockSpec((1, IDX_HEADS, tq, hd), lambda bi, i: (bi, 0, i, 0)),
                  whole,
                  pl.BlockSpec((1, IDX_HEADS, tq), lambda bi, i: (bi, 0, i))],
        out_specs=mask_spec,
        out_shape=jax.ShapeDtypeStruct((b, nk, s, tk), BF16),
        scratch_shapes=[pltpu.VMEM((nk, tk, tq), jnp.int32)],
        compiler_params=_cp(("parallel", "parallel")),
        name="dsa_index",
    )(iq, ik, jnp.swapaxes(w, 1, 2))

    return pl.pallas_call(
        functools.partial(_dsa_attn_kernel, tq=tq, tk=tk),
        grid=(b, s // tq),
        in_specs=[pl.BlockSpec((1, DSA_HEADS, tq, hd), lambda bi, i: (bi, 0, i, 0)),
                  whole, whole, mask_spec],
        out_specs=pl.BlockSpec((1, tq, D_BRANCH), lambda bi, i: (bi, i, 0)),
        out_shape=jax.ShapeDtypeStruct((b, s, D_BRANCH), BF16),
        scratch_shapes=[pltpu.VMEM((nk, DSA_HEADS, tq, tk), F32),
                        pltpu.VMEM((DSA_HEADS, tq, LANE), F32), pltpu.VMEM((DSA_HEADS, tq, LANE), F32),
                        pltpu.VMEM((DSA_HEADS, tq, hd), F32)],
        compiler_params=_cp(("parallel", "parallel")),
        name="dsa_attention",
    )(q, k, v, mask)


def _split_w_in(w_in):
    c = lambda a, b_: w_in[:, a:b_]
    pad = jnp.zeros((w_in.shape[0], LANE - 64 - IDX_HEADS), w_in.dtype)
    main = jnp.concatenate([
        c(1536, 3328),
        c(5888, 6016),
        c(6528, 6600), pad,
        c(0, 1536),
        c(3328, 5376),
        c(5376, 5888), c(6016, 6528),
    ], axis=1).astype(BF16)
    gate = c(6600, 6600 + N_BRANCH * D_MODEL).astype(BF16)
    return main, gate


def _layer(x2, b, s, cos, sin, layer, norm_g, w_in, diff_lambda, diff_subln_g, rwkv_mu, rwkv_w_up, rwkv_a_up,
           rwkv_g_up, rwkv_w0, rwkv_a0, rwkv_k_k, rwkv_k_a, rwkv_r_k, rwkv_gn_g, rwkv_gn_b, hgrn_lb, hgrn_norm_g,
           w_branch, w_out, mlp_w1, mlp_w2):
    w_main, w_gate = _split_w_in(w_in)
    h = rmsnorm_bf16(x2, norm_g[0])
    pm = matmul(h, w_main, None, F32, name="proj_main").reshape(b, s, N_MAIN)
    gate = matmul(h, w_gate, "sigmoid", BF16, name="proj_gate")
    cos3 = cos.reshape(b, s, LANE)
    sin3 = sin.reshape(b, s, LANE)
    y_a = diff_attention(pm, cos3, sin3, diff_lambda, diff_subln_g, layer)
    y_b = rwkv7(pm, rwkv_mu, rwkv_w_up, rwkv_a_up, rwkv_g_up, rwkv_w0, rwkv_a0, rwkv_k_k, rwkv_k_a,
                rwkv_r_k, rwkv_gn_g, rwkv_gn_b)
    y_c = hgrn2(pm, hgrn_lb, hgrn_norm_g)
    y_d = dsa_attention(pm, cos3, sin3)
    m = b * s
    ys = [y.reshape(m, D_BRANCH) for y in (y_a, y_b, y_c, y_d)]
    merged = gated_merge(ys, w_branch.astype(BF16), gate)
    x2 = matmul_norm_residual(merged, w_out.astype(BF16), x2, norm_g[1], name="out_proj")
    h2 = rmsnorm_bf16(x2, norm_g[2])
    ff = matmul(h2, mlp_w1.astype(BF16), "relu2", BF16, name="mlp_up")
    return matmul_norm_residual(ff, mlp_w2.astype(BF16), x2, norm_g[3], name="mlp_down")


def kernel(x, positions, norm_g, w_in, diff_lambda, diff_subln_g, rwkv_mu, rwkv_w_up, rwkv_a_up, rwkv_g_up, rwkv_w0, rwkv_a0, rwkv_k_k, rwkv_k_a, rwkv_r_k, rwkv_gn_g, rwkv_gn_b, hgrn_lb_logits, hgrn_norm_g, w_branch, w_out, mlp_w1, mlp_w2):
    b, s, d = x.shape
    depth = w_in.shape[0]
    lb = jax.nn.softmax(hgrn_lb_logits.astype(F32), axis=0)
    lb = jnp.cumsum(lb, axis=0) - lb[0]
    cos, sin = rope_tables(positions)
    x2 = x.reshape(b * s, d)
    for l in range(depth):
        x2 = _layer(x2, b, s, cos, sin, l, norm_g[l], w_in[l], diff_lambda[l], diff_subln_g[l], rwkv_mu[l],
                    rwkv_w_up[l], rwkv_a_up[l], rwkv_g_up[l], rwkv_w0[l], rwkv_a0[l], rwkv_k_k[l], rwkv_k_a[l],
                    rwkv_r_k[l], rwkv_gn_g[l], rwkv_gn_b[l], lb[l], hgrn_norm_g[l], w_branch[l], w_out[l],
                    mlp_w1[l], mlp_w2[l])
    return x2.reshape(b, s, d)
```

```python
import functools
import math

import numpy as np
import jax
import jax.numpy as jnp
from jax import lax
from jax.experimental import pallas as pl
from jax.experimental.pallas import tpu as pltpu

F32 = jnp.float32
BF16 = jnp.bfloat16

D_MODEL = 2048
D_BRANCH = 512
D_FF = 8192
N_BRANCH = 4
CHUNK = 64
EPS = 1e-6
ROPE_THETA = 10000.0
HEAD_DIM = 64
DIFF_HEADS = 4
RWKV_GN_EPS = 64e-5
DSA_HEADS = 8
IDX_HEADS = 8
TOPK_MAX = 256

LANE = 128
VMEM_LIMIT = 56 * 1024 * 1024

U_B, U_DKV, U_DIKW = 0, 14, 15
U_AQ, U_AK, U_AV = 16, 20, 24
U_CQ, U_CF, U_CI, U_CG = 28, 32, 36, 40
U_DQ, U_DIQ = 44, 48
N_MAIN = 52 * LANE
B_WIDTH = 1792

NEG = -1e30
INT_MIN = -2147483648
LOG2E = 1.4426950408889634

NT_DIMS = (((1,), (1,)), ((), ()))
TN_DIMS = (((0,), (0,)), ((), ()))


def _cp(sem, vmem=VMEM_LIMIT):
    return pltpu.CompilerParams(dimension_semantics=sem, vmem_limit_bytes=vmem)


def _dot(a, b, dims=None):
    if dims is None:
        return jnp.dot(a, b, preferred_element_type=F32)
    return lax.dot_general(a, b, dims, preferred_element_type=F32)


def _dot_hi(a, b, dims=(((1,), (0,)), ((), ()))):
    return lax.dot_general(a, b, dims, preferred_element_type=F32,
                           precision=lax.Precision.HIGHEST)


def _rmsnorm_kernel(x_ref, g_ref, o_ref):
    x = x_ref[...]
    ms = jnp.mean(x * x, axis=-1, keepdims=True)
    o_ref[...] = (x * lax.rsqrt(ms + EPS) * g_ref[...]).astype(o_ref.dtype)


def rmsnorm_bf16(x, g, tm=512):
    m, d = x.shape
    return pl.pallas_call(
        _rmsnorm_kernel,
        grid=(m // tm,),
        in_specs=[pl.BlockSpec((tm, d), lambda i: (i, 0)),
                  pl.BlockSpec((1, d), lambda i: (0, 0))],
        out_specs=pl.BlockSpec((tm, d), lambda i: (i, 0)),
        out_shape=jax.ShapeDtypeStruct((m, d), BF16),
        compiler_params=_cp(("parallel",)),
        name="rmsnorm",
    )(x, g.reshape(1, d))


def _mm_kernel(x_ref, w_ref, o_ref, *, act):
    acc = _dot(x_ref[...], w_ref[...])
    if act == "sigmoid":
        acc = jax.nn.sigmoid(acc)
    elif act == "relu2":
        acc = jnp.square(jnp.maximum(acc, 0.0))
    o_ref[...] = acc.astype(o_ref.dtype)


def matmul(x, w, act, out_dtype, tm=2048, tn=512, name="mm"):
    m, k = x.shape
    n = w.shape[1]
    tm = min(tm, m)
    return pl.pallas_call(
        functools.partial(_mm_kernel, act=act),
        grid=(m // tm, n // tn),
        in_specs=[pl.BlockSpec((tm, k), lambda i, j: (i, 0)),
                  pl.BlockSpec((k, tn), lambda i, j: (0, j))],
        out_specs=pl.BlockSpec((tm, tn), lambda i, j: (i, j)),
        out_shape=jax.ShapeDtypeStruct((m, n), out_dtype),
        compiler_params=_cp(("parallel", "parallel")),
        name=name,
    )(x, w)


def _norm_residual(y, x_ref, g_ref, o_ref):
    ms = jnp.mean(y * y, axis=-1, keepdims=True)
    o_ref[...] = x_ref[...] + y * lax.rsqrt(ms + EPS) * g_ref[...]


def _mm_norm_res_kernel(a_ref, w_ref, x_ref, g_ref, o_ref, acc_ref):
    kk = pl.program_id(1)

    @pl.when(kk == 0)
    def _():
        acc_ref[...] = jnp.zeros_like(acc_ref)

    acc_ref[...] += _dot(a_ref[...], w_ref[...])

    @pl.when(kk == pl.num_programs(1) - 1)
    def _():
        _norm_residual(acc_ref[...], x_ref, g_ref, o_ref)


def _mm_norm_res_fullk_kernel(a_ref, w_ref, x_ref, g_ref, o_ref):
    _norm_residual(_dot(a_ref[...], w_ref[...]), x_ref, g_ref, o_ref)


def matmul_norm_residual(a, w, x, g, tm=512, tk=2048, name="mm_norm_res"):
    m, k = a.shape
    n = w.shape[1]
    out_shape = jax.ShapeDtypeStruct((m, n), F32)
    g = g.reshape(1, n)
    if k <= tk:
        row = lambda i: (i, 0)
        return pl.pallas_call(
            _mm_norm_res_fullk_kernel,
            grid=(m // tm,),
            in_specs=[pl.BlockSpec((tm, k), row), pl.BlockSpec((k, n), lambda i: (0, 0)),
                      pl.BlockSpec((tm, n), row), pl.BlockSpec((1, n), lambda i: (0, 0))],
            out_specs=pl.BlockSpec((tm, n), row),
            out_shape=out_shape,
            compiler_params=_cp(("parallel",)),
            name=name,
        )(a, w, x, g)
    return pl.pallas_call(
        _mm_norm_res_kernel,
        grid=(m // tm, k // tk),
        in_specs=[pl.BlockSpec((tm, tk), lambda i, kk: (i, kk)),
                  pl.BlockSpec((tk, n), lambda i, kk: (kk, 0)),
                  pl.BlockSpec((tm, n), lambda i, kk: (i, 0)),
                  pl.BlockSpec((1, n), lambda i, kk: (0, 0))],
        out_specs=pl.BlockSpec((tm, n), lambda i, kk: (i, 0)),
        out_shape=out_shape,
        scratch_shapes=[pltpu.VMEM((tm, n), F32)],
        compiler_params=_cp(("parallel", "arbitrary")),
        name=name,
    )(a, w, x, g)


def _merge_kernel(ya, yb, yc, yd, wb, ga, gb, gc, gd, o_ref):
    acc = None
    for n, (y, g) in enumerate(((ya, ga), (yb, gb), (yc, gc), (yd, gd))):
        t = g[...].astype(F32) * _dot(y[...], wb[n])
        acc = t if acc is None else acc + t
    o_ref[...] = acc.astype(o_ref.dtype)


def gated_merge(ys, wb, gate, tm=512, tn=1024):
    m = ys[0].shape[0]
    nj = D_MODEL // tn
    y_spec = pl.BlockSpec((tm, D_BRANCH), lambda j, i: (i, 0))
    g_specs = [pl.BlockSpec((tm, tn), functools.partial(lambda j, i, n: (i, n * nj + j), n=n))
               for n in range(N_BRANCH)]
    return pl.pallas_call(
        _merge_kernel,
        grid=(nj, m // tm),
        in_specs=[y_spec] * 4 + [pl.BlockSpec((N_BRANCH, D_BRANCH, tn), lambda j, i: (0, 0, j))] + g_specs,
        out_specs=pl.BlockSpec((tm, tn), lambda j, i: (i, j)),
        out_shape=jax.ShapeDtypeStruct((m, D_MODEL), BF16),
        compiler_params=_cp(("parallel", "parallel")),
        name="gated_merge",
    )(*ys, wb, gate, gate, gate, gate)


def _rope_table_kernel(pos_ref, inv_ref, sgn_ref, cos_ref, sin_ref):
    ang = pos_ref[...].astype(F32) * inv_ref[...]
    cos_ref[...] = jnp.cos(ang)
    sin_ref[...] = jnp.sin(ang) * sgn_ref[...]


def rope_tables(positions, tm=512):
    m = positions.size
    half = HEAD_DIM // 2
    inv = ROPE_THETA ** (-np.arange(0, HEAD_DIM, 2, dtype=np.float32) / HEAD_DIM)
    inv = np.tile(inv.astype(np.float32), 4).reshape(1, LANE)
    sgn = np.tile(np.concatenate([-np.ones(half, np.float32), np.ones(half, np.float32)]), 2).reshape(1, LANE)
    spec = pl.BlockSpec((tm, LANE), lambda i: (i, 0))
    cst = pl.BlockSpec((1, LANE), lambda i: (0, 0))
    return pl.pallas_call(
        _rope_table_kernel,
        grid=(m // tm,),
        in_specs=[pl.BlockSpec((tm, 1), lambda i: (i, 0)), cst, cst],
        out_specs=[spec, spec],
        out_shape=[jax.ShapeDtypeStruct((m, LANE), F32)] * 2,
        compiler_params=_cp(("parallel",)),
        name="rope_tables",
    )(positions.reshape(m, 1), jnp.asarray(inv), jnp.asarray(sgn))


def _rope(x, cos, sin):
    w = x.shape[1]
    n = w // LANE
    if n > 1:
        cos = jnp.concatenate([cos] * n, axis=1)
        sin = jnp.concatenate([sin] * n, axis=1)
    lane = lax.broadcasted_iota(jnp.int32, x.shape, 1)
    up = pltpu.roll(x, w - HEAD_DIM // 2, axis=1)
    dn = pltpu.roll(x, HEAD_DIM // 2, axis=1)
    rot = jnp.where((lane & (HEAD_DIM // 2)) == 0, up, dn)
    return x * cos + rot * sin


def _rope_qkv_kernel(q_ref, k_ref, v_ref, cos_ref, sin_ref, qo_ref, ko_ref, vo_ref, *, scale):
    cos = cos_ref[0]
    sin = sin_ref[0]
    hd = HEAD_DIM
    q = (_rope(q_ref[0], cos, sin) * scale).astype(qo_ref.dtype)
    k = _rope(k_ref[0], cos, sin).astype(ko_ref.dtype)
    for h in range(q.shape[1] // hd):
        qo_ref[0, h] = q[:, h * hd:(h + 1) * hd]
        ko_ref[0, h] = k[:, h * hd:(h + 1) * hd]
    vo_ref[0] = v_ref[0].astype(vo_ref.dtype)


def rope_qkv(pm, cos, sin, uq, uk, uv, scale, ts=512):
    b, s, _ = pm.shape
    wq = D_BRANCH
    nh = wq // HEAD_DIM
    tab = pl.BlockSpec((1, ts, LANE), lambda bi, i: (bi, i, 0))
    col = lambda u: pl.BlockSpec((1, ts, wq), lambda bi, i: (bi, i, u * LANE // wq))
    heads = pl.BlockSpec((1, nh, ts, HEAD_DIM), lambda bi, i: (bi, 0, i, 0))
    return pl.pallas_call(
        functools.partial(_rope_qkv_kernel, scale=scale),
        grid=(b, s // ts),
        in_specs=[col(uq), col(uk), col(uv), tab, tab],
        out_specs=[heads, heads, pl.BlockSpec((1, ts, wq), lambda bi, i: (bi, i, 0))],
        out_shape=[jax.ShapeDtypeStruct((b, nh, s, HEAD_DIM), BF16)] * 2
                  + [jax.ShapeDtypeStruct((b, s, wq), BF16)],
        compiler_params=_cp(("parallel", "parallel")),
        name="rope_qkv",
    )(pm, pm, pm, cos, sin)


def _lane_fold(x, op):
    out = x[..., 0:LANE]
    for c in range(1, x.shape[-1] // LANE):
        out = op(out, x[..., c * LANE:(c + 1) * LANE])
    return out


def _paired_loop(n, step):
    def body(jj, carry):
        step(2 * jj)
        step(2 * jj + 1)
        return carry
    lax.fori_loop(0, n // 2, body, 0)

    @pl.when(n % 2 == 1)
    def _():
        step(n - 1)


def _diff_attn_kernel(q_ref, k_ref, v_ref, lam_ref, g_ref, o_ref, s_ref, m_ref, l_ref, acc_ref, *, tq, tk, lam_init):
    i = pl.program_id(2)
    nfull = (i * tq) // tk
    row = lax.broadcasted_iota(jnp.int32, (tq, tk), 0)
    col = lax.broadcasted_iota(jnp.int32, (tq, tk), 1)
    vis = (col >> 6) <= (row >> 6)

    m_ref[...] = jnp.full_like(m_ref, NEG)

    def max_step(j, masked=False):
        for mm in range(2):
            kb = k_ref[0, mm, pl.ds(pl.multiple_of(j * tk, tk), tk), :]
            s = _dot(q_ref[0, mm], kb, NT_DIMS)
            if masked:
                s = jnp.where(vis, s, NEG)
            s_ref[j, mm] = s
            m_ref[mm] = jnp.maximum(m_ref[mm], _lane_fold(s, jnp.maximum))

    _paired_loop(nfull, max_step)
    max_step(nfull, True)
    for mm in range(2):
        m_ref[mm] = jnp.broadcast_to(jnp.max(m_ref[mm], axis=1, keepdims=True), (tq, LANE))
    l_ref[...] = jnp.zeros_like(l_ref)
    acc_ref[...] = jnp.zeros_like(acc_ref)

    def acc_step(j):
        vb = v_ref[0, pl.ds(pl.multiple_of(j * tk, tk), tk), :]
        for mm in range(2):
            s = s_ref[j, mm]
            mb = m_ref[mm]
            ps = [jnp.exp2(s[:, c * LANE:(c + 1) * LANE] - mb) for c in range(tk // LANE)]
            l_ref[mm] += functools.reduce(jnp.add, ps)
            acc_ref[mm] += _dot(jnp.concatenate(ps, axis=1).astype(BF16), vb)

    _paired_loop(nfull + 1, acc_step)

    lv = lam_ref[...]
    lam = (jnp.exp(jnp.sum(lv[0:1] * lv[1:2], keepdims=True))
           - jnp.exp(jnp.sum(lv[2:3] * lv[3:4], keepdims=True)) + lam_init)
    l0 = jnp.sum(l_ref[0], axis=1, keepdims=True)
    l1 = jnp.sum(l_ref[1], axis=1, keepdims=True)
    o = acc_ref[0] / l0 - lam * (acc_ref[1] / l1)
    ms = jnp.mean(o * o, axis=-1, keepdims=True)
    o = o * lax.rsqrt(ms + EPS) * g_ref[...] * (1.0 - lam_init)
    o_ref[0] = o.astype(o_ref.dtype)


def diff_attention(pm, cos, sin, lam_vecs, subln_g, layer, tq=512):
    b, s, _ = pm.shape
    tk = tq
    qa, ka, va = rope_qkv(pm, cos, sin, U_AQ, U_AK, U_AV, HEAD_DIM ** -0.5 * LOG2E)
    lam_init = 0.8 - 0.6 * math.exp(-0.3 * layer)
    return pl.pallas_call(
        functools.partial(_diff_attn_kernel, tq=tq, tk=tk, lam_init=lam_init),
        grid=(b, DIFF_HEADS, s // tq),
        in_specs=[pl.BlockSpec((1, 2, tq, HEAD_DIM), lambda bi, h, i: (bi, h, i, 0)),
                  pl.BlockSpec((1, 2, s, HEAD_DIM), lambda bi, h, i: (bi, h, 0, 0)),
                  pl.BlockSpec((1, s, LANE), lambda bi, h, i: (bi, 0, h)),
                  pl.BlockSpec((4, HEAD_DIM), lambda bi, h, i: (0, 0)),
                  pl.BlockSpec((1, LANE), lambda bi, h, i: (0, 0))],
        out_specs=pl.BlockSpec((1, tq, LANE), lambda bi, h, i: (bi, i, h)),
        out_shape=jax.ShapeDtypeStruct((b, s, D_BRANCH), BF16),
        scratch_shapes=[pltpu.VMEM((s // tk, 2, tq, tk), F32),
                        pltpu.VMEM((2, tq, LANE), F32), pltpu.VMEM((2, tq, LANE), F32),
                        pltpu.VMEM((2, tq, LANE), F32)],
        compiler_params=_cp(("parallel", "parallel", "parallel")),
        name="diff_attention",
    )(qa, ka, va, lam_vecs, subln_g.reshape(1, LANE))


def _softplus(z):
    return jnp.maximum(z, 0.0) + jnp.log(1.0 + jnp.exp(-jnp.abs(z)))


def _rwkv_prep_kernel(x_ref, prev_ref, mu_ref, wup_ref, aup_ref, gup_ref, w0_ref, a0_ref, kkg_ref, kag_ref,
                      seg_ref, r_ref, k_ref, v_ref, al_ref, be_ref, lw_ref, g_ref):
    i = pl.program_id(1)
    p = x_ref[0]
    t = p.shape[0]
    row = lax.broadcasted_iota(jnp.int32, p.shape, 0)
    last = prev_ref[0][7:8, :]
    last = jnp.where(i == 0, jnp.zeros_like(last), last)
    prev = jnp.where(row == 0, jnp.broadcast_to(last, p.shape), pltpu.roll(p, 1, axis=0))
    ps = p + (prev - p) * mu_ref[...]
    d = D_BRANCH
    r, k, v = ps[:, 0:d], ps[:, d:2 * d], ps[:, 2 * d:3 * d]
    wd = ps[:, 3 * d:3 * d + 64]
    ad = ps[:, 3 * d + 64:3 * d + 128]
    gd = ps[:, 3 * d + 128:3 * d + 256]
    w_log = -_softplus(-(w0_ref[...] + _dot_hi(jnp.tanh(wd), wup_ref[...]))) - 0.5
    a = jax.nn.sigmoid(a0_ref[...] + _dot_hi(ad, aup_ref[...]))
    g = _dot_hi(jax.nn.sigmoid(gd), gup_ref[...])
    kk = k * kkg_ref[...]
    nrm = jnp.sqrt(_dot_hi(kk * kk, seg_ref[...]))
    kk = kk / jnp.maximum(nrm, 1e-12)
    r_ref[0] = r
    k_ref[0] = k * (1.0 + (a - 1.0) * kag_ref[...])
    v_ref[0] = v
    al_ref[0] = -kk
    be_ref[0] = kk * a
    lw_ref[0] = -jnp.exp(w_log)
    g_ref[0] = g


def _bdot(a, b, dims):
    return lax.dot_general(a, b, ((dims[0], dims[1]), ((0,), (0,))), preferred_element_type=F32)


BNN = ((2,), (1,))
BNT = ((2,), (2,))


def _rwkv_chunk_kernel(r_ref, k_ref, v_ref, al_ref, be_ref, lw_ref, q_ref, en_ref, wl_ref, *, rows, sub):
    hd = HEAD_DIM
    nch = rows // CHUNK
    nsb = rows // sub
    lw = lw_ref[0]
    rowc = lax.broadcasted_iota(jnp.int32, lw.shape, 0) & (CHUNK - 1)
    cum = lw
    for dd in (1, 2, 4, 8, 16, 32):
        cum = cum + jnp.where(rowc >= dd, pltpu.roll(cum, dd, axis=0), 0.0)
    cl = jnp.concatenate(
        [jnp.broadcast_to(cum[(c + 1) * CHUNK - 1:(c + 1) * CHUNK, :], (CHUNK, LANE)) for c in range(nch)], axis=0)
    e_in = jnp.exp(cum)
    e_out = jnp.exp(-cum)
    e_end = jnp.exp(cl - cum)
    r2, k2, v2, al2, be2 = r_ref[0], k_ref[0], v_ref[0], al_ref[0], be_ref[0]
    at2 = al2 * jnp.exp(cum - lw)
    rt2 = r2 * e_in
    bt2 = be2 * e_out
    kt2 = k2 * e_out
    bh2 = be2 * e_end
    kh2 = k2 * e_end
    wl2 = jnp.exp(cl)

    row = lax.broadcasted_iota(jnp.int32, (sub, sub), 0)
    col = lax.broadcasted_iota(jnp.int32, (sub, sub), 1)
    same = (row >> 6) == (col >> 6)
    m_strict = same & (row > col)
    m_incl = same & (row >= col)
    eye = (row == col).astype(F32)
    m_blk8 = (row >> 3) == (col >> 3)
    m_lvls = [((row >> (sh + 1)) == (col >> (sh + 1))) & ((row >> sh) != (col >> sh)) for sh in (3, 4, 5)]
    wl_ref[0] = jnp.concatenate([wl2[c * CHUNK:c * CHUNK + 1, :] for c in range(nch)], axis=0)

    def stack(a):
        return jnp.stack([a[sb * sub:(sb + 1) * sub, hh * hd:(hh + 1) * hd]
                          for hh in range(2) for sb in range(nsb)])

    at, rt, vb = stack(at2), stack(rt2), stack(v2).astype(BF16)
    lhs = jnp.concatenate([at, rt], axis=1).astype(BF16)
    rhs = jnp.concatenate([stack(bt2), stack(kt2)], axis=1).astype(BF16)
    gm = _bdot(lhs, rhs, BNT)
    n_ab = jnp.where(m_strict[None], gm[:, :sub, :sub], 0.0)
    a_ak = jnp.where(m_strict[None], gm[:, :sub, sub:], 0.0).astype(BF16)
    a_rb = jnp.where(m_incl[None], gm[:, sub:, :sub], 0.0).astype(BF16)
    a_rk = jnp.where(m_incl[None], gm[:, sub:, sub:], 0.0).astype(BF16)
    n8 = jnp.where(m_blk8[None], n_ab, 0.0)
    n8b = n8.astype(BF16)
    n_2 = _bdot(n8b, n8b, BNN)
    n2b = n_2.astype(BF16)
    n_3 = _bdot(n2b, n8b, BNN)
    n_4 = _bdot(n2b, n2b, BNN)
    tinv = eye[None] + n8 + n_2 + n_3
    tinv = tinv + _bdot(tinv.astype(BF16), n_4.astype(BF16), BNN)
    for m_lvl in m_lvls:
        nl = jnp.where(m_lvl[None], n_ab, 0.0).astype(BF16)
        tb = tinv.astype(BF16)
        tinv = tinv + _bdot(_bdot(tb, nl, BNN).astype(BF16), tb, BNN)
    tb = tinv.astype(BF16)
    akv = _bdot(a_ak, vb, BNN)
    pmat = _bdot(tb, jnp.concatenate([at, akv], axis=2).astype(BF16), BNN)
    qmat = _bdot(a_rb, pmat.astype(BF16), BNN) + jnp.concatenate([rt, _bdot(a_rk, vb, BNN)], axis=2)
    pb = pmat.astype(BF16)
    bhb = stack(bh2).astype(BF16)
    khb = stack(kh2).astype(BF16)
    for hh in range(2):
        for sb in range(nsb):
            bi = hh * nsb + sb
            q_ref[0, hh, sb * sub:(sb + 1) * sub, :] = qmat[bi]
            for cc in range(sub // CHUNK):
                cs = slice(cc * CHUNK, (cc + 1) * CHUNK)
                mn = _dot(pb[bi, cs], bhb[bi, cs], TN_DIMS)
                n_c = mn[hd:] + _dot(vb[bi, cs], khb[bi, cs], TN_DIMS)
                en_ref[0, hh, sb * (sub // CHUNK) + cc] = jnp.concatenate([mn[:hd], n_c], axis=0)


def _rwkv_state_kernel(q_ref, en_ref, wl_ref, r_ref, k_ref, v_ref, g_ref, rk_ref, gng_ref, gnb_ref,
                       o_ref, st_ref, *, rows):
    t = pl.program_id(2)
    hd = HEAD_DIM
    nch = rows // CHUNK

    @pl.when(t == 0)
    def _():
        st_ref[...] = jnp.zeros_like(st_ref)

    wl = wl_ref[0]
    r2, k2, v2, g2 = r_ref[0], k_ref[0], v_ref[0], g_ref[0]
    sts = [st_ref[0], st_ref[1]]
    ys = [[], []]
    for c in range(nch):
        for hh in range(2):
            qc = q_ref[0, hh, c * CHUNK:(c + 1) * CHUNK, :]
            en = en_ref[0, hh, c]
            stb = sts[hh].astype(BF16)
            ys[hh].append(_dot(qc[:, :hd].astype(BF16), stb, NT_DIMS) + qc[:, hd:])
            sts[hh] = (sts[hh] * wl[c:c + 1, hh * hd:(hh + 1) * hd]
                       + _dot(stb, en[:hd].astype(BF16)) + en[hd:])
    outs = []
    for hh in range(2):
        sl = slice(hh * hd, (hh + 1) * hd)
        st_ref[hh] = sts[hh]
        v, r, kp = v2[:, sl], r2[:, sl], k2[:, sl]
        y = jnp.concatenate(ys[hh], axis=0)
        mu = jnp.mean(y, axis=-1, keepdims=True)
        var = jnp.mean(jnp.square(y - mu), axis=-1, keepdims=True)
        yn = (y - mu) * lax.rsqrt(var + RWKV_GN_EPS) * gng_ref[:, sl] + gnb_ref[:, sl]
        bonus = jnp.sum(r * kp * rk_ref[0, hh:hh + 1, :], axis=-1, keepdims=True) * v
        outs.append((yn + bonus) * g2[:, sl])
    o_ref[0] = jnp.concatenate(outs, axis=1).astype(o_ref.dtype)


def rwkv7(pm, mu, w_up, a_up, g_up, w0, a0, kk_gain, ka_gain, r_k, gn_g, gn_b, ts=256, rows=512, sub=256):
    b, s, _ = pm.shape
    d = D_BRANCH
    seg = np.kron(np.eye(d // HEAD_DIM, dtype=np.float32), np.ones((HEAD_DIM, HEAD_DIM), np.float32))
    row1 = lambda a: a.reshape(1, -1)
    cst = lambda shape: pl.BlockSpec(shape, lambda bi, i: (0,) * len(shape))
    blk = pl.BlockSpec((1, ts, d), lambda bi, i: (bi, i, 0))
    r, k, v, al, be, lw, g = pl.pallas_call(
        _rwkv_prep_kernel,
        grid=(b, s // ts),
        in_specs=[pl.BlockSpec((1, ts, B_WIDTH), lambda bi, i: (bi, i, 0)),
                  pl.BlockSpec((1, 8, B_WIDTH), lambda bi, i: (bi, jnp.maximum(i * (ts // 8) - 1, 0), 0)),
                  cst((1, B_WIDTH)), cst((64, d)), cst((64, d)), cst((128, d)),
                  cst((1, d)), cst((1, d)), cst((1, d)), cst((1, d)), cst((d, d))],
        out_specs=[blk] * 7,
        out_shape=[jax.ShapeDtypeStruct((b, s, d), F32)] * 7,
        compiler_params=_cp(("parallel", "arbitrary")),
        name="rwkv_prep",
    )(pm, pm, row1(mu), w_up, a_up, g_up, row1(w0), row1(a0), row1(kk_gain), row1(ka_gain), jnp.asarray(seg))

    hp = pl.BlockSpec((1, rows, LANE), lambda bi, h, t: (bi, t, h))
    par = pl.BlockSpec((1, LANE), lambda bi, h, t: (0, h))
    nh = d // HEAD_DIM
    nch = rows // CHUNK
    q_spec = pl.BlockSpec((1, 2, rows, LANE), lambda bi, h, t: (bi, h, t, 0))
    en_spec = pl.BlockSpec((1, 2, nch, LANE, HEAD_DIM), lambda bi, h, t: (bi, h, t, 0, 0))
    wl_spec = pl.BlockSpec((1, nch, LANE), lambda bi, h, t: (bi, t, h))
    grid = (b, d // LANE, s // rows)
    qm, en, wl = pl.pallas_call(
        functools.partial(_rwkv_chunk_kernel, rows=rows, sub=sub),
        grid=grid,
        in_specs=[hp] * 6,
        out_specs=[q_spec, en_spec, wl_spec],
        out_shape=[jax.ShapeDtypeStruct((b, nh, s, LANE), F32),
                   jax.ShapeDtypeStruct((b, nh, s // CHUNK, LANE, HEAD_DIM), F32),
                   jax.ShapeDtypeStruct((b, s // CHUNK, d), F32)],
        compiler_params=_cp(("parallel", "parallel", "parallel")),
        name="rwkv_chunk",
    )(r, k, v, al, be, lw)
    return pl.pallas_call(
        functools.partial(_rwkv_state_kernel, rows=rows),
        grid=grid,
        in_specs=[q_spec, en_spec, wl_spec, hp, hp, hp, hp,
                  pl.BlockSpec((1, 2, HEAD_DIM), lambda bi, h, t: (h, 0, 0)), par, par],
        out_specs=hp,
        out_shape=jax.ShapeDtypeStruct((b, s, d), BF16),
        scratch_shapes=[pltpu.VMEM((2, HEAD_DIM, HEAD_DIM), F32)],
        compiler_params=_cp(("parallel", "parallel", "arbitrary")),
        name="rwkv_state",
    )(qm, en, wl, r, k, v, g, r_k.reshape(-1, 2, HEAD_DIM), row1(gn_g), row1(gn_b))


def _hgrn_kernel(q_ref, f_ref, i_ref, g_ref, lb_ref, ng_ref, o_ref, st_ref, *, rows):
    t = pl.program_id(2)
    sub = 16

    @pl.when(t == 0)
    def _():
        st_ref[...] = jnp.zeros_like(st_ref)

    q = jax.nn.silu(q_ref[0])
    lb = lb_ref[...]
    fg = lb + (1.0 - lb) * jax.nn.sigmoid(f_ref[0])
    lf = jnp.log(fg)
    kf = 1.0 - fg
    iv = i_ref[0]
    rowc = lax.broadcasted_iota(jnp.int32, lf.shape, 0) & (CHUNK - 1)
    bcum = lf
    for dd in (1, 2, 4, 8, 16, 32):
        bcum = bcum + jnp.where(rowc >= dd, pltpu.roll(bcum, dd, axis=0), 0.0)
    ivb = iv.astype(BF16)
    rsub = lax.broadcasted_iota(jnp.int32, (sub, LANE), 0)
    st = st_ref[...]
    outs = []
    for c in range(rows // CHUNK):
        c0 = c * CHUNK
        bc = bcum[c0:c0 + CHUNK]
        qc = q[c0:c0 + CHUNK]
        kc = kf[c0:c0 + CHUNK]
        ic = iv[c0:c0 + CHUNK]
        icb = ivb[c0:c0 + CHUNK]
        o_inter = _dot((qc * jnp.exp(bc)).astype(BF16), st.astype(BF16), NT_DIMS)
        blocks = []
        for ib in range(CHUNK // sub):
            r0 = ib * sub
            bi = bc[r0:r0 + sub]
            qi = qc[r0:r0 + sub]
            ki = kc[r0:r0 + sub]
            ii = ic[r0:r0 + sub]
            acc = jnp.zeros((sub, LANE), F32)
            for s_ in range(sub):
                e = jnp.where(rsub >= s_, jnp.exp(jnp.minimum(bi - bi[s_:s_ + 1], 0.0)), 0.0)
                a = jnp.sum(qi * (ki[s_:s_ + 1] * e), axis=-1, keepdims=True)
                acc = acc + a * ii[s_:s_ + 1]
            if ib > 0:
                ref_row = bc[r0 - 1:r0]
                qt = (qi * jnp.exp(bi - ref_row)).astype(BF16)
                kt = (kc[0:r0] * jnp.exp(ref_row - bc[0:r0])).astype(BF16)
                a_off = _dot(qt, kt, NT_DIMS).astype(BF16)
                acc = acc + _dot(a_off, icb[0:r0])
            blocks.append(acc)
        outs.append(o_inter + jnp.concatenate(blocks, axis=0))
        bl = bc[CHUNK - 1:CHUNK]
        kdec = (kc * jnp.exp(bl - bc)).astype(BF16)
        st = st * jnp.exp(bl) + _dot(icb, kdec, TN_DIMS)
    st_ref[...] = st
    o = jnp.concatenate(outs, axis=0)
    ms = jnp.mean(o * o, axis=-1, keepdims=True)
    o = o * lax.rsqrt(ms + EPS) * ng_ref[...]
    o_ref[0] = (o * jax.nn.silu(g_ref[0])).astype(o_ref.dtype)


def hgrn2(pm, lb, norm_g, rows=256):
    b, s, _ = pm.shape
    heads = D_BRANCH // LANE
    spec = lambda u: pl.BlockSpec((1, rows, LANE), lambda bi, h, t: (bi, t, u + h))
    par = pl.BlockSpec((1, LANE), lambda bi, h, t: (0, h))
    return pl.pallas_call(
        functools.partial(_hgrn_kernel, rows=rows),
        grid=(b, heads, s // rows),
        in_specs=[spec(U_CQ), spec(U_CF), spec(U_CI), spec(U_CG), par, par],
        out_specs=spec(0),
        out_shape=jax.ShapeDtypeStruct((b, s, D_BRANCH), BF16),
        scratch_shapes=[pltpu.VMEM((LANE, LANE), F32)],
        compiler_params=_cp(("parallel", "parallel", "arbitrary")),
        name="hgrn2",
    )(pm, pm, pm, pm, lb.reshape(1, -1), norm_g.reshape(1, -1))


def _dsa_prep_kernel(q_ref, iq_ref, kv_ref, ikw_ref, cos_ref, sin_ref,
                     qo_ref, iqo_ref, ko_ref, vo_ref, iko_ref, wo_ref, *, scale, wscale):
    cos = cos_ref[0]
    sin = sin_ref[0]
    hd = HEAD_DIM
    q = (_rope(q_ref[0], cos, sin) * scale).astype(BF16)
    iq = _rope(iq_ref[0], cos, sin).astype(BF16)
    for h in range(DSA_HEADS):
        qo_ref[0, h] = q[:, h * hd:(h + 1) * hd]
        iqo_ref[0, h] = iq[:, h * hd:(h + 1) * hd]
    kv = kv_ref[0]
    ko_ref[0] = _rope(kv, cos, sin)[:, :hd].astype(BF16)
    vo_ref[0] = kv[:, hd:].astype(BF16)
    ikw = ikw_ref[0]
    iko_ref[0] = _rope(ikw, cos, sin)[:, :hd].astype(BF16)
    wo_ref[0] = ikw[:, hd:hd + IDX_HEADS] * wscale


def _to_key(x):
    x = jnp.where(x == 0.0, 0.0, x)
    bits = pltpu.bitcast(x, jnp.int32)
    return jnp.where(bits < 0, bits ^ jnp.int32(0x7FFFFFFF), bits)


def _dsa_index_kernel(iq_ref, ik_ref, wt_ref, mask_ref, key_ref, *, tq, tk, top_k):
    i = pl.program_id(1)
    nk = key_ref.shape[0]
    nvis = (i * tq + tq - 1) // tk + 1
    krow = lax.broadcasted_iota(jnp.int32, (tk, tq), 0)
    qcol = i * tq + lax.broadcasted_iota(jnp.int32, (tk, tq), 1)
    wt = wt_ref[0]

    def score_block(jb, carry):
        ikb = ik_ref[0, pl.ds(pl.multiple_of(jb * tk, tk), tk), :]
        s_all = _dot(ikb, iq_ref[0].reshape(IDX_HEADS * tq, HEAD_DIM), NT_DIMS)
        acc = jnp.zeros((tk, tq), F32)
        for h in range(IDX_HEADS):
            acc = acc + jnp.maximum(s_all[:, h * tq:(h + 1) * tq], 0.0) * wt[h:h + 1, :]
        vis = ((jb * tk + krow) >> 6) <= (qcol >> 6)
        key_ref[jb] = jnp.where(vis, _to_key(acc), INT_MIN)
        return carry

    lax.fori_loop(0, nvis, score_block, 0)

    def count(pred_fn):
        def blk(jb, cnt):
            ind = jnp.where(pred_fn(key_ref[jb], jb), 1, 0)
            return cnt + jnp.sum(ind.reshape(tk // 8, 8, tq), axis=0)
        cnt = lax.fori_loop(0, nvis, blk, jnp.zeros((8, tq), jnp.int32))
        return jnp.sum(cnt, axis=0, keepdims=True)

    def bit_step(bi, prefix):
        cand = prefix | lax.shift_left(jnp.int32(1), 31 - bi)
        cand_s = cand ^ jnp.int32(INT_MIN)
        c = count(lambda kb, jb: kb >= cand_s)
        return jnp.where(c >= top_k, cand, prefix)

    prefix = lax.fori_loop(0, 32, bit_step, jnp.zeros((1, tq), jnp.int32))
    tau = prefix ^ jnp.int32(INT_MIN)
    n_gt = count(lambda kb, jb: kb > tau)
    n_eq = count(lambda kb, jb: kb == tau)
    need = top_k - n_gt
    tied = (n_eq > need) & (tau != INT_MIN)
    n_cols = nk * tk

    idx_bits = int(n_cols).bit_length()

    def tie_break():
        def idx_step(bi, pre):
            cand = pre | lax.shift_left(jnp.int32(1), idx_bits - 1 - bi)
            c = count(lambda kb, jb: (kb == tau) & ((jb * tk + krow) < cand))
            return jnp.where(c < need, cand, pre)
        cut = lax.fori_loop(0, idx_bits, idx_step, jnp.zeros((1, tq), jnp.int32))
        return jnp.where(tied, cut, n_cols)

    cut = lax.cond(jnp.max(tied.astype(jnp.int32)) > 0, tie_break,
                   lambda: jnp.full((1, tq), n_cols, jnp.int32))

    def write_block(jb, carry):
        kb = key_ref[jb]
        sel = (kb > tau) | ((kb == tau) & ((jb * tk + krow) <= cut))
        sel = sel & (kb != INT_MIN)
        mask_ref[0, jb] = jnp.where(sel, 1.0, 0.0).T.astype(mask_ref.dtype)
        return carry

    def zero_block(jb, carry):
        mask_ref[0, jb] = jnp.zeros((tq, tk), mask_ref.dtype)
        return carry

    lax.fori_loop(0, nvis, write_block, 0)
    lax.fori_loop(nvis, nk, zero_block, 0)


def _dsa_attn_kernel(q_ref, k_ref, v_ref, mask_ref, o_ref, s_ref, m_ref, l_ref, acc_ref, *, tq, tk):
    i = pl.program_id(1)
    nh = DSA_HEADS
    hd = HEAD_DIM
    nvis = (i * tq + tq - 1) // tk + 1
    nchunk = tk // LANE

    m_ref[...] = jnp.full_like(m_ref, NEG)

    def max_step(j):
        kb = k_ref[0, pl.ds(pl.multiple_of(j * tk, tk), tk), :]
        s = _dot(q_ref[0].reshape(nh * tq, hd), kb, NT_DIMS).reshape(nh, tq, tk)
        s = jnp.where((mask_ref[0, j] > 0)[None], s, NEG)
        s_ref[j] = s
        m_ref[...] = jnp.maximum(m_ref[...], _lane_fold(s, jnp.maximum))

    _paired_loop(nvis, max_step)
    m_ref[...] = jnp.broadcast_to(jnp.max(m_ref[...], axis=2, keepdims=True), m_ref.shape)
    l_ref[...] = jnp.zeros_like(l_ref)
    acc_ref[...] = jnp.zeros_like(acc_ref)

    def acc_step(j):
        s = s_ref[j]
        mb = m_ref[...]
        ps = [jnp.exp2(s[:, :, c * LANE:(c + 1) * LANE] - mb) for c in range(nchunk)]
        l_ref[...] += functools.reduce(jnp.add, ps)
        p = jnp.concatenate(ps, axis=2).astype(BF16).reshape(nh * tq, tk)
        vb = v_ref[0, pl.ds(pl.multiple_of(j * tk, tk), tk), :]
        acc_ref[...] += _dot(p, vb).reshape(nh, tq, hd)

    _paired_loop(nvis, acc_step)
    o = acc_ref[...] / jnp.sum(l_ref[...], axis=2, keepdims=True)
    o_ref[0] = jnp.concatenate([o[h] for h in range(nh)], axis=1).astype(o_ref.dtype)


def dsa_attention(pm, cos, sin, ts=512, tq=128, tk=512):
    b, s, _ = pm.shape
    top_k = min(TOPK_MAX, s // 4)
    hd = HEAD_DIM
    nk = s // tk
    wide = lambda u: pl.BlockSpec((1, ts, D_BRANCH), lambda bi, i: (bi, i, u * LANE // D_BRANCH))
    one = lambda u: pl.BlockSpec((1, ts, LANE), lambda bi, i: (bi, i, u))
    heads_out = pl.BlockSpec((1, DSA_HEADS, ts, hd), lambda bi, i: (bi, 0, i, 0))
    narrow = lambda w: pl.BlockSpec((1, ts, w), lambda bi, i: (bi, i, 0))
    q, iq, k, v, ik, w = pl.pallas_call(
        functools.partial(_dsa_prep_kernel, scale=hd ** -0.5 * LOG2E, wscale=(IDX_HEADS ** -0.5) * (hd ** -0.5)),
        grid=(b, s // ts),
        in_specs=[wide(U_DQ), wide(U_DIQ), one(U_DKV), one(U_DIKW), one(0), one(0)],
        out_specs=[heads_out, heads_out, narrow(hd), narrow(hd), narrow(hd), narrow(IDX_HEADS)],
        out_shape=[jax.ShapeDtypeStruct((b, DSA_HEADS, s, hd), BF16)] * 2
                  + [jax.ShapeDtypeStruct((b, s, hd), BF16)] * 3
                  + [jax.ShapeDtypeStruct((b, s, IDX_HEADS), F32)],
        compiler_params=_cp(("parallel", "parallel")),
        name="dsa_prep",
    )(pm, pm, pm, pm, cos, sin)

    mask_spec = pl.BlockSpec((1, nk, tq, tk), lambda bi, i: (bi, 0, i, 0))
    whole = pl.BlockSpec((1, s, hd), lambda bi, i: (bi, 0, 0))
    mask = pl.pallas_call(
        functools.partial(_dsa_index_kernel, tq=tq, tk=tk, top_k=top_k),
        grid=(b, s // tq),
        in_specs=[pl.BlockSpec((1, IDX_HEADS, tq, hd), lambda bi, i: (bi, 0, i, 0)),
                  whole,
                  pl.BlockSpec((1, IDX_HEADS, tq), lambda bi, i: (bi, 0, i))],
        out_specs=mask_spec,
        out_shape=jax.ShapeDtypeStruct((b, nk, s, tk), BF16),
        scratch_shapes=[pltpu.VMEM((nk, tk, tq), jnp.int32)],
        compiler_params=_cp(("parallel", "parallel")),
        name="dsa_index",
    )(iq, ik, jnp.swapaxes(w, 1, 2))

    return pl.pallas_call(
        functools.partial(_dsa_attn_kernel, tq=tq, tk=tk),
        grid=(b, s // tq),
        in_specs=[pl.BlockSpec((1, DSA_HEADS, tq, hd), lambda bi, i: (bi, 0, i, 0)),
                  whole, whole, mask_spec],
        out_specs=pl.BlockSpec((1, tq, D_BRANCH), lambda bi, i: (bi, i, 0)),
        out_shape=jax.ShapeDtypeStruct((b, s, D_BRANCH), BF16),
        scratch_shapes=[pltpu.VMEM((nk, DSA_HEADS, tq, tk), F32),
                        pltpu.VMEM((DSA_HEADS, tq, LANE), F32), pltpu.VMEM((DSA_HEADS, tq, LANE), F32),
                        pltpu.VMEM((DSA_HEADS, tq, hd), F32)],
        compiler_params=_cp(("parallel", "parallel")),
        name="dsa_attention",
    )(q, k, v, mask)


def _split_w_in(w_in):
    c = lambda a, b_: w_in[:, a:b_]
    pad = jnp.zeros((w_in.shape[0], LANE - 64 - IDX_HEADS), w_in.dtype)
    main = jnp.concatenate([
        c(1536, 3328),
        c(5888, 6016),
        c(6528, 6600), pad,
        c(0, 1536),
        c(3328, 5376),
        c(5376, 5888), c(6016, 6528),
    ], axis=1).astype(BF16)
    gate = c(6600, 6600 + N_BRANCH * D_MODEL).astype(BF16)
    return main, gate


def _layer(x2, b, s, cos, sin, layer, norm_g, w_in, diff_lambda, diff_subln_g, rwkv_mu, rwkv_w_up, rwkv_a_up,
           rwkv_g_up, rwkv_w0, rwkv_a0, rwkv_k_k, rwkv_k_a, rwkv_r_k, rwkv_gn_g, rwkv_gn_b, hgrn_lb, hgrn_norm_g,
           w_branch, w_out, mlp_w1, mlp_w2):
    w_main, w_gate = _split_w_in(w_in)
    h = rmsnorm_bf16(x2, norm_g[0])
    pm = matmul(h, w_main, None, F32, name="proj_main").reshape(b, s, N_MAIN)
    gate = matmul(h, w_gate, "sigmoid", BF16, name="proj_gate")
    cos3 = cos.reshape(b, s, LANE)
    sin3 = sin.reshape(b, s, LANE)
    y_a = diff_attention(pm, cos3, sin3, diff_lambda, diff_subln_g, layer)
    y_b = rwkv7(pm, rwkv_mu, rwkv_w_up, rwkv_a_up, rwkv_g_up, rwkv_w0, rwkv_a0, rwkv_k_k, rwkv_k_a,
                rwkv_r_k, rwkv_gn_g, rwkv_gn_b)
    y_c = hgrn2(pm, hgrn_lb, hgrn_norm_g)
    y_d = dsa_attention(pm, cos3, sin3)
    m = b * s
    ys = [y.reshape(m, D_BRANCH) for y in (y_a, y_b, y_c, y_d)]
    merged = gated_merge(ys, w_branch.astype(BF16), gate)
    x2 = matmul_norm_residual(merged, w_out.astype(BF16), x2, norm_g[1], name="out_proj")
    h2 = rmsnorm_bf16(x2, norm_g[2])
    ff = matmul(h2, mlp_w1.astype(BF16), "relu2", BF16, name="mlp_up")
    return matmul_norm_residual(ff, mlp_w2.astype(BF16), x2, norm_g[3], name="mlp_down")


def kernel(x, positions, norm_g, w_in, diff_lambda, diff_subln_g, rwkv_mu, rwkv_w_up, rwkv_a_up, rwkv_g_up, rwkv_w0, rwkv_a0, rwkv_k_k, rwkv_k_a, rwkv_r_k, rwkv_gn_g, rwkv_gn_b, hgrn_lb_logits, hgrn_norm_g, w_branch, w_out, mlp_w1, mlp_w2):
    b, s, d = x.shape
    depth = w_in.shape[0]
    lb = jax.nn.softmax(hgrn_lb_logits.astype(F32), axis=0)
    lb = jnp.cumsum(lb, axis=0) - lb[0]
    cos, sin = rope_tables(positions)
    x2 = x.reshape(b * s, d)
    for l in range(depth):
        x2 = _layer(x2, b, s, cos, sin, l, norm_g[l], w_in[l], diff_lambda[l], diff_subln_g[l], rwkv_mu[l],
                    rwkv_w_up[l], rwkv_a_up[l], rwkv_g_up[l], rwkv_w0[l], rwkv_a0[l], rwkv_k_k[l], rwkv_k_a[l],
                    rwkv_r_k[l], rwkv_gn_g[l], rwkv_gn_b[l], lb[l], hgrn_norm_g[l], w_branch[l], w_out[l],
                    mlp_w1[l], mlp_w2[l])
    return x2.reshape(b, s, d)
```

```python
import functools
import math

import numpy as np
import jax
import jax.numpy as jnp
from jax import lax
from jax.experimental import pallas as pl
from jax.experimental.pallas import tpu as pltpu

F32 = jnp.float32
BF16 = jnp.bfloat16

D_MODEL = 2048
D_BRANCH = 512
D_FF = 8192
N_BRANCH = 4
CHUNK = 64
EPS = 1e-6
ROPE_THETA = 10000.0
HEAD_DIM = 64
DIFF_HEADS = 4
RWKV_GN_EPS = 64e-5
DSA_HEADS = 8
IDX_HEADS = 8
TOPK_MAX = 256

LANE = 128
VMEM_LIMIT = 56 * 1024 * 1024

U_B, U_DKV, U_DIKW = 0, 14, 15
U_AQ, U_AK, U_AV = 16, 20, 24
U_CQ, U_CF, U_CI, U_CG = 28, 32, 36, 40
U_DQ, U_DIQ = 44, 48
N_MAIN = 52 * LANE
B_WIDTH = 1792

NEG = -1e30
INT_MIN = -2147483648
LOG2E = 1.4426950408889634

NT_DIMS = (((1,), (1,)), ((), ()))
TN_DIMS = (((0,), (0,)), ((), ()))


def _cp(sem, vmem=VMEM_LIMIT):
    return pltpu.CompilerParams(dimension_semantics=sem, vmem_limit_bytes=vmem)


def _dot(a, b, dims=None):
    if dims is None:
        return jnp.dot(a, b, preferred_element_type=F32)
    return lax.dot_general(a, b, dims, preferred_element_type=F32)


def _dot_hi(a, b, dims=(((1,), (0,)), ((), ()))):
    return lax.dot_general(a, b, dims, preferred_element_type=F32,
                           precision=lax.Precision.HIGHEST)


def _rmsnorm_kernel(x_ref, g_ref, o_ref):
    x = x_ref[...]
    ms = jnp.mean(x * x, axis=-1, keepdims=True)
    o_ref[...] = (x * lax.rsqrt(ms + EPS) * g_ref[...]).astype(o_ref.dtype)


def rmsnorm_bf16(x, g, tm=512):
    m, d = x.shape
    return pl.pallas_call(
        _rmsnorm_kernel,
        grid=(m // tm,),
        in_specs=[pl.BlockSpec((tm, d), lambda i: (i, 0)),
                  pl.BlockSpec((1, d), lambda i: (0, 0))],
        out_specs=pl.BlockSpec((tm, d), lambda i: (i, 0)),
        out_shape=jax.ShapeDtypeStruct((m, d), BF16),
        compiler_params=_cp(("parallel",)),
        name="rmsnorm",
    )(x, g.reshape(1, d))


def _mm_kernel(x_ref, w_ref, o_ref, *, act):
    acc = _dot(x_ref[...], w_ref[...].astype(BF16))
    if act == "sigmoid":
        acc = 0.5 * jnp.tanh(0.5 * acc) + 0.5
    elif act == "relu2":
        acc = jnp.square(jnp.maximum(acc, 0.0))
    o_ref[...] = acc.astype(o_ref.dtype)


def matmul(x, w, layer, act, out_dtype, tm=2048, tn=512, name="mm"):
    m, k = x.shape
    n = w.shape[2]
    tm = min(tm, m)
    return pl.pallas_call(
        functools.partial(_mm_kernel, act=act),
        grid=(m // tm, n // tn),
        in_specs=[pl.BlockSpec((tm, k), lambda i, j: (i, 0)),
                  pl.BlockSpec((None, k, tn), lambda i, j: (layer, 0, j))],
        out_specs=pl.BlockSpec((tm, tn), lambda i, j: (i, j)),
        out_shape=jax.ShapeDtypeStruct((m, n), out_dtype),
        compiler_params=_cp(("parallel", "parallel")),
        name=name,
    )(x, w)


def _norm_residual(y, x_ref, g_ref, o_ref, h_ref):
    ms = jnp.mean(y * y, axis=-1, keepdims=True)
    o = x_ref[...] + y * lax.rsqrt(ms + EPS) * g_ref[0:1, :]
    o_ref[...] = o
    if h_ref is not None:
        ms = jnp.mean(o * o, axis=-1, keepdims=True)
        h_ref[...] = (o * lax.rsqrt(ms + EPS) * g_ref[1:2, :]).astype(h_ref.dtype)


def _mm_norm_res_kernel(a_ref, w_ref, x_ref, g_ref, o_ref, *rest):
    h_ref, acc_ref = (rest[0], rest[1]) if len(rest) == 2 else (None, rest[0])
    kk = pl.program_id(1)

    @pl.when(kk == 0)
    def _():
        acc_ref[...] = jnp.zeros_like(acc_ref)

    acc_ref[...] += _dot(a_ref[...], w_ref[...])

    @pl.when(kk == pl.num_programs(1) - 1)
    def _():
        _norm_residual(acc_ref[...], x_ref, g_ref, o_ref, h_ref)


def _mm_norm_res_fullk_kernel(a_ref, w_ref, x_ref, g_ref, o_ref, *rest):
    _norm_residual(_dot(a_ref[...], w_ref[...]), x_ref, g_ref, o_ref, rest[0] if rest else None)


def matmul_norm_residual(a, w, layer, x, g, g_next=None, tm=512, tk=2048, name="mm_norm_res"):
    m, k = a.shape
    n = w.shape[2]
    gs = jnp.stack([g, g if g_next is None else g_next])
    out_shape = [jax.ShapeDtypeStruct((m, n), F32)]
    if g_next is not None:
        out_shape.append(jax.ShapeDtypeStruct((m, n), BF16))
    if k <= tk:
        row = lambda i: (i, 0)
        outs = pl.pallas_call(
            _mm_norm_res_fullk_kernel,
            grid=(m // tm,),
            in_specs=[pl.BlockSpec((tm, k), row), pl.BlockSpec((None, k, n), lambda i: (layer, 0, 0)),
                      pl.BlockSpec((tm, n), row), pl.BlockSpec((2, n), lambda i: (0, 0))],
            out_specs=[pl.BlockSpec((tm, n), row)] * len(out_shape),
            out_shape=out_shape,
            compiler_params=_cp(("parallel",)),
            name=name,
        )(a, w, x, gs)
    else:
        row = lambda i, kk: (i, 0)
        outs = pl.pallas_call(
            _mm_norm_res_kernel,
            grid=(m // tm, k // tk),
            in_specs=[pl.BlockSpec((tm, tk), lambda i, kk: (i, kk)),
                      pl.BlockSpec((None, tk, n), lambda i, kk: (layer, kk, 0)),
                      pl.BlockSpec((tm, n), row),
                      pl.BlockSpec((2, n), lambda i, kk: (0, 0))],
            out_specs=[pl.BlockSpec((tm, n), row)] * len(out_shape),
            out_shape=out_shape,
            scratch_shapes=[pltpu.VMEM((tm, n), F32)],
            compiler_params=_cp(("parallel", "arbitrary")),
            name=name,
        )(a, w, x, gs)
    return outs if g_next is not None else outs[0]


def _merge_kernel(ya, yb, yc, yd, wb, ga, gb, gc, gd, o_ref):
    acc = None
    for n, (y, g) in enumerate(((ya, ga), (yb, gb), (yc, gc), (yd, gd))):
        t = g[...].astype(F32) * _dot(y[...], wb[n].astype(BF16))
        acc = t if acc is None else acc + t
    o_ref[...] = acc.astype(o_ref.dtype)


def gated_merge(ys, wb, layer, gate, tm=512, tn=1024):
    m = ys[0].shape[0]
    nj = D_MODEL // tn
    y_spec = pl.BlockSpec((tm, D_BRANCH), lambda j, i: (i, 0))
    g_specs = [pl.BlockSpec((tm, tn), functools.partial(lambda j, i, n: (i, n * nj + j), n=n))
               for n in range(N_BRANCH)]
    return pl.pallas_call(
        _merge_kernel,
        grid=(nj, m // tm),
        in_specs=[y_spec] * 4 + [pl.BlockSpec((None, N_BRANCH, D_BRANCH, tn), lambda j, i: (layer, 0, 0, j))]
                 + g_specs,
        out_specs=pl.BlockSpec((tm, tn), lambda j, i: (i, j)),
        out_shape=jax.ShapeDtypeStruct((m, D_MODEL), BF16),
        compiler_params=_cp(("parallel", "parallel")),
        name="gated_merge",
    )(*ys, wb, gate, gate, gate, gate)


def _rope_table_kernel(pos_ref, inv_ref, sgn_ref, cos_ref, sin_ref):
    ang = pos_ref[...].astype(F32) * inv_ref[...]
    cos_ref[...] = jnp.cos(ang)
    sin_ref[...] = jnp.sin(ang) * sgn_ref[...]


def rope_tables(positions, tm=512):
    m = positions.size
    half = HEAD_DIM // 2
    inv = ROPE_THETA ** (-np.arange(0, HEAD_DIM, 2, dtype=np.float32) / HEAD_DIM)
    inv = np.tile(inv.astype(np.float32), 4).reshape(1, LANE)
    sgn = np.tile(np.concatenate([-np.ones(half, np.float32), np.ones(half, np.float32)]), 2).reshape(1, LANE)
    spec = pl.BlockSpec((tm, LANE), lambda i: (i, 0))
    cst = pl.BlockSpec((1, LANE), lambda i: (0, 0))
    return pl.pallas_call(
        _rope_table_kernel,
        grid=(m // tm,),
        in_specs=[pl.BlockSpec((tm, 1), lambda i: (i, 0)), cst, cst],
        out_specs=[spec, spec],
        out_shape=[jax.ShapeDtypeStruct((m, LANE), F32)] * 2,
        compiler_params=_cp(("parallel",)),
        name="rope_tables",
    )(positions.reshape(m, 1), jnp.asarray(inv), jnp.asarray(sgn))


def _rope(x, cos, sin):
    w = x.shape[1]
    n = w // LANE
    if n > 1:
        cos = jnp.concatenate([cos] * n, axis=1)
        sin = jnp.concatenate([sin] * n, axis=1)
    lane = lax.broadcasted_iota(jnp.int32, x.shape, 1)
    up = pltpu.roll(x, w - HEAD_DIM // 2, axis=1)
    dn = pltpu.roll(x, HEAD_DIM // 2, axis=1)
    rot = jnp.where((lane & (HEAD_DIM // 2)) == 0, up, dn)
    return x * cos + rot * sin


def _rope_qkv_kernel(q_ref, k_ref, v_ref, cos_ref, sin_ref, qo_ref, ko_ref, vo_ref, *, scale):
    cos = cos_ref[0]
    sin = sin_ref[0]
    hd = HEAD_DIM
    q = (_rope(q_ref[0], cos, sin) * scale).astype(qo_ref.dtype)
    k = _rope(k_ref[0], cos, sin).astype(ko_ref.dtype)
    for h in range(q.shape[1] // hd):
        qo_ref[0, h] = q[:, h * hd:(h + 1) * hd]
        ko_ref[0, h] = k[:, h * hd:(h + 1) * hd]
    vo_ref[0] = v_ref[0].astype(vo_ref.dtype)


def rope_qkv(pm, cos, sin, uq, uk, uv, scale, ts=512):
    b, s, _ = pm.shape
    wq = D_BRANCH
    nh = wq // HEAD_DIM
    tab = pl.BlockSpec((1, ts, LANE), lambda bi, i: (bi, i, 0))
    col = lambda u: pl.BlockSpec((1, ts, wq), lambda bi, i: (bi, i, u * LANE // wq))
    heads = pl.BlockSpec((1, nh, ts, HEAD_DIM), lambda bi, i: (bi, 0, i, 0))
    return pl.pallas_call(
        functools.partial(_rope_qkv_kernel, scale=scale),
        grid=(b, s // ts),
        in_specs=[col(uq), col(uk), col(uv), tab, tab],
        out_specs=[heads, heads, pl.BlockSpec((1, ts, wq), lambda bi, i: (bi, i, 0))],
        out_shape=[jax.ShapeDtypeStruct((b, nh, s, HEAD_DIM), BF16)] * 2
                  + [jax.ShapeDtypeStruct((b, s, wq), BF16)],
        compiler_params=_cp(("parallel", "parallel")),
        name="rope_qkv",
    )(pm, pm, pm, cos, sin)


def _lane_fold(x, op):
    out = x[..., 0:LANE]
    for c in range(1, x.shape[-1] // LANE):
        out = op(out, x[..., c * LANE:(c + 1) * LANE])
    return out


def _paired_loop(n, step):
    def body(jj, carry):
        step(2 * jj)
        step(2 * jj + 1)
        return carry
    lax.fori_loop(0, n // 2, body, 0)

    @pl.when(n % 2 == 1)
    def _():
        step(n - 1)


def _diff_attn_kernel(q_ref, k_ref, v_ref, lam_ref, g_ref, o_ref, s_ref, m_ref, l_ref, acc_ref, *, tq, tk, lam_init):
    i = pl.program_id(2)
    nfull = (i * tq) // tk
    row = lax.broadcasted_iota(jnp.int32, (tq, tk), 0)
    col = lax.broadcasted_iota(jnp.int32, (tq, tk), 1)
    vis = (col >> 6) <= (row >> 6)

    m_ref[...] = jnp.full_like(m_ref, NEG)

    def max_step(j, masked=False):
        for mm in range(2):
            kb = k_ref[0, mm, pl.ds(pl.multiple_of(j * tk, tk), tk), :]
            s = _dot(q_ref[0, mm], kb, NT_DIMS)
            if masked:
                s = jnp.where(vis, s, NEG)
            s_ref[j, mm] = s
            m_ref[mm] = jnp.maximum(m_ref[mm], _lane_fold(s, jnp.maximum))

    _paired_loop(nfull, max_step)
    max_step(nfull, True)
    for mm in range(2):
        m_ref[mm] = jnp.broadcast_to(jnp.max(m_ref[mm], axis=1, keepdims=True), (tq, LANE))
    l_ref[...] = jnp.zeros_like(l_ref)
    acc_ref[...] = jnp.zeros_like(acc_ref)

    def acc_step(j):
        vb = v_ref[0, pl.ds(pl.multiple_of(j * tk, tk), tk), :]
        for mm in range(2):
            s = s_ref[j, mm]
            mb = m_ref[mm]
            ps = [jnp.exp2(s[:, c * LANE:(c + 1) * LANE] - mb) for c in range(tk // LANE)]
            l_ref[mm] += functools.reduce(jnp.add, ps)
            acc_ref[mm] += _dot(jnp.concatenate(ps, axis=1).astype(BF16), vb)

    _paired_loop(nfull + 1, acc_step)

    lv = lam_ref[...]
    lam = (jnp.exp(jnp.sum(lv[0:1] * lv[1:2], keepdims=True))
           - jnp.exp(jnp.sum(lv[2:3] * lv[3:4], keepdims=True)) + lam_init)
    l0 = jnp.sum(l_ref[0], axis=1, keepdims=True)
    l1 = jnp.sum(l_ref[1], axis=1, keepdims=True)
    o = acc_ref[0] / l0 - lam * (acc_ref[1] / l1)
    ms = jnp.mean(o * o, axis=-1, keepdims=True)
    o = o * lax.rsqrt(ms + EPS) * g_ref[...] * (1.0 - lam_init)
    o_ref[0] = o.astype(o_ref.dtype)


def diff_attention(pm, cos, sin, lam_vecs, subln_g, layer, tq=512):
    b, s, _ = pm.shape
    tk = tq
    qa, ka, va = rope_qkv(pm, cos, sin, U_AQ, U_AK, U_AV, HEAD_DIM ** -0.5 * LOG2E)
    lam_init = 0.8 - 0.6 * math.exp(-0.3 * layer)
    return pl.pallas_call(
        functools.partial(_diff_attn_kernel, tq=tq, tk=tk, lam_init=lam_init),
        grid=(b, DIFF_HEADS, s // tq),
        in_specs=[pl.BlockSpec((1, 2, tq, HEAD_DIM), lambda bi, h, i: (bi, h, i, 0)),
                  pl.BlockSpec((1, 2, s, HEAD_DIM), lambda bi, h, i: (bi, h, 0, 0)),
                  pl.BlockSpec((1, s, LANE), lambda bi, h, i: (bi, 0, h)),
                  pl.BlockSpec((4, HEAD_DIM), lambda bi, h, i: (0, 0)),
                  pl.BlockSpec((1, LANE), lambda bi, h, i: (0, 0))],
        out_specs=pl.BlockSpec((1, tq, LANE), lambda bi, h, i: (bi, i, h)),
        out_shape=jax.ShapeDtypeStruct((b, s, D_BRANCH), BF16),
        scratch_shapes=[pltpu.VMEM((s // tk, 2, tq, tk), F32),
                        pltpu.VMEM((2, tq, LANE), F32), pltpu.VMEM((2, tq, LANE), F32),
                        pltpu.VMEM((2, tq, LANE), F32)],
        compiler_params=_cp(("parallel", "parallel", "parallel")),
        name="diff_attention",
    )(qa, ka, va, lam_vecs, subln_g.reshape(1, LANE))


def _softplus(z):
    return jnp.maximum(z, 0.0) + jnp.log(1.0 + jnp.exp(-jnp.abs(z)))


def _rwkv_prep_kernel(x_ref, prev_ref, mu_ref, wup_ref, aup_ref, gup_ref, w0_ref, a0_ref, kkg_ref, kag_ref,
                      seg_ref, r_ref, k_ref, v_ref, al_ref, be_ref, lw_ref, g_ref):
    i = pl.program_id(1)
    p = x_ref[0]
    t = p.shape[0]
    row = lax.broadcasted_iota(jnp.int32, p.shape, 0)
    last = prev_ref[0][7:8, :]
    last = jnp.where(i == 0, jnp.zeros_like(last), last)
    prev = jnp.where(row == 0, jnp.broadcast_to(last, p.shape), pltpu.roll(p, 1, axis=0))
    ps = p + (prev - p) * mu_ref[...]
    d = D_BRANCH
    r, k, v = ps[:, 0:d], ps[:, d:2 * d], ps[:, 2 * d:3 * d]
    wd = ps[:, 3 * d:3 * d + 64]
    ad = ps[:, 3 * d + 64:3 * d + 128]
    gd = ps[:, 3 * d + 128:3 * d + 256]
    w_log = -_softplus(-(w0_ref[...] + _dot_hi(jnp.tanh(wd), wup_ref[...]))) - 0.5
    a = jax.nn.sigmoid(a0_ref[...] + _dot_hi(ad, aup_ref[...]))
    g = _dot_hi(jax.nn.sigmoid(gd), gup_ref[...])
    kk = k * kkg_ref[...]
    nrm = jnp.sqrt(_dot_hi(kk * kk, seg_ref[...]))
    kk = kk / jnp.maximum(nrm, 1e-12)
    r_ref[0] = r
    k_ref[0] = k * (1.0 + (a - 1.0) * kag_ref[...])
    v_ref[0] = v
    al_ref[0] = -kk
    be_ref[0] = kk * a
    lw_ref[0] = -jnp.exp(w_log)
    g_ref[0] = g


def _bdot(a, b, dims):
    return lax.dot_general(a, b, ((dims[0], dims[1]), ((0,), (0,))), preferred_element_type=F32)


BNN = ((2,), (1,))
BNT = ((2,), (2,))


def _rwkv_chunk_kernel(r_ref, k_ref, v_ref, al_ref, be_ref, lw_ref, q_ref, en_ref, wl_ref, *, rows, sub):
    hd = HEAD_DIM
    nch = rows // CHUNK
    nsb = rows // sub
    lw = lw_ref[0]
    rowc = lax.broadcasted_iota(jnp.int32, lw.shape, 0) & (CHUNK - 1)
    cum = lw
    for dd in (1, 2, 4, 8, 16, 32):
        cum = cum + jnp.where(rowc >= dd, pltpu.roll(cum, dd, axis=0), 0.0)
    cl = jnp.concatenate(
        [jnp.broadcast_to(cum[(c + 1) * CHUNK - 1:(c + 1) * CHUNK, :], (CHUNK, LANE)) for c in range(nch)], axis=0)
    e_in = jnp.exp(cum)
    e_out = jnp.exp(-cum)
    e_end = jnp.exp(cl - cum)
    r2, k2, v2, al2, be2 = r_ref[0], k_ref[0], v_ref[0], al_ref[0], be_ref[0]
    at2 = al2 * jnp.exp(cum - lw)
    rt2 = r2 * e_in
    bt2 = be2 * e_out
    kt2 = k2 * e_out
    bh2 = be2 * e_end
    kh2 = k2 * e_end
    wl2 = jnp.exp(cl)

    row = lax.broadcasted_iota(jnp.int32, (sub, sub), 0)
    col = lax.broadcasted_iota(jnp.int32, (sub, sub), 1)
    same = (row >> 6) == (col >> 6)
    m_strict = same & (row > col)
    m_incl = same & (row >= col)
    eye = (row == col).astype(F32)
    m_blk8 = (row >> 3) == (col >> 3)
    m_lvls = [((row >> (sh + 1)) == (col >> (sh + 1))) & ((row >> sh) != (col >> sh)) for sh in (3, 4, 5)]
    wl_ref[0] = jnp.concatenate([wl2[c * CHUNK:c * CHUNK + 1, :] for c in range(nch)], axis=0)

    def stack(a):
        return jnp.stack([a[sb * sub:(sb + 1) * sub, hh * hd:(hh + 1) * hd]
                          for hh in range(2) for sb in range(nsb)])

    at, rt, vb = stack(at2), stack(rt2), stack(v2).astype(BF16)
    lhs = jnp.concatenate([at, rt], axis=1).astype(BF16)
    rhs = jnp.concatenate([stack(bt2), stack(kt2)], axis=1).astype(BF16)
    gm = _bdot(lhs, rhs, BNT)
    n_ab = jnp.where(m_strict[None], gm[:, :sub, :sub], 0.0)
    a_ak = jnp.where(m_strict[None], gm[:, :sub, sub:], 0.0).astype(BF16)
    a_rb = jnp.where(m_incl[None], gm[:, sub:, :sub], 0.0).astype(BF16)
    a_rk = jnp.where(m_incl[None], gm[:, sub:, sub:], 0.0).astype(BF16)
    n8 = jnp.where(m_blk8[None], n_ab, 0.0)
    n8b = n8.astype(BF16)
    n_2 = _bdot(n8b, n8b, BNN)
    n2b = n_2.astype(BF16)
    n_3 = _bdot(n2b, n8b, BNN)
    n_4 = _bdot(n2b, n2b, BNN)
    tinv = eye[None] + n8 + n_2 + n_3
    tinv = tinv + _bdot(tinv.astype(BF16), n_4.astype(BF16), BNN)
    for m_lvl in m_lvls:
        nl = jnp.where(m_lvl[None], n_ab, 0.0).astype(BF16)
        tb = tinv.astype(BF16)
        tinv = tinv + _bdot(_bdot(tb, nl, BNN).astype(BF16), tb, BNN)
    tb = tinv.astype(BF16)
    akv = _bdot(a_ak, vb, BNN)
    pmat = _bdot(tb, jnp.concatenate([at, akv], axis=2).astype(BF16), BNN)
    qmat = _bdot(a_rb, pmat.astype(BF16), BNN) + jnp.concatenate([rt, _bdot(a_rk, vb, BNN)], axis=2)
    pb = pmat.astype(BF16)
    bhb = stack(bh2).astype(BF16)
    khb = stack(kh2).astype(BF16)
    for hh in range(2):
        for sb in range(nsb):
            bi = hh * nsb + sb
            q_ref[0, hh, sb * sub:(sb + 1) * sub, :] = qmat[bi]
            for cc in range(sub // CHUNK):
                cs = slice(cc * CHUNK, (cc + 1) * CHUNK)
                mn = _dot(pb[bi, cs], bhb[bi, cs], TN_DIMS)
                n_c = mn[hd:] + _dot(vb[bi, cs], khb[bi, cs], TN_DIMS)
                en_ref[0, hh, sb * (sub // CHUNK) + cc] = jnp.concatenate([mn[:hd], n_c], axis=0)


def _rwkv_state_kernel(q_ref, en_ref, wl_ref, r_ref, k_ref, v_ref, g_ref, rk_ref, gng_ref, gnb_ref,
                       o_ref, st_ref, *, rows):
    t = pl.program_id(2)
    hd = HEAD_DIM
    nch = rows // CHUNK

    @pl.when(t == 0)
    def _():
        st_ref[...] = jnp.zeros_like(st_ref)

    wl = wl_ref[0]
    r2, k2, v2, g2 = r_ref[0], k_ref[0], v_ref[0], g_ref[0]
    sts = [st_ref[0], st_ref[1]]
    ys = [[], []]
    for c in range(nch):
        for hh in range(2):
            qc = q_ref[0, hh, c * CHUNK:(c + 1) * CHUNK, :]
            en = en_ref[0, hh, c]
            stb = sts[hh].astype(BF16)
            ys[hh].append(_dot(qc[:, :hd].astype(BF16), stb, NT_DIMS) + qc[:, hd:])
            sts[hh] = (sts[hh] * wl[c:c + 1, hh * hd:(hh + 1) * hd]
                       + _dot(stb, en[:hd].astype(BF16)) + en[hd:])
    outs = []
    for hh in range(2):
        sl = slice(hh * hd, (hh + 1) * hd)
        st_ref[hh] = sts[hh]
        v, r, kp = v2[:, sl], r2[:, sl], k2[:, sl]
        y = jnp.concatenate(ys[hh], axis=0)
        mu = jnp.mean(y, axis=-1, keepdims=True)
        var = jnp.mean(jnp.square(y - mu), axis=-1, keepdims=True)
        yn = (y - mu) * lax.rsqrt(var + RWKV_GN_EPS) * gng_ref[:, sl] + gnb_ref[:, sl]
        bonus = jnp.sum(r * kp * rk_ref[0, hh:hh + 1, :], axis=-1, keepdims=True) * v
        outs.append((yn + bonus) * g2[:, sl])
    o_ref[0] = jnp.concatenate(outs, axis=1).astype(o_ref.dtype)


def rwkv7(pm, mu, w_up, a_up, g_up, w0, a0, kk_gain, ka_gain, r_k, gn_g, gn_b, ts=256, rows=512, sub=256):
    b, s, _ = pm.shape
    d = D_BRANCH
    seg = np.kron(np.eye(d // HEAD_DIM, dtype=np.float32), np.ones((HEAD_DIM, HEAD_DIM), np.float32))
    row1 = lambda a: a.reshape(1, -1)
    cst = lambda shape: pl.BlockSpec(shape, lambda bi, i: (0,) * len(shape))
    blk = pl.BlockSpec((1, ts, d), lambda bi, i: (bi, i, 0))
    r, k, v, al, be, lw, g = pl.pallas_call(
        _rwkv_prep_kernel,
        grid=(b, s // ts),
        in_specs=[pl.BlockSpec((1, ts, B_WIDTH), lambda bi, i: (bi, i, 0)),
                  pl.BlockSpec((1, 8, B_WIDTH), lambda bi, i: (bi, jnp.maximum(i * (ts // 8) - 1, 0), 0)),
                  cst((1, B_WIDTH)), cst((64, d)), cst((64, d)), cst((128, d)),
                  cst((1, d)), cst((1, d)), cst((1, d)), cst((1, d)), cst((d, d))],
        out_specs=[blk] * 7,
        out_shape=[jax.ShapeDtypeStruct((b, s, d), F32)] * 7,
        compiler_params=_cp(("parallel", "arbitrary")),
        name="rwkv_prep",
    )(pm, pm, row1(mu), w_up, a_up, g_up, row1(w0), row1(a0), row1(kk_gain), row1(ka_gain), jnp.asarray(seg))

    hp = pl.BlockSpec((1, rows, LANE), lambda bi, h, t: (bi, t, h))
    par = pl.BlockSpec((1, LANE), lambda bi, h, t: (0, h))
    nh = d // HEAD_DIM
    nch = rows // CHUNK
    q_spec = pl.BlockSpec((1, 2, rows, LANE), lambda bi, h, t: (bi, h, t, 0))
    en_spec = pl.BlockSpec((1, 2, nch, LANE, HEAD_DIM), lambda bi, h, t: (bi, h, t, 0, 0))
    wl_spec = pl.BlockSpec((1, nch, LANE), lambda bi, h, t: (bi, t, h))
    grid = (b, d // LANE, s // rows)
    qm, en, wl = pl.pallas_call(
        functools.partial(_rwkv_chunk_kernel, rows=rows, sub=sub),
        grid=grid,
        in_specs=[hp] * 6,
        out_specs=[q_spec, en_spec, wl_spec],
        out_shape=[jax.ShapeDtypeStruct((b, nh, s, LANE), F32),
                   jax.ShapeDtypeStruct((b, nh, s // CHUNK, LANE, HEAD_DIM), F32),
                   jax.ShapeDtypeStruct((b, s // CHUNK, d), F32)],
        compiler_params=_cp(("parallel", "parallel", "parallel")),
        name="rwkv_chunk",
    )(r, k, v, al, be, lw)
    return pl.pallas_call(
        functools.partial(_rwkv_state_kernel, rows=rows),
        grid=grid,
        in_specs=[q_spec, en_spec, wl_spec, hp, hp, hp, hp,
                  pl.BlockSpec((1, 2, HEAD_DIM), lambda bi, h, t: (h, 0, 0)), par, par],
        out_specs=hp,
        out_shape=jax.ShapeDtypeStruct((b, s, d), BF16),
        scratch_shapes=[pltpu.VMEM((2, HEAD_DIM, HEAD_DIM), F32)],
        compiler_params=_cp(("parallel", "parallel", "arbitrary")),
        name="rwkv_state",
    )(qm, en, wl, r, k, v, g, r_k.reshape(-1, 2, HEAD_DIM), row1(gn_g), row1(gn_b))


def _hgrn_kernel(q_ref, f_ref, i_ref, g_ref, lb_ref, ng_ref, o_ref, st_ref, *, rows):
    t = pl.program_id(2)
    sub = 16

    @pl.when(t == 0)
    def _():
        st_ref[...] = jnp.zeros_like(st_ref)

    q = jax.nn.silu(q_ref[0])
    lb = lb_ref[...]
    fg = lb + (1.0 - lb) * jax.nn.sigmoid(f_ref[0])
    lf = jnp.log(fg)
    kf = 1.0 - fg
    iv = i_ref[0]
    rowc = lax.broadcasted_iota(jnp.int32, lf.shape, 0) & (CHUNK - 1)
    bcum = lf
    for dd in (1, 2, 4, 8, 16, 32):
        bcum = bcum + jnp.where(rowc >= dd, pltpu.roll(bcum, dd, axis=0), 0.0)
    ivb = iv.astype(BF16)
    rsub = lax.broadcasted_iota(jnp.int32, (sub, LANE), 0)
    st = st_ref[...]
    outs = []
    for c in range(rows // CHUNK):
        c0 = c * CHUNK
        bc = bcum[c0:c0 + CHUNK]
        qc = q[c0:c0 + CHUNK]
        kc = kf[c0:c0 + CHUNK]
        ic = iv[c0:c0 + CHUNK]
        icb = ivb[c0:c0 + CHUNK]
        o_inter = _dot((qc * jnp.exp(bc)).astype(BF16), st.astype(BF16), NT_DIMS)
        blocks = []
        for ib in range(CHUNK // sub):
            r0 = ib * sub
            bi = bc[r0:r0 + sub]
            qi = qc[r0:r0 + sub]
            ki = kc[r0:r0 + sub]
            ii = ic[r0:r0 + sub]
            acc = jnp.zeros((sub, LANE), F32)
            for s_ in range(sub):
                e = jnp.where(rsub >= s_, jnp.exp(jnp.minimum(bi - bi[s_:s_ + 1], 0.0)), 0.0)
                a = jnp.sum(qi * (ki[s_:s_ + 1] * e), axis=-1, keepdims=True)
                acc = acc + a * ii[s_:s_ + 1]
            if ib > 0:
                ref_row = bc[r0 - 1:r0]
                qt = (qi * jnp.exp(bi - ref_row)).astype(BF16)
                kt = (kc[0:r0] * jnp.exp(ref_row - bc[0:r0])).astype(BF16)
                a_off = _dot(qt, kt, NT_DIMS).astype(BF16)
                acc = acc + _dot(a_off, icb[0:r0])
            blocks.append(acc)
        outs.append(o_inter + jnp.concatenate(blocks, axis=0))
        bl = bc[CHUNK - 1:CHUNK]
        kdec = (kc * jnp.exp(bl - bc)).astype(BF16)
        st = st * jnp.exp(bl) + _dot(icb, kdec, TN_DIMS)
    st_ref[...] = st
    o = jnp.concatenate(outs, axis=0)
    ms = jnp.mean(o * o, axis=-1, keepdims=True)
    o = o * lax.rsqrt(ms + EPS) * ng_ref[...]
    o_ref[0] = (o * jax.nn.silu(g_ref[0])).astype(o_ref.dtype)


def hgrn2(pm, lb, norm_g, rows=256):
    b, s, _ = pm.shape
    heads = D_BRANCH // LANE
    spec = lambda u: pl.BlockSpec((1, rows, LANE), lambda bi, h, t: (bi, t, u + h))
    par = pl.BlockSpec((1, LANE), lambda bi, h, t: (0, h))
    return pl.pallas_call(
        functools.partial(_hgrn_kernel, rows=rows),
        grid=(b, heads, s // rows),
        in_specs=[spec(U_CQ), spec(U_CF), spec(U_CI), spec(U_CG), par, par],
        out_specs=spec(0),
        out_shape=jax.ShapeDtypeStruct((b, s, D_BRANCH), BF16),
        scratch_shapes=[pltpu.VMEM((LANE, LANE), F32)],
        compiler_params=_cp(("parallel", "parallel", "arbitrary")),
        name="hgrn2",
    )(pm, pm, pm, pm, lb.reshape(1, -1), norm_g.reshape(1, -1))


def _dsa_prep_kernel(q_ref, iq_ref, kv_ref, ikw_ref, cos_ref, sin_ref,
                     qo_ref, iqo_ref, ko_ref, vo_ref, iko_ref, wo_ref, *, scale, wscale):
    cos = cos_ref[0]
    sin = sin_ref[0]
    hd = HEAD_DIM
    q = (_rope(q_ref[0], cos, sin) * scale).astype(BF16)
    iq = _rope(iq_ref[0], cos, sin).astype(BF16)
    for h in range(DSA_HEADS):
        qo_ref[0, h] = q[:, h * hd:(h + 1) * hd]
        iqo_ref[0, h] = iq[:, h * hd:(h + 1) * hd]
    kv = kv_ref[0]
    ko_ref[0] = _rope(kv, cos, sin)[:, :hd].astype(BF16)
    vo_ref[0] = kv[:, hd:].astype(BF16)
    ikw = ikw_ref[0]
    iko_ref[0] = _rope(ikw, cos, sin)[:, :hd].astype(BF16)
    wo_ref[0] = ikw[:, hd:hd + IDX_HEADS] * wscale


def _to_key(x):
    x = jnp.where(x == 0.0, 0.0, x)
    bits = pltpu.bitcast(x, jnp.int32)
    return jnp.where(bits < 0, bits ^ jnp.int32(0x7FFFFFFF), bits)


def _dsa_index_kernel(iq_ref, ik_ref, wt_ref, mask_ref, key_ref, *, tq, tk, top_k):
    i = pl.program_id(1)
    nk = key_ref.shape[0]
    nvis = (i * tq + tq - 1) // tk + 1
    krow = lax.broadcasted_iota(jnp.int32, (tk, tq), 0)
    qcol = i * tq + lax.broadcasted_iota(jnp.int32, (tk, tq), 1)
    wt = wt_ref[0]

    def score_block(jb):
        ikb = ik_ref[0, pl.ds(pl.multiple_of(jb * tk, tk), tk), :]
        s_all = _dot(ikb, iq_ref[0].reshape(IDX_HEADS * tq, HEAD_DIM), NT_DIMS)
        acc = jnp.zeros((tk, tq), F32)
        for h in range(IDX_HEADS):
            acc = acc + jnp.maximum(s_all[:, h * tq:(h + 1) * tq], 0.0) * wt[h:h + 1, :]
        vis = ((jb * tk + krow) >> 6) <= (qcol >> 6)
        key_ref[jb] = jnp.where(vis, _to_key(acc), INT_MIN)

    _paired_loop(nvis, score_block)

    def count(pred_fn):
        def blk(jb, cnt):
            ind = jnp.where(pred_fn(key_ref[jb], jb), 1, 0)
            return cnt + jnp.sum(ind.reshape(tk // 8, 8, tq), axis=0)
        cnt = lax.fori_loop(0, nvis, blk, jnp.zeros((8, tq), jnp.int32))
        return jnp.sum(cnt, axis=0, keepdims=True)

    def bit_step(bi, prefix):
        cand = prefix | lax.shift_left(jnp.int32(1), 31 - bi)
        cand_s = cand ^ jnp.int32(INT_MIN)
        c = count(lambda kb, jb: kb >= cand_s)
        return jnp.where(c >= top_k, cand, prefix)

    prefix = lax.fori_loop(0, 32, bit_step, jnp.zeros((1, tq), jnp.int32))
    tau = prefix ^ jnp.int32(INT_MIN)
    n_gt = count(lambda kb, jb: kb > tau)
    n_eq = count(lambda kb, jb: kb == tau)
    need = top_k - n_gt
    tied = (n_eq > need) & (tau != INT_MIN)
    n_cols = nk * tk

    idx_bits = int(n_cols).bit_length()

    def tie_break():
        def idx_step(bi, pre):
            cand = pre | lax.shift_left(jnp.int32(1), idx_bits - 1 - bi)
            c = count(lambda kb, jb: (kb == tau) & ((jb * tk + krow) < cand))
            return jnp.where(c < need, cand, pre)
        cut = lax.fori_loop(0, idx_bits, idx_step, jnp.zeros((1, tq), jnp.int32))
        return jnp.where(tied, cut, n_cols)

    cut = lax.cond(jnp.max(tied.astype(jnp.int32)) > 0, tie_break,
                   lambda: jnp.full((1, tq), n_cols, jnp.int32))

    def write_block(jb, carry):
        kb = key_ref[jb]
        sel = (kb > tau) | ((kb == tau) & ((jb * tk + krow) <= cut))
        sel = sel & (kb != INT_MIN)
        mask_ref[0, jb] = jnp.where(sel, 1.0, 0.0).T.astype(mask_ref.dtype)
        return carry

    def zero_block(jb, carry):
        mask_ref[0, jb] = jnp.zeros((tq, tk), mask_ref.dtype)
        return carry

    lax.fori_loop(0, nvis, write_block, 0)
    lax.fori_loop(nvis, nk, zero_block, 0)


def _dsa_attn_kernel(q_ref, k_ref, v_ref, mask_ref, o_ref, s_ref, m_ref, l_ref, acc_ref, *, tq, tk):
    i = pl.program_id(1)
    nh = DSA_HEADS
    hd = HEAD_DIM
    nvis = (i * tq + tq - 1) // tk + 1
    nchunk = tk // LANE

    m_ref[...] = jnp.full_like(m_ref, NEG)

    def max_step(j):
        kb = k_ref[0, pl.ds(pl.multiple_of(j * tk, tk), tk), :]
        s = _dot(q_ref[0].reshape(nh * tq, hd), kb, NT_DIMS).reshape(nh, tq, tk)
        s = jnp.where((mask_ref[0, j] > 0)[None], s, NEG)
        s_ref[j] = s
        m_ref[...] = jnp.maximum(m_ref[...], _lane_fold(s, jnp.maximum))

    _paired_loop(nvis, max_step)
    m_ref[...] = jnp.broadcast_to(jnp.max(m_ref[...], axis=2, keepdims=True), m_ref.shape)
    l_ref[...] = jnp.zeros_like(l_ref)
    acc_ref[...] = jnp.zeros_like(acc_ref)

    def acc_step(j):
        s = s_ref[j]
        mb = m_ref[...]
        ps = [jnp.exp2(s[:, :, c * LANE:(c + 1) * LANE] - mb) for c in range(nchunk)]
        l_ref[...] += functools.reduce(jnp.add, ps)
        p = jnp.concatenate(ps, axis=2).astype(BF16).reshape(nh * tq, tk)
        vb = v_ref[0, pl.ds(pl.multiple_of(j * tk, tk), tk), :]
        acc_ref[...] += _dot(p, vb).reshape(nh, tq, hd)

    _paired_loop(nvis, acc_step)
    o = acc_ref[...] / jnp.sum(l_ref[...], axis=2, keepdims=True)
    o_ref[0] = jnp.concatenate([o[h] for h in range(nh)], axis=1).astype(o_ref.dtype)


def dsa_attention(pm, cos, sin, ts=512, tq=128, tk=512):
    b, s, _ = pm.shape
    top_k = min(TOPK_MAX, s // 4)
    hd = HEAD_DIM
    nk = s // tk
    wide = lambda u: pl.BlockSpec((1, ts, D_BRANCH), lambda bi, i: (bi, i, u * LANE // D_BRANCH))
    one = lambda u: pl.BlockSpec((1, ts, LANE), lambda bi, i: (bi, i, u))
    heads_out = pl.BlockSpec((1, DSA_HEADS, ts, hd), lambda bi, i: (bi, 0, i, 0))
    narrow = lambda w: pl.BlockSpec((1, ts, w), lambda bi, i: (bi, i, 0))
    q, iq, k, v, ik, w = pl.pallas_call(
        functools.partial(_dsa_prep_kernel, scale=hd ** -0.5 * LOG2E, wscale=(IDX_HEADS ** -0.5) * (hd ** -0.5)),
        grid=(b, s // ts),
        in_specs=[wide(U_DQ), wide(U_DIQ), one(U_DKV), one(U_DIKW), one(0), one(0)],
        out_specs=[heads_out, heads_out, narrow(hd), narrow(hd), narrow(hd), narrow(IDX_HEADS)],
        out_shape=[jax.ShapeDtypeStruct((b, DSA_HEADS, s, hd), BF16)] * 2
                  + [jax.ShapeDtypeStruct((b, s, hd), BF16)] * 3
                  + [jax.ShapeDtypeStruct((b, s, IDX_HEADS), F32)],
        compiler_params=_cp(("parallel", "parallel")),
        name="dsa_prep",
    )(pm, pm, pm, pm, cos, sin)

    mask_spec = pl.BlockSpec((1, nk, tq, tk), lambda bi, i: (bi, 0, i, 0))
    whole = pl.BlockSpec((1, s, hd), lambda bi, i: (bi, 0, 0))
    mask = pl.pallas_call(
        functools.partial(_dsa_index_kernel, tq=tq, tk=tk, top_k=top_k),
        grid=(b, s // tq),
        in_specs=[pl.BlockSpec((1, IDX_HEADS, tq, hd), lambda bi, i: (bi, 0, i, 0)),
                  whole,
                  pl.BlockSpec((1, IDX_HEADS, tq), lambda bi, i: (bi, 0, i))],
        out_specs=mask_spec,
        out_shape=jax.ShapeDtypeStruct((b, nk, s, tk), BF16),
        scratch_shapes=[pltpu.VMEM((nk, tk, tq), jnp.int32)],
        compiler_params=_cp(("parallel", "parallel")),
        name="dsa_index",
    )(iq, ik, jnp.swapaxes(w, 1, 2))

    return pl.pallas_call(
        functools.partial(_dsa_attn_kernel, tq=tq, tk=tk),
        grid=(b, s // tq),
        in_specs=[pl.BlockSpec((1, DSA_HEADS, tq, hd), lambda bi, i: (bi, 0, i, 0)),
                  whole, whole, mask_spec],
        out_specs=pl.BlockSpec((1, tq, D_BRANCH), lambda bi, i: (bi, i, 0)),
        out_shape=jax.ShapeDtypeStruct((b, s, D_BRANCH), BF16),
        scratch_shapes=[pltpu.VMEM((nk, DSA_HEADS, tq, tk), F32),
                        pltpu.VMEM((DSA_HEADS, tq, LANE), F32), pltpu.VMEM((DSA_HEADS, tq, LANE), F32),
                        pltpu.VMEM((DSA_HEADS, tq, hd), F32)],
        compiler_params=_cp(("parallel", "parallel")),
        name="dsa_attention",
    )(q, k, v, mask)


def _split_w_in(w_in):
    c = lambda a, b_: w_in[:, :, a:b_]
    pad = jnp.zeros(w_in.shape[:2] + (LANE - 64 - IDX_HEADS,), w_in.dtype)
    main = jnp.concatenate([
        c(1536, 3328),
        c(5888, 6016),
        c(6528, 6600), pad,
        c(0, 1536),
        c(3328, 5376),
        c(5376, 5888), c(6016, 6528),
    ], axis=2).astype(BF16)
    gate = c(6600, 6600 + N_BRANCH * D_MODEL).astype(BF16)
    return main, gate


def kernel(x, positions, norm_g, w_in, diff_lambda, diff_subln_g, rwkv_mu, rwkv_w_up, rwkv_a_up, rwkv_g_up, rwkv_w0, rwkv_a0, rwkv_k_k, rwkv_k_a, rwkv_r_k, rwkv_gn_g, rwkv_gn_b, hgrn_lb_logits, hgrn_norm_g, w_branch, w_out, mlp_w1, mlp_w2):
    b, s, d = x.shape
    m = b * s
    depth = w_in.shape[0]
    lb = jax.nn.softmax(hgrn_lb_logits.astype(F32), axis=0)
    lb = jnp.cumsum(lb, axis=0) - lb[0]
    w_main, w_gate = _split_w_in(w_in)
    w_out_b = w_out.astype(BF16)
    w2_b = mlp_w2.astype(BF16)
    cos, sin = rope_tables(positions)
    cos3 = cos.reshape(b, s, LANE)
    sin3 = sin.reshape(b, s, LANE)
    x2 = x.reshape(m, d)
    h = rmsnorm_bf16(x2, norm_g[0, 0])
    for l in range(depth):
        pm = matmul(h, w_main, l, None, F32, name="proj_main").reshape(b, s, N_MAIN)
        gate = matmul(h, w_gate, l, "sigmoid", BF16, name="proj_gate")
        y_a = diff_attention(pm, cos3, sin3, diff_lambda[l], diff_subln_g[l], l)
        y_b = rwkv7(pm, rwkv_mu[l], rwkv_w_up[l], rwkv_a_up[l], rwkv_g_up[l], rwkv_w0[l], rwkv_a0[l],
                    rwkv_k_k[l], rwkv_k_a[l], rwkv_r_k[l], rwkv_gn_g[l], rwkv_gn_b[l])
        y_c = hgrn2(pm, lb[l], hgrn_norm_g[l])
        y_d = dsa_attention(pm, cos3, sin3)
        ys = [y.reshape(m, D_BRANCH) for y in (y_a, y_b, y_c, y_d)]
        merged = gated_merge(ys, w_branch, l, gate)
        x2, h2 = matmul_norm_residual(merged, w_out_b, l, x2, norm_g[l, 1], norm_g[l, 2], name="out_proj")
        ff = matmul(h2, mlp_w1, l, "relu2", BF16, name="mlp_up")
        if l + 1 < depth:
            x2, h = matmul_norm_residual(ff, w2_b, l, x2, norm_g[l, 3], norm_g[l + 1, 0], name="mlp_down")
        else:
            x2 = matmul_norm_residual(ff, w2_b, l, x2, norm_g[l, 3], name="mlp_down")
    return x2.reshape(b, s, d)
```

```python
import functools
import math

import numpy as np
import jax
import jax.numpy as jnp
from jax import lax
from jax.experimental import pallas as pl
from jax.experimental.pallas import tpu as pltpu

F32 = jnp.float32
BF16 = jnp.bfloat16

D_MODEL = 2048
D_BRANCH = 512
D_FF = 8192
N_BRANCH = 4
CHUNK = 64
EPS = 1e-6
ROPE_THETA = 10000.0
HEAD_DIM = 64
DIFF_HEADS = 4
RWKV_GN_EPS = 64e-5
DSA_HEADS = 8
IDX_HEADS = 8
TOPK_MAX = 256

LANE = 128
VMEM_LIMIT = 56 * 1024 * 1024

U_AQ, U_AK, U_AV = 0, 4, 8
U_B = 12
U_CQ, U_CF, U_CI, U_CG = 26, 30, 34, 38
U_DQ, U_DKV, U_DIQ, U_DIKW = 42, 46, 47, 51
U_D_LO, U_D_HI = 40, 48
N_MAIN = 52 * LANE
B_WIDTH = 1792
B_BLOCK = 3584

NEG = -1e30
INT_MIN = -2147483648
LOG2E = 1.4426950408889634

NT_DIMS = (((1,), (1,)), ((), ()))
TN_DIMS = (((0,), (0,)), ((), ()))


def _cp(sem, vmem=VMEM_LIMIT):
    return pltpu.CompilerParams(dimension_semantics=sem, vmem_limit_bytes=vmem)


def _dot(a, b, dims=None):
    if dims is None:
        return jnp.dot(a, b, preferred_element_type=F32)
    return lax.dot_general(a, b, dims, preferred_element_type=F32)


def _dot_hi(a, b, dims=(((1,), (0,)), ((), ()))):
    return lax.dot_general(a, b, dims, preferred_element_type=F32,
                           precision=lax.Precision.HIGHEST)


def _rmsnorm_kernel(x_ref, g_ref, o_ref):
    x = x_ref[...]
    ms = jnp.mean(x * x, axis=-1, keepdims=True)
    o_ref[...] = (x * lax.rsqrt(ms + EPS) * g_ref[...]).astype(o_ref.dtype)


def rmsnorm_bf16(x, g, tm=512):
    m, d = x.shape
    return pl.pallas_call(
        _rmsnorm_kernel,
        grid=(m // tm,),
        in_specs=[pl.BlockSpec((tm, d), lambda i: (i, 0)),
                  pl.BlockSpec((1, d), lambda i: (0, 0))],
        out_specs=pl.BlockSpec((tm, d), lambda i: (i, 0)),
        out_shape=jax.ShapeDtypeStruct((m, d), BF16),
        compiler_params=_cp(("parallel",)),
        name="rmsnorm",
    )(x, g.reshape(1, d))


def _mm_kernel(x_ref, w_ref, o_ref, *, act):
    acc = _dot(x_ref[...], w_ref[...].astype(BF16))
    if act == "sigmoid":
        acc = 0.5 * jnp.tanh(0.5 * acc) + 0.5
    elif act == "relu2":
        acc = jnp.square(jnp.maximum(acc, 0.0))
    o_ref[...] = acc.astype(o_ref.dtype)


def matmul(x, w, layer, act, out_dtype, n=None, tm=2048, tn=512, name="mm"):
    m, k = x.shape
    n = w.shape[2] if n is None else n
    tm = min(tm, m)
    return pl.pallas_call(
        functools.partial(_mm_kernel, act=act),
        grid=(m // tm, n // tn),
        in_specs=[pl.BlockSpec((tm, k), lambda i, j: (i, 0)),
                  pl.BlockSpec((None, k, tn), lambda i, j: (layer, 0, j))],
        out_specs=pl.BlockSpec((tm, tn), lambda i, j: (i, j)),
        out_shape=jax.ShapeDtypeStruct((m, n), out_dtype),
        compiler_params=_cp(("parallel", "parallel")),
        name=name,
    )(x, w)


def _norm_residual(y, x_ref, g_ref, o_ref, h_ref):
    ms = jnp.mean(y * y, axis=-1, keepdims=True)
    o = x_ref[...] + y * lax.rsqrt(ms + EPS) * g_ref[0:1, :]
    o_ref[...] = o
    if h_ref is not None:
        ms = jnp.mean(o * o, axis=-1, keepdims=True)
        h_ref[...] = (o * lax.rsqrt(ms + EPS) * g_ref[1:2, :]).astype(h_ref.dtype)


def _mm_norm_res_kernel(a_ref, w_ref, x_ref, g_ref, o_ref, *rest):
    h_ref, acc_ref = (rest[0], rest[1]) if len(rest) == 2 else (None, rest[0])
    kk = pl.program_id(1)

    @pl.when(kk == 0)
    def _():
        acc_ref[...] = jnp.zeros_like(acc_ref)

    acc_ref[...] += _dot(a_ref[...], w_ref[...])

    @pl.when(kk == pl.num_programs(1) - 1)
    def _():
        _norm_residual(acc_ref[...], x_ref, g_ref, o_ref, h_ref)


def _mm_norm_res_fullk_kernel(a_ref, w_ref, x_ref, g_ref, o_ref, *rest):
    _norm_residual(_dot(a_ref[...], w_ref[...]), x_ref, g_ref, o_ref, rest[0] if rest else None)


def matmul_norm_residual(a, w, layer, x, g, g_next=None, tm=512, tk=2048, name="mm_norm_res"):
    m, k = a.shape
    n = w.shape[2]
    gs = jnp.stack([g, g if g_next is None else g_next])
    out_shape = [jax.ShapeDtypeStruct((m, n), F32)]
    if g_next is not None:
        out_shape.append(jax.ShapeDtypeStruct((m, n), BF16))
    if k <= tk:
        row = lambda i: (i, 0)
        outs = pl.pallas_call(
            _mm_norm_res_fullk_kernel,
            grid=(m // tm,),
            in_specs=[pl.BlockSpec((tm, k), row), pl.BlockSpec((None, k, n), lambda i: (layer, 0, 0)),
                      pl.BlockSpec((tm, n), row), pl.BlockSpec((2, n), lambda i: (0, 0))],
            out_specs=[pl.BlockSpec((tm, n), row)] * len(out_shape),
            out_shape=out_shape,
            compiler_params=_cp(("parallel",)),
            name=name,
        )(a, w, x, gs)
    else:
        row = lambda i, kk: (i, 0)
        outs = pl.pallas_call(
            _mm_norm_res_kernel,
            grid=(m // tm, k // tk),
            in_specs=[pl.BlockSpec((tm, tk), lambda i, kk: (i, kk)),
                      pl.BlockSpec((None, tk, n), lambda i, kk: (layer, kk, 0)),
                      pl.BlockSpec((tm, n), row),
                      pl.BlockSpec((2, n), lambda i, kk: (0, 0))],
            out_specs=[pl.BlockSpec((tm, n), row)] * len(out_shape),
            out_shape=out_shape,
            scratch_shapes=[pltpu.VMEM((tm, n), F32)],
            compiler_params=_cp(("parallel", "arbitrary")),
            name=name,
        )(a, w, x, gs)
    return outs if g_next is not None else outs[0]


def _merge_kernel(ya, yb, yc, yd, wb, ga, gb, gc, gd, o_ref):
    acc = None
    for n, (y, g) in enumerate(((ya, ga), (yb, gb), (yc, gc), (yd, gd))):
        t = g[...].astype(F32) * _dot(y[...], wb[n].astype(BF16))
        acc = t if acc is None else acc + t
    o_ref[...] = acc.astype(o_ref.dtype)


def gated_merge(ys, wb, layer, gate, tm=512, tn=1024):
    m = ys[0].shape[0]
    nj = D_MODEL // tn
    y_spec = pl.BlockSpec((tm, D_BRANCH), lambda j, i: (i, 0))
    g_specs = [pl.BlockSpec((tm, tn), functools.partial(lambda j, i, n: (i, n * nj + j), n=n))
               for n in range(N_BRANCH)]
    return pl.pallas_call(
        _merge_kernel,
        grid=(nj, m // tm),
        in_specs=[y_spec] * 4 + [pl.BlockSpec((None, N_BRANCH, D_BRANCH, tn), lambda j, i: (layer, 0, 0, j))]
                 + g_specs,
        out_specs=pl.BlockSpec((tm, tn), lambda j, i: (i, j)),
        out_shape=jax.ShapeDtypeStruct((m, D_MODEL), BF16),
        compiler_params=_cp(("parallel", "parallel")),
        name="gated_merge",
    )(*ys, wb, gate, gate, gate, gate)


def _rope_table_kernel(pos_ref, inv_ref, sgn_ref, cos_ref, sin_ref):
    ang = pos_ref[...].astype(F32) * inv_ref[...]
    cos_ref[...] = jnp.cos(ang)
    sin_ref[...] = jnp.sin(ang) * sgn_ref[...]


def rope_tables(positions, tm=512):
    m = positions.size
    half = HEAD_DIM // 2
    inv = ROPE_THETA ** (-np.arange(0, HEAD_DIM, 2, dtype=np.float32) / HEAD_DIM)
    inv = np.tile(inv.astype(np.float32), 4).reshape(1, LANE)
    sgn = np.tile(np.concatenate([-np.ones(half, np.float32), np.ones(half, np.float32)]), 2).reshape(1, LANE)
    spec = pl.BlockSpec((tm, LANE), lambda i: (i, 0))
    cst = pl.BlockSpec((1, LANE), lambda i: (0, 0))
    return pl.pallas_call(
        _rope_table_kernel,
        grid=(m // tm,),
        in_specs=[pl.BlockSpec((tm, 1), lambda i: (i, 0)), cst, cst],
        out_specs=[spec, spec],
        out_shape=[jax.ShapeDtypeStruct((m, LANE), F32)] * 2,
        compiler_params=_cp(("parallel",)),
        name="rope_tables",
    )(positions.reshape(m, 1), jnp.asarray(inv), jnp.asarray(sgn))


def _rope(x, cos, sin):
    w = x.shape[1]
    n = w // LANE
    if n > 1:
        cos = jnp.concatenate([cos] * n, axis=1)
        sin = jnp.concatenate([sin] * n, axis=1)
    lane = lax.broadcasted_iota(jnp.int32, x.shape, 1)
    up = pltpu.roll(x, w - HEAD_DIM // 2, axis=1)
    dn = pltpu.roll(x, HEAD_DIM // 2, axis=1)
    rot = jnp.where((lane & (HEAD_DIM // 2)) == 0, up, dn)
    return x * cos + rot * sin


def _rope_qkv_kernel(q_ref, k_ref, v_ref, cos_ref, sin_ref, qo_ref, ko_ref, vo_ref, *, scale):
    cos = cos_ref[0]
    sin = sin_ref[0]
    hd = HEAD_DIM
    q = (_rope(q_ref[0], cos, sin) * scale).astype(qo_ref.dtype)
    k = _rope(k_ref[0], cos, sin).astype(ko_ref.dtype)
    for h in range(q.shape[1] // hd):
        qo_ref[0, h] = q[:, h * hd:(h + 1) * hd]
        ko_ref[0, h] = k[:, h * hd:(h + 1) * hd]
    vo_ref[0] = v_ref[0].astype(vo_ref.dtype)


def rope_qkv(pm, cos, sin, uq, uk, uv, scale, ts=512):
    b, s, _ = pm.shape
    wq = D_BRANCH
    nh = wq // HEAD_DIM
    tab = pl.BlockSpec((1, ts, LANE), lambda bi, i: (bi, i, 0))
    col = lambda u: pl.BlockSpec((1, ts, wq), lambda bi, i: (bi, i, u * LANE // wq))
    heads = pl.BlockSpec((1, nh, ts, HEAD_DIM), lambda bi, i: (bi, 0, i, 0))
    return pl.pallas_call(
        functools.partial(_rope_qkv_kernel, scale=scale),
        grid=(b, s // ts),
        in_specs=[col(uq), col(uk), col(uv), tab, tab],
        out_specs=[heads, heads, pl.BlockSpec((1, ts, wq), lambda bi, i: (bi, i, 0))],
        out_shape=[jax.ShapeDtypeStruct((b, nh, s, HEAD_DIM), BF16)] * 2
                  + [jax.ShapeDtypeStruct((b, s, wq), BF16)],
        compiler_params=_cp(("parallel", "parallel")),
        name="rope_qkv",
    )(pm, pm, pm, cos, sin)


def _lane_fold(x, op):
    out = x[..., 0:LANE]
    for c in range(1, x.shape[-1] // LANE):
        out = op(out, x[..., c * LANE:(c + 1) * LANE])
    return out


def _paired_loop(n, step):
    def body(jj, carry):
        step(2 * jj)
        step(2 * jj + 1)
        return carry
    lax.fori_loop(0, n // 2, body, 0)

    @pl.when(n % 2 == 1)
    def _():
        step(n - 1)


def _diff_attn_kernel(q_ref, k_ref, v_ref, lam_ref, g_ref, o_ref, s_ref, m_ref, l_ref, acc_ref, *, tq, tk, lam_init):
    i = pl.program_id(2)
    nfull = (i * tq) // tk
    row = lax.broadcasted_iota(jnp.int32, (tq, tk), 0)
    col = lax.broadcasted_iota(jnp.int32, (tq, tk), 1)
    vis = (col >> 6) <= (row >> 6)

    m_ref[...] = jnp.full_like(m_ref, NEG)

    def max_step(j, masked=False):
        for mm in range(2):
            kb = k_ref[0, mm, pl.ds(pl.multiple_of(j * tk, tk), tk), :]
            s = _dot(q_ref[0, mm], kb, NT_DIMS)
            if masked:
                s = jnp.where(vis, s, NEG)
            s_ref[j, mm] = s
            m_ref[mm] = jnp.maximum(m_ref[mm], _lane_fold(s, jnp.maximum))

    _paired_loop(nfull, max_step)
    max_step(nfull, True)
    for mm in range(2):
        m_ref[mm] = jnp.broadcast_to(jnp.max(m_ref[mm], axis=1, keepdims=True), (tq, LANE))
    l_ref[...] = jnp.zeros_like(l_ref)
    acc_ref[...] = jnp.zeros_like(acc_ref)

    def acc_step(j):
        vb = v_ref[0, pl.ds(pl.multiple_of(j * tk, tk), tk), :]
        for mm in range(2):
            s = s_ref[j, mm]
            mb = m_ref[mm]
            ps = [jnp.exp2(s[:, c * LANE:(c + 1) * LANE] - mb) for c in range(tk // LANE)]
            l_ref[mm] += functools.reduce(jnp.add, ps)
            acc_ref[mm] += _dot(jnp.concatenate(ps, axis=1).astype(BF16), vb)

    _paired_loop(nfull + 1, acc_step)

    lv = lam_ref[...]
    lam = (jnp.exp(jnp.sum(lv[0:1] * lv[1:2], keepdims=True))
           - jnp.exp(jnp.sum(lv[2:3] * lv[3:4], keepdims=True)) + lam_init)
    l0 = jnp.sum(l_ref[0], axis=1, keepdims=True)
    l1 = jnp.sum(l_ref[1], axis=1, keepdims=True)
    o = acc_ref[0] / l0 - lam * (acc_ref[1] / l1)
    ms = jnp.mean(o * o, axis=-1, keepdims=True)
    o = o * lax.rsqrt(ms + EPS) * g_ref[...] * (1.0 - lam_init)
    o_ref[0] = o.astype(o_ref.dtype)


def diff_attention(pm, cos, sin, lam_vecs, subln_g, layer, tq=512):
    b, s, _ = pm.shape
    tk = tq
    qa, ka, va = rope_qkv(pm, cos, sin, U_AQ, U_AK, U_AV, HEAD_DIM ** -0.5 * LOG2E)
    lam_init = 0.8 - 0.6 * math.exp(-0.3 * layer)
    return pl.pallas_call(
        functools.partial(_diff_attn_kernel, tq=tq, tk=tk, lam_init=lam_init),
        grid=(b, DIFF_HEADS, s // tq),
        in_specs=[pl.BlockSpec((1, 2, tq, HEAD_DIM), lambda bi, h, i: (bi, h, i, 0)),
                  pl.BlockSpec((1, 2, s, HEAD_DIM), lambda bi, h, i: (bi, h, 0, 0)),
                  pl.BlockSpec((1, s, LANE), lambda bi, h, i: (bi, 0, h)),
                  pl.BlockSpec((4, HEAD_DIM), lambda bi, h, i: (0, 0)),
                  pl.BlockSpec((1, LANE), lambda bi, h, i: (0, 0))],
        out_specs=pl.BlockSpec((1, tq, LANE), lambda bi, h, i: (bi, i, h)),
        out_shape=jax.ShapeDtypeStruct((b, s, D_BRANCH), BF16),
        scratch_shapes=[pltpu.VMEM((s // tk, 2, tq, tk), F32),
                        pltpu.VMEM((2, tq, LANE), F32), pltpu.VMEM((2, tq, LANE), F32),
                        pltpu.VMEM((2, tq, LANE), F32)],
        compiler_params=_cp(("parallel", "parallel", "parallel")),
        name="diff_attention",
    )(qa, ka, va, lam_vecs, subln_g.reshape(1, LANE))


def _softplus(z):
    return jnp.maximum(z, 0.0) + jnp.log(1.0 + jnp.exp(-jnp.abs(z)))


def _rwkv_prep_kernel(x_ref, prev_ref, mu_ref, wup_ref, aup_ref, gup_ref, w0_ref, a0_ref, kkg_ref, kag_ref,
                      seg_ref, r_ref, k_ref, v_ref, al_ref, be_ref, lw_ref, g_ref):
    i = pl.program_id(1)
    c0 = U_B * LANE
    p = x_ref[0, :, c0:c0 + B_WIDTH]
    row = lax.broadcasted_iota(jnp.int32, p.shape, 0)
    last = prev_ref[0, 7:8, c0:c0 + B_WIDTH]
    last = jnp.where(i == 0, jnp.zeros_like(last), last)
    prev = jnp.where(row == 0, jnp.broadcast_to(last, p.shape), pltpu.roll(p, 1, axis=0))
    ps = p + (prev - p) * mu_ref[...]
    d = D_BRANCH
    r, k, v = ps[:, 0:d], ps[:, d:2 * d], ps[:, 2 * d:3 * d]
    wd = ps[:, 3 * d:3 * d + 64]
    ad = ps[:, 3 * d + 64:3 * d + 128]
    gd = ps[:, 3 * d + 128:3 * d + 256]
    w_log = -_softplus(-(w0_ref[...] + _dot_hi(jnp.tanh(wd), wup_ref[...]))) - 0.5
    a = jax.nn.sigmoid(a0_ref[...] + _dot_hi(ad, aup_ref[...]))
    g = _dot_hi(jax.nn.sigmoid(gd), gup_ref[...])
    kk = k * kkg_ref[...]
    nrm = jnp.sqrt(_dot_hi(kk * kk, seg_ref[...]))
    kk = kk / jnp.maximum(nrm, 1e-12)
    r_ref[0] = r
    k_ref[0] = k * (1.0 + (a - 1.0) * kag_ref[...])
    v_ref[0] = v
    al_ref[0] = -kk
    be_ref[0] = kk * a
    lw_ref[0] = -jnp.exp(w_log)
    g_ref[0] = g


def _bdot(a, b, dims):
    return lax.dot_general(a, b, ((dims[0], dims[1]), ((0,), (0,))), preferred_element_type=F32)


BNN = ((2,), (1,))
BNT = ((2,), (2,))


def _rwkv_chunk_kernel(r_ref, k_ref, v_ref, al_ref, be_ref, lw_ref, q_ref, en_ref, wl_ref, *, rows, sub):
    hd = HEAD_DIM
    nch = rows // CHUNK
    nsb = rows // sub
    lw = lw_ref[0]
    rowc = lax.broadcasted_iota(jnp.int32, lw.shape, 0) & (CHUNK - 1)
    cum = lw
    for dd in (1, 2, 4, 8, 16, 32):
        cum = cum + jnp.where(rowc >= dd, pltpu.roll(cum, dd, axis=0), 0.0)
    cl = jnp.concatenate(
        [jnp.broadcast_to(cum[(c + 1) * CHUNK - 1:(c + 1) * CHUNK, :], (CHUNK, LANE)) for c in range(nch)], axis=0)
    e_in = jnp.exp(cum)
    e_out = jnp.exp(-cum)
    e_end = jnp.exp(cl - cum)
    r2, k2, v2, al2, be2 = r_ref[0], k_ref[0], v_ref[0], al_ref[0], be_ref[0]
    at2 = al2 * jnp.exp(cum - lw)
    rt2 = r2 * e_in
    bt2 = be2 * e_out
    kt2 = k2 * e_out
    bh2 = be2 * e_end
    kh2 = k2 * e_end
    wl2 = jnp.exp(cl)

    row = lax.broadcasted_iota(jnp.int32, (sub, sub), 0)
    col = lax.broadcasted_iota(jnp.int32, (sub, sub), 1)
    same = (row >> 6) == (col >> 6)
    m_strict = same & (row > col)
    m_incl = same & (row >= col)
    eye = (row == col).astype(F32)
    m_blk8 = (row >> 3) == (col >> 3)
    m_lvls = [((row >> (sh + 1)) == (col >> (sh + 1))) & ((row >> sh) != (col >> sh)) for sh in (3, 4, 5)]
    wl_ref[0] = jnp.concatenate([wl2[c * CHUNK:c * CHUNK + 1, :] for c in range(nch)], axis=0)

    def stack(a):
        return jnp.stack([a[sb * sub:(sb + 1) * sub, hh * hd:(hh + 1) * hd]
                          for hh in range(2) for sb in range(nsb)])

    at, rt, vb = stack(at2), stack(rt2), stack(v2).astype(BF16)
    lhs = jnp.concatenate([at, rt], axis=1).astype(BF16)
    rhs = jnp.concatenate([stack(bt2), stack(kt2)], axis=1).astype(BF16)
    gm = _bdot(lhs, rhs, BNT)
    n_ab = jnp.where(m_strict[None], gm[:, :sub, :sub], 0.0)
    a_ak = jnp.where(m_strict[None], gm[:, :sub, sub:], 0.0).astype(BF16)
    a_rb = jnp.where(m_incl[None], gm[:, sub:, :sub], 0.0).astype(BF16)
    a_rk = jnp.where(m_incl[None], gm[:, sub:, sub:], 0.0).astype(BF16)
    n8 = jnp.where(m_blk8[None], n_ab, 0.0)
    n8b = n8.astype(BF16)
    n_2 = _bdot(n8b, n8b, BNN)
    n2b = n_2.astype(BF16)
    n_3 = _bdot(n2b, n8b, BNN)
    n_4 = _bdot(n2b, n2b, BNN)
    tinv = eye[None] + n8 + n_2 + n_3
    tinv = tinv + _bdot(tinv.astype(BF16), n_4.astype(BF16), BNN)
    for m_lvl in m_lvls:
        nl = jnp.where(m_lvl[None], n_ab, 0.0).astype(BF16)
        tb = tinv.astype(BF16)
        tinv = tinv + _bdot(_bdot(tb, nl, BNN).astype(BF16), tb, BNN)
    tb = tinv.astype(BF16)
    akv = _bdot(a_ak, vb, BNN)
    pmat = _bdot(tb, jnp.concatenate([at, akv], axis=2).astype(BF16), BNN)
    qmat = _bdot(a_rb, pmat.astype(BF16), BNN) + jnp.concatenate([rt, _bdot(a_rk, vb, BNN)], axis=2)
    pb = pmat.astype(BF16)
    bhb = stack(bh2).astype(BF16)
    khb = stack(kh2).astype(BF16)
    for hh in range(2):
        for sb in range(nsb):
            bi = hh * nsb + sb
            q_ref[0, hh, sb * sub:(sb + 1) * sub, :] = qmat[bi]
            for cc in range(sub // CHUNK):
                cs = slice(cc * CHUNK, (cc + 1) * CHUNK)
                mn = _dot(pb[bi, cs], bhb[bi, cs], TN_DIMS)
                n_c = mn[hd:] + _dot(vb[bi, cs], khb[bi, cs], TN_DIMS)
                en_ref[0, hh, sb * (sub // CHUNK) + cc] = jnp.concatenate([mn[:hd], n_c], axis=0)


def _rwkv_state_kernel(q_ref, en_ref, wl_ref, r_ref, k_ref, v_ref, g_ref, rk_ref, gng_ref, gnb_ref,
                       o_ref, st_ref, *, rows):
    t = pl.program_id(2)
    hd = HEAD_DIM
    nch = rows // CHUNK

    @pl.when(t == 0)
    def _():
        st_ref[...] = jnp.zeros_like(st_ref)

    wl = wl_ref[0]
    r2, k2, v2, g2 = r_ref[0], k_ref[0], v_ref[0], g_ref[0]
    sts = [st_ref[0], st_ref[1]]
    ys = [[], []]
    for c in range(nch):
        for hh in range(2):
            qc = q_ref[0, hh, c * CHUNK:(c + 1) * CHUNK, :]
            en = en_ref[0, hh, c]
            stb = sts[hh].astype(BF16)
            ys[hh].append(_dot(qc[:, :hd].astype(BF16), stb, NT_DIMS) + qc[:, hd:])
            sts[hh] = (sts[hh] * wl[c:c + 1, hh * hd:(hh + 1) * hd]
                       + _dot(stb, en[:hd].astype(BF16)) + en[hd:])
    outs = []
    for hh in range(2):
        sl = slice(hh * hd, (hh + 1) * hd)
        st_ref[hh] = sts[hh]
        v, r, kp = v2[:, sl], r2[:, sl], k2[:, sl]
        y = jnp.concatenate(ys[hh], axis=0)
        mu = jnp.mean(y, axis=-1, keepdims=True)
        var = jnp.mean(jnp.square(y - mu), axis=-1, keepdims=True)
        yn = (y - mu) * lax.rsqrt(var + RWKV_GN_EPS) * gng_ref[:, sl] + gnb_ref[:, sl]
        bonus = jnp.sum(r * kp * rk_ref[0, hh:hh + 1, :], axis=-1, keepdims=True) * v
        outs.append((yn + bonus) * g2[:, sl])
    o_ref[0] = jnp.concatenate(outs, axis=1).astype(o_ref.dtype)


def rwkv7(pm, mu, w_up, a_up, g_up, w0, a0, kk_gain, ka_gain, r_k, gn_g, gn_b, ts=256, rows=512, sub=256):
    b, s, _ = pm.shape
    d = D_BRANCH
    seg = np.kron(np.eye(d // HEAD_DIM, dtype=np.float32), np.ones((HEAD_DIM, HEAD_DIM), np.float32))
    row1 = lambda a: a.reshape(1, -1)
    cst = lambda shape: pl.BlockSpec(shape, lambda bi, i: (0,) * len(shape))
    blk = pl.BlockSpec((1, ts, d), lambda bi, i: (bi, i, 0))
    r, k, v, al, be, lw, g = pl.pallas_call(
        _rwkv_prep_kernel,
        grid=(b, s // ts),
        in_specs=[pl.BlockSpec((1, ts, B_BLOCK), lambda bi, i: (bi, i, 0)),
                  pl.BlockSpec((1, 8, B_BLOCK), lambda bi, i: (bi, jnp.maximum(i * (ts // 8) - 1, 0), 0)),
                  cst((1, B_WIDTH)), cst((64, d)), cst((64, d)), cst((128, d)),
                  cst((1, d)), cst((1, d)), cst((1, d)), cst((1, d)), cst((d, d))],
        out_specs=[blk] * 7,
        out_shape=[jax.ShapeDtypeStruct((b, s, d), F32)] * 7,
        compiler_params=_cp(("parallel", "arbitrary")),
        name="rwkv_prep",
    )(pm, pm, row1(mu), w_up, a_up, g_up, row1(w0), row1(a0), row1(kk_gain), row1(ka_gain), jnp.asarray(seg))

    hp = pl.BlockSpec((1, rows, LANE), lambda bi, h, t: (bi, t, h))
    par = pl.BlockSpec((1, LANE), lambda bi, h, t: (0, h))
    nh = d // HEAD_DIM
    nch = rows // CHUNK
    q_spec = pl.BlockSpec((1, 2, rows, LANE), lambda bi, h, t: (bi, h, t, 0))
    en_spec = pl.BlockSpec((1, 2, nch, LANE, HEAD_DIM), lambda bi, h, t: (bi, h, t, 0, 0))
    wl_spec = pl.BlockSpec((1, nch, LANE), lambda bi, h, t: (bi, t, h))
    grid = (b, d // LANE, s // rows)
    qm, en, wl = pl.pallas_call(
        functools.partial(_rwkv_chunk_kernel, rows=rows, sub=sub),
        grid=grid,
        in_specs=[hp] * 6,
        out_specs=[q_spec, en_spec, wl_spec],
        out_shape=[jax.ShapeDtypeStruct((b, nh, s, LANE), F32),
                   jax.ShapeDtypeStruct((b, nh, s // CHUNK, LANE, HEAD_DIM), F32),
                   jax.ShapeDtypeStruct((b, s // CHUNK, d), F32)],
        compiler_params=_cp(("parallel", "parallel", "parallel")),
        name="rwkv_chunk",
    )(r, k, v, al, be, lw)
    return pl.pallas_call(
        functools.partial(_rwkv_state_kernel, rows=rows),
        grid=grid,
        in_specs=[q_spec, en_spec, wl_spec, hp, hp, hp, hp,
                  pl.BlockSpec((1, 2, HEAD_DIM), lambda bi, h, t: (h, 0, 0)), par, par],
        out_specs=hp,
        out_shape=jax.ShapeDtypeStruct((b, s, d), BF16),
        scratch_shapes=[pltpu.VMEM((2, HEAD_DIM, HEAD_DIM), F32)],
        compiler_params=_cp(("parallel", "parallel", "arbitrary")),
        name="rwkv_state",
    )(qm, en, wl, r, k, v, g, r_k.reshape(-1, 2, HEAD_DIM), row1(gn_g), row1(gn_b))


def _hgrn_kernel(q_ref, f_ref, i_ref, g_ref, lb_ref, ng_ref, o_ref, st_ref, *, rows):
    t = pl.program_id(2)
    sub = 16

    @pl.when(t == 0)
    def _():
        st_ref[...] = jnp.zeros_like(st_ref)

    q = jax.nn.silu(q_ref[0])
    lb = lb_ref[...]
    fg = lb + (1.0 - lb) * jax.nn.sigmoid(f_ref[0])
    lf = jnp.log(fg)
    kf = 1.0 - fg
    iv = i_ref[0]
    rowc = lax.broadcasted_iota(jnp.int32, lf.shape, 0) & (CHUNK - 1)
    bcum = lf
    for dd in (1, 2, 4, 8, 16, 32):
        bcum = bcum + jnp.where(rowc >= dd, pltpu.roll(bcum, dd, axis=0), 0.0)
    ivb = iv.astype(BF16)
    rsub = lax.broadcasted_iota(jnp.int32, (sub, LANE), 0)
    st = st_ref[...]
    outs = []
    for c in range(rows // CHUNK):
        c0 = c * CHUNK
        bc = bcum[c0:c0 + CHUNK]
        qc = q[c0:c0 + CHUNK]
        kc = kf[c0:c0 + CHUNK]
        ic = iv[c0:c0 + CHUNK]
        icb = ivb[c0:c0 + CHUNK]
        o_inter = _dot((qc * jnp.exp(bc)).astype(BF16), st.astype(BF16), NT_DIMS)
        blocks = []
        for ib in range(CHUNK // sub):
            r0 = ib * sub
            bi = bc[r0:r0 + sub]
            qi = qc[r0:r0 + sub]
            ki = kc[r0:r0 + sub]
            ii = ic[r0:r0 + sub]
            acc = jnp.zeros((sub, LANE), F32)
            for s_ in range(sub):
                e = jnp.where(rsub >= s_, jnp.exp(jnp.minimum(bi - bi[s_:s_ + 1], 0.0)), 0.0)
                a = jnp.sum(qi * (ki[s_:s_ + 1] * e), axis=-1, keepdims=True)
                acc = acc + a * ii[s_:s_ + 1]
            if ib > 0:
                ref_row = bc[r0 - 1:r0]
                qt = (qi * jnp.exp(bi - ref_row)).astype(BF16)
                kt = (kc[0:r0] * jnp.exp(ref_row - bc[0:r0])).astype(BF16)
                a_off = _dot(qt, kt, NT_DIMS).astype(BF16)
                acc = acc + _dot(a_off, icb[0:r0])
            blocks.append(acc)
        outs.append(o_inter + jnp.concatenate(blocks, axis=0))
        bl = bc[CHUNK - 1:CHUNK]
        kdec = (kc * jnp.exp(bl - bc)).astype(BF16)
        st = st * jnp.exp(bl) + _dot(icb, kdec, TN_DIMS)
    st_ref[...] = st
    o = jnp.concatenate(outs, axis=0)
    ms = jnp.mean(o * o, axis=-1, keepdims=True)
    o = o * lax.rsqrt(ms + EPS) * ng_ref[...]
    o_ref[0] = (o * jax.nn.silu(g_ref[0])).astype(o_ref.dtype)


def hgrn2(pm, lb, norm_g, rows=256):
    b, s, _ = pm.shape
    heads = D_BRANCH // LANE
    spec = lambda u: pl.BlockSpec((1, rows, LANE), lambda bi, h, t: (bi, t, u + h))
    par = pl.BlockSpec((1, LANE), lambda bi, h, t: (0, h))
    return pl.pallas_call(
        functools.partial(_hgrn_kernel, rows=rows),
        grid=(b, heads, s // rows),
        in_specs=[spec(U_CQ), spec(U_CF), spec(U_CI), spec(U_CG), par, par],
        out_specs=spec(0),
        out_shape=jax.ShapeDtypeStruct((b, s, D_BRANCH), BF16),
        scratch_shapes=[pltpu.VMEM((LANE, LANE), F32)],
        compiler_params=_cp(("parallel", "parallel", "arbitrary")),
        name="hgrn2",
    )(pm, pm, pm, pm, lb.reshape(1, -1), norm_g.reshape(1, -1))


def _dsa_prep_kernel(lo_ref, hi_ref, cos_ref, sin_ref,
                     qo_ref, iqo_ref, ko_ref, vo_ref, iko_ref, wo_ref, *, scale, wscale):
    cos = cos_ref[0]
    sin = sin_ref[0]
    hd = HEAD_DIM
    lo = lo_ref[0]
    hi = hi_ref[0]
    col = lambda u: (u - U_D_LO) * LANE
    q = (_rope(lo[:, col(U_DQ):col(U_DKV)], cos, sin) * scale).astype(BF16)
    iq_in = jnp.concatenate([lo[:, col(U_DIQ):], hi[:, :(U_DIKW - U_D_HI) * LANE]], axis=1)
    iq = _rope(iq_in, cos, sin).astype(BF16)
    for h in range(DSA_HEADS):
        qo_ref[0, h] = q[:, h * hd:(h + 1) * hd]
        iqo_ref[0, h] = iq[:, h * hd:(h + 1) * hd]
    kv = lo[:, col(U_DKV):col(U_DIQ)]
    ko_ref[0] = _rope(kv, cos, sin)[:, :hd].astype(BF16)
    vo_ref[0] = kv[:, hd:].astype(BF16)
    ikw = hi[:, (U_DIKW - U_D_HI) * LANE:]
    iko_ref[0] = _rope(ikw, cos, sin)[:, :hd].astype(BF16)
    wo_ref[0] = ikw[:, hd:hd + IDX_HEADS] * wscale


def _to_key(x):
    x = jnp.where(x == 0.0, 0.0, x)
    bits = pltpu.bitcast(x, jnp.int32)
    return jnp.where(bits < 0, bits ^ jnp.int32(0x7FFFFFFF), bits)


def _dsa_index_kernel(iq_ref, ik_ref, wt_ref, mask_ref, key_ref, *, tq, tk, top_k):
    i = pl.program_id(1)
    nk = key_ref.shape[0]
    nvis = (i * tq + tq - 1) // tk + 1
    krow = lax.broadcasted_iota(jnp.int32, (tk, tq), 0)
    qcol = i * tq + lax.broadcasted_iota(jnp.int32, (tk, tq), 1)
    wt = wt_ref[0]

    def score_block(jb):
        ikb = ik_ref[0, pl.ds(pl.multiple_of(jb * tk, tk), tk), :]
        s_all = _dot(ikb, iq_ref[0].reshape(IDX_HEADS * tq, HEAD_DIM), NT_DIMS)
        acc = jnp.zeros((tk, tq), F32)
        for h in range(IDX_HEADS):
            acc = acc + jnp.maximum(s_all[:, h * tq:(h + 1) * tq], 0.0) * wt[h:h + 1, :]
        vis = ((jb * tk + krow) >> 6) <= (qcol >> 6)
        key_ref[jb] = jnp.where(vis, _to_key(acc), INT_MIN)

    _paired_loop(nvis, score_block)

    def count(pred_fn):
        def blk(jb, cnt):
            ind = jnp.where(pred_fn(key_ref[jb], jb), 1, 0)
            return cnt + jnp.sum(ind.reshape(tk // 8, 8, tq), axis=0)
        cnt = lax.fori_loop(0, nvis, blk, jnp.zeros((8, tq), jnp.int32))
        return jnp.sum(cnt, axis=0, keepdims=True)

    def bit_step(bi, prefix):
        cand = prefix | lax.shift_left(jnp.int32(1), 31 - bi)
        cand_s = cand ^ jnp.int32(INT_MIN)
        c = count(lambda kb, jb: kb >= cand_s)
        return jnp.where(c >= top_k, cand, prefix)

    prefix = lax.fori_loop(0, 32, bit_step, jnp.zeros((1, tq), jnp.int32))
    tau = prefix ^ jnp.int32(INT_MIN)
    n_gt = count(lambda kb, jb: kb > tau)
    n_eq = count(lambda kb, jb: kb == tau)
    need = top_k - n_gt
    tied = (n_eq > need) & (tau != INT_MIN)
    n_cols = nk * tk

    idx_bits = int(n_cols).bit_length()

    def tie_break():
        def idx_step(bi, pre):
            cand = pre | lax.shift_left(jnp.int32(1), idx_bits - 1 - bi)
            c = count(lambda kb, jb: (kb == tau) & ((jb * tk + krow) < cand))
            return jnp.where(c < need, cand, pre)
        cut = lax.fori_loop(0, idx_bits, idx_step, jnp.zeros((1, tq), jnp.int32))
        return jnp.where(tied, cut, n_cols)

    cut = lax.cond(jnp.max(tied.astype(jnp.int32)) > 0, tie_break,
                   lambda: jnp.full((1, tq), n_cols, jnp.int32))

    def write_block(jb, carry):
        kb = key_ref[jb]
        sel = (kb > tau) | ((kb == tau) & ((jb * tk + krow) <= cut))
        sel = sel & (kb != INT_MIN)
        mask_ref[0, jb] = jnp.where(sel, 1.0, 0.0).T.astype(mask_ref.dtype)
        return carry

    def zero_block(jb, carry):
        mask_ref[0, jb] = jnp.zeros((tq, tk), mask_ref.dtype)
        return carry

    lax.fori_loop(0, nvis, write_block, 0)
    lax.fori_loop(nvis, nk, zero_block, 0)


def _dsa_attn_kernel(q_ref, k_ref, v_ref, mask_ref, o_ref, s_ref, m_ref, l_ref, acc_ref, *, tq, tk):
    i = pl.program_id(1)
    nh = DSA_HEADS
    hd = HEAD_DIM
    nvis = (i * tq + tq - 1) // tk + 1
    nchunk = tk // LANE

    m_ref[...] = jnp.full_like(m_ref, NEG)

    def max_step(j):
        kb = k_ref[0, pl.ds(pl.multiple_of(j * tk, tk), tk), :]
        s = _dot(q_ref[0].reshape(nh * tq, hd), kb, NT_DIMS).reshape(nh, tq, tk)
        s = jnp.where((mask_ref[0, j] > 0)[None], s, NEG)
        s_ref[j] = s
        m_ref[...] = jnp.maximum(m_ref[...], _lane_fold(s, jnp.maximum))

    _paired_loop(nvis, max_step)
    m_ref[...] = jnp.broadcast_to(jnp.max(m_ref[...], axis=2, keepdims=True), m_ref.shape)
    l_ref[...] = jnp.zeros_like(l_ref)
    acc_ref[...] = jnp.zeros_like(acc_ref)

    def acc_step(j):
        s = s_ref[j]
        mb = m_ref[...]
        ps = [jnp.exp2(s[:, :, c * LANE:(c + 1) * LANE] - mb) for c in range(nchunk)]
        l_ref[...] += functools.reduce(jnp.add, ps)
        p = jnp.concatenate(ps, axis=2).astype(BF16).reshape(nh * tq, tk)
        vb = v_ref[0, pl.ds(pl.multiple_of(j * tk, tk), tk), :]
        acc_ref[...] += _dot(p, vb).reshape(nh, tq, hd)

    _paired_loop(nvis, acc_step)
    o = acc_ref[...] / jnp.sum(l_ref[...], axis=2, keepdims=True)
    o_ref[0] = jnp.concatenate([o[h] for h in range(nh)], axis=1).astype(o_ref.dtype)


def dsa_attention(pm, cos, sin, ts=512, tq=128, tk=512):
    b, s, _ = pm.shape
    top_k = min(TOPK_MAX, s // 4)
    hd = HEAD_DIM
    nk = s // tk
    tab = pl.BlockSpec((1, ts, LANE), lambda bi, i: (bi, i, 0))
    heads_out = pl.BlockSpec((1, DSA_HEADS, ts, hd), lambda bi, i: (bi, 0, i, 0))
    narrow = lambda w: pl.BlockSpec((1, ts, w), lambda bi, i: (bi, i, 0))
    w_lo = (U_D_HI - U_D_LO) * LANE
    w_hi = N_MAIN - U_D_HI * LANE
    q, iq, k, v, ik, w = pl.pallas_call(
        functools.partial(_dsa_prep_kernel, scale=hd ** -0.5 * LOG2E, wscale=(IDX_HEADS ** -0.5) * (hd ** -0.5)),
        grid=(b, s // ts),
        in_specs=[pl.BlockSpec((1, ts, w_lo), lambda bi, i: (bi, i, U_D_LO * LANE // w_lo)),
                  pl.BlockSpec((1, ts, w_hi), lambda bi, i: (bi, i, U_D_HI * LANE // w_hi)),
                  tab, tab],
        out_specs=[heads_out, heads_out, narrow(hd), narrow(hd), narrow(hd), narrow(IDX_HEADS)],
        out_shape=[jax.ShapeDtypeStruct((b, DSA_HEADS, s, hd), BF16)] * 2
                  + [jax.ShapeDtypeStruct((b, s, hd), BF16)] * 3
                  + [jax.ShapeDtypeStruct((b, s, IDX_HEADS), F32)],
        compiler_params=_cp(("parallel", "parallel")),
        name="dsa_prep",
    )(pm, pm, cos, sin)

    mask_spec = pl.BlockSpec((1, nk, tq, tk), lambda bi, i: (bi, 0, i, 0))
    whole = pl.BlockSpec((1, s, hd), lambda bi, i: (bi, 0, 0))
    mask = pl.pallas_call(
        functools.partial(_dsa_index_kernel, tq=tq, tk=tk, top_k=top_k),
        grid=(b, s // tq),
        in_specs=[pl.BlockSpec((1, IDX_HEADS, tq, hd), lambda bi, i: (bi, 0, i, 0)),
                  whole,
                  pl.BlockSpec((1, IDX_HEADS, tq), lambda bi, i: (bi, 0, i))],
        out_specs=mask_spec,
        out_shape=jax.ShapeDtypeStruct((b, nk, s, tk), BF16),
        scratch_shapes=[pltpu.VMEM((nk, tk, tq), jnp.int32)],
        compiler_params=_cp(("parallel", "parallel")),
        name="dsa_index",
    )(iq, ik, jnp.swapaxes(w, 1, 2))

    return pl.pallas_call(
        functools.partial(_dsa_attn_kernel, tq=tq, tk=tk),
        grid=(b, s // tq),
        in_specs=[pl.BlockSpec((1, DSA_HEADS, tq, hd), lambda bi, i: (bi, 0, i, 0)),
                  whole, whole, mask_spec],
        out_specs=pl.BlockSpec((1, tq, D_BRANCH), lambda bi, i: (bi, i, 0)),
        out_shape=jax.ShapeDtypeStruct((b, s, D_BRANCH), BF16),
        scratch_shapes=[pltpu.VMEM((nk, DSA_HEADS, tq, tk), F32),
                        pltpu.VMEM((DSA_HEADS, tq, LANE), F32), pltpu.VMEM((DSA_HEADS, tq, LANE), F32),
                        pltpu.VMEM((DSA_HEADS, tq, hd), F32)],
        compiler_params=_cp(("parallel", "parallel")),
        name="dsa_attention",
    )(q, k, v, mask)


GATE_COL = 6600


def kernel(x, positions, norm_g, w_in, diff_lambda, diff_subln_g, rwkv_mu, rwkv_w_up, rwkv_a_up, rwkv_g_up, rwkv_w0, rwkv_a0, rwkv_k_k, rwkv_k_a, rwkv_r_k, rwkv_gn_g, rwkv_gn_b, hgrn_lb_logits, hgrn_norm_g, w_branch, w_out, mlp_w1, mlp_w2):
    b, s, d = x.shape
    m = b * s
    depth = w_in.shape[0]
    lb = jax.nn.softmax(hgrn_lb_logits.astype(F32), axis=0)
    lb = jnp.cumsum(lb, axis=0) - lb[0]
    w_gate = w_in[:, :, GATE_COL:GATE_COL + N_BRANCH * D_MODEL].astype(BF16)
    w_out_b = w_out.astype(BF16)
    w2_b = mlp_w2.astype(BF16)
    cos, sin = rope_tables(positions)
    cos3 = cos.reshape(b, s, LANE)
    sin3 = sin.reshape(b, s, LANE)
    x2 = x.reshape(m, d)
    h = rmsnorm_bf16(x2, norm_g[0, 0])
    for l in range(depth):
        pm = matmul(h, w_in, l, None, F32, n=N_MAIN, name="proj_main").reshape(b, s, N_MAIN)
        gate = matmul(h, w_gate, l, "sigmoid", BF16, name="proj_gate")
        y_a = diff_attention(pm, cos3, sin3, diff_lambda[l], diff_subln_g[l], l)
        y_b = rwkv7(pm, rwkv_mu[l], rwkv_w_up[l], rwkv_a_up[l], rwkv_g_up[l], rwkv_w0[l], rwkv_a0[l],
                    rwkv_k_k[l], rwkv_k_a[l], rwkv_r_k[l], rwkv_gn_g[l], rwkv_gn_b[l])
        y_c = hgrn2(pm, lb[l], hgrn_norm_g[l])
        y_d = dsa_attention(pm, cos3, sin3)
        ys = [y.reshape(m, D_BRANCH) for y in (y_a, y_b, y_c, y_d)]
        merged = gated_merge(ys, w_branch, l, gate)
        x2, h2 = matmul_norm_residual(merged, w_out_b, l, x2, norm_g[l, 1], norm_g[l, 2], name="out_proj")
        ff = matmul(h2, mlp_w1, l, "relu2", BF16, name="mlp_up")
        if l + 1 < depth:
            x2, h = matmul_norm_residual(ff, w2_b, l, x2, norm_g[l, 3], norm_g[l + 1, 0], name="mlp_down")
        else:
            x2 = matmul_norm_residual(ff, w2_b, l, x2, norm_g[l, 3], name="mlp_down")
    return x2.reshape(b, s, d)
```

```python
import functools
import math

import numpy as np
import jax
import jax.numpy as jnp
from jax import lax
from jax.experimental import pallas as pl
from jax.experimental.pallas import tpu as pltpu

F32 = jnp.float32
BF16 = jnp.bfloat16

D_MODEL = 2048
D_BRANCH = 512
D_FF = 8192
N_BRANCH = 4
CHUNK = 64
EPS = 1e-6
ROPE_THETA = 10000.0
HEAD_DIM = 64
DIFF_HEADS = 4
RWKV_GN_EPS = 64e-5
DSA_HEADS = 8
IDX_HEADS = 8
TOPK_MAX = 256

LANE = 128
VMEM_LIMIT = 56 * 1024 * 1024

U_AQ, U_AK, U_AV = 0, 4, 8
U_B = 12
U_CQ, U_CF, U_CI, U_CG = 26, 30, 34, 38
U_DQ, U_DKV, U_DIQ, U_DIKW = 42, 46, 47, 51
U_D_LO, U_D_HI = 40, 48
N_MAIN = 52 * LANE
B_WIDTH = 1792
B_BLOCK = 3584

NEG = -1e30
INT_MIN = -2147483648
LOG2E = 1.4426950408889634

NT_DIMS = (((1,), (1,)), ((), ()))
TN_DIMS = (((0,), (0,)), ((), ()))


def _cp(sem, vmem=VMEM_LIMIT):
    return pltpu.CompilerParams(dimension_semantics=sem, vmem_limit_bytes=vmem)


def _dot(a, b, dims=None):
    if dims is None:
        return jnp.dot(a, b, preferred_element_type=F32)
    return lax.dot_general(a, b, dims, preferred_element_type=F32)


def _dot_hi(a, b, dims=(((1,), (0,)), ((), ()))):
    return lax.dot_general(a, b, dims, preferred_element_type=F32,
                           precision=lax.Precision.HIGHEST)


def _rmsnorm_kernel(x_ref, g_ref, o_ref):
    x = x_ref[...]
    ms = jnp.mean(x * x, axis=-1, keepdims=True)
    o_ref[...] = (x * lax.rsqrt(ms + EPS) * g_ref[...]).astype(o_ref.dtype)


def rmsnorm_bf16(x, g, tm=512):
    m, d = x.shape
    return pl.pallas_call(
        _rmsnorm_kernel,
        grid=(m // tm,),
        in_specs=[pl.BlockSpec((tm, d), lambda i: (i, 0)),
                  pl.BlockSpec((1, d), lambda i: (0, 0))],
        out_specs=pl.BlockSpec((tm, d), lambda i: (i, 0)),
        out_shape=jax.ShapeDtypeStruct((m, d), BF16),
        compiler_params=_cp(("parallel",)),
        name="rmsnorm",
    )(x, g.reshape(1, d))


def _mm_kernel(x_ref, w_ref, o_ref, *, act, dims):
    w = w_ref[...] if len(w_ref.shape) == 2 else w_ref[0]
    acc = _dot(x_ref[...], w.astype(BF16), dims)
    if act == "sigmoid":
        acc = 0.5 * jnp.tanh(0.5 * acc) + 0.5
    elif act == "relu2":
        acc = jnp.square(jnp.maximum(acc, 0.0))
    o_ref[...] = acc.astype(o_ref.dtype)


def matmul(x, w, layer, act, out_dtype, tm=2048, tn=512, name="mm"):
    m, k = x.shape
    n = w.shape[2]
    tm = min(tm, m)
    return pl.pallas_call(
        functools.partial(_mm_kernel, act=act, dims=None),
        grid=(m // tm, n // tn),
        in_specs=[pl.BlockSpec((tm, k), lambda i, j: (i, 0)),
                  pl.BlockSpec((None, k, tn), lambda i, j: (layer, 0, j))],
        out_specs=pl.BlockSpec((tm, tn), lambda i, j: (i, j)),
        out_shape=jax.ShapeDtypeStruct((m, n), out_dtype),
        compiler_params=_cp(("parallel", "parallel")),
        name=name,
    )(x, w)


def matmul_wt(x, wt, layer, row0, n, act, out_dtype, tm=2048, tn=512, name="mm_wt"):
    m, k = x.shape
    tm = min(tm, m)
    return pl.pallas_call(
        functools.partial(_mm_kernel, act=act, dims=NT_DIMS),
        grid=(m // tm, n // tn),
        in_specs=[pl.BlockSpec((tm, k), lambda i, j: (i, 0)),
                  pl.BlockSpec((pl.Element(1), pl.Element(tn), pl.Element(k)),
                               lambda i, j: (layer, pl.multiple_of(row0 + j * tn, 8), 0))],
        out_specs=pl.BlockSpec((tm, tn), lambda i, j: (i, j)),
        out_shape=jax.ShapeDtypeStruct((m, n), out_dtype),
        compiler_params=_cp(("parallel", "parallel")),
        name=name,
    )(x, wt)


def _norm_residual(y, x_ref, g_ref, o_ref, h_ref):
    ms = jnp.mean(y * y, axis=-1, keepdims=True)
    o = x_ref[...] + y * lax.rsqrt(ms + EPS) * g_ref[0:1, :]
    o_ref[...] = o
    if h_ref is not None:
        ms = jnp.mean(o * o, axis=-1, keepdims=True)
        h_ref[...] = (o * lax.rsqrt(ms + EPS) * g_ref[1:2, :]).astype(h_ref.dtype)


def _mm_norm_res_kernel(a_ref, w_ref, x_ref, g_ref, o_ref, *rest):
    h_ref, acc_ref = (rest[0], rest[1]) if len(rest) == 2 else (None, rest[0])
    kk = pl.program_id(1)

    @pl.when(kk == 0)
    def _():
        acc_ref[...] = jnp.zeros_like(acc_ref)

    acc_ref[...] += _dot(a_ref[...], w_ref[...])

    @pl.when(kk == pl.num_programs(1) - 1)
    def _():
        _norm_residual(acc_ref[...], x_ref, g_ref, o_ref, h_ref)


def _mm_norm_res_fullk_kernel(a_ref, w_ref, x_ref, g_ref, o_ref, *rest):
    _norm_residual(_dot(a_ref[...], w_ref[...]), x_ref, g_ref, o_ref, rest[0] if rest else None)


def matmul_norm_residual(a, w, layer, x, g, g_next=None, tm=512, tk=2048, name="mm_norm_res"):
    m, k = a.shape
    n = w.shape[2]
    gs = jnp.stack([g, g if g_next is None else g_next])
    out_shape = [jax.ShapeDtypeStruct((m, n), F32)]
    if g_next is not None:
        out_shape.append(jax.ShapeDtypeStruct((m, n), BF16))
    if k <= tk:
        row = lambda i: (i, 0)
        outs = pl.pallas_call(
            _mm_norm_res_fullk_kernel,
            grid=(m // tm,),
            in_specs=[pl.BlockSpec((tm, k), row), pl.BlockSpec((None, k, n), lambda i: (layer, 0, 0)),
                      pl.BlockSpec((tm, n), row), pl.BlockSpec((2, n), lambda i: (0, 0))],
            out_specs=[pl.BlockSpec((tm, n), row)] * len(out_shape),
            out_shape=out_shape,
            compiler_params=_cp(("parallel",)),
            name=name,
        )(a, w, x, gs)
    else:
        row = lambda i, kk: (i, 0)
        outs = pl.pallas_call(
            _mm_norm_res_kernel,
            grid=(m // tm, k // tk),
            in_specs=[pl.BlockSpec((tm, tk), lambda i, kk: (i, kk)),
                      pl.BlockSpec((None, tk, n), lambda i, kk: (layer, kk, 0)),
                      pl.BlockSpec((tm, n), row),
                      pl.BlockSpec((2, n), lambda i, kk: (0, 0))],
            out_specs=[pl.BlockSpec((tm, n), row)] * len(out_shape),
            out_shape=out_shape,
            scratch_shapes=[pltpu.VMEM((tm, n), F32)],
            compiler_params=_cp(("parallel", "arbitrary")),
            name=name,
        )(a, w, x, gs)
    return outs if g_next is not None else outs[0]


def _merge_kernel(ya, yb, yc, yd, wb, ga, gb, gc, gd, o_ref):
    acc = None
    for n, (y, g) in enumerate(((ya, ga), (yb, gb), (yc, gc), (yd, gd))):
        t = g[...].astype(F32) * _dot(y[...], wb[n].astype(BF16))
        acc = t if acc is None else acc + t
    o_ref[...] = acc.astype(o_ref.dtype)


def gated_merge(ys, wb, layer, gate, tm=512, tn=1024):
    m = ys[0].shape[0]
    nj = D_MODEL // tn
    y_spec = pl.BlockSpec((tm, D_BRANCH), lambda j, i: (i, 0))
    g_specs = [pl.BlockSpec((tm, tn), functools.partial(lambda j, i, n: (i, n * nj + j), n=n))
               for n in range(N_BRANCH)]
    return pl.pallas_call(
        _merge_kernel,
        grid=(nj, m // tm),
        in_specs=[y_spec] * 4 + [pl.BlockSpec((None, N_BRANCH, D_BRANCH, tn), lambda j, i: (layer, 0, 0, j))]
                 + g_specs,
        out_specs=pl.BlockSpec((tm, tn), lambda j, i: (i, j)),
        out_shape=jax.ShapeDtypeStruct((m, D_MODEL), BF16),
        compiler_params=_cp(("parallel", "parallel")),
        name="gated_merge",
    )(*ys, wb, gate, gate, gate, gate)


def _rope_table_kernel(pos_ref, inv_ref, sgn_ref, cos_ref, sin_ref):
    ang = pos_ref[...].astype(F32) * inv_ref[...]
    cos_ref[...] = jnp.cos(ang)
    sin_ref[...] = jnp.sin(ang) * sgn_ref[...]


def rope_tables(positions, tm=512):
    m = positions.size
    half = HEAD_DIM // 2
    inv = ROPE_THETA ** (-np.arange(0, HEAD_DIM, 2, dtype=np.float32) / HEAD_DIM)
    inv = np.tile(inv.astype(np.float32), 4).reshape(1, LANE)
    sgn = np.tile(np.concatenate([-np.ones(half, np.float32), np.ones(half, np.float32)]), 2).reshape(1, LANE)
    spec = pl.BlockSpec((tm, LANE), lambda i: (i, 0))
    cst = pl.BlockSpec((1, LANE), lambda i: (0, 0))
    return pl.pallas_call(
        _rope_table_kernel,
        grid=(m // tm,),
        in_specs=[pl.BlockSpec((tm, 1), lambda i: (i, 0)), cst, cst],
        out_specs=[spec, spec],
        out_shape=[jax.ShapeDtypeStruct((m, LANE), F32)] * 2,
        compiler_params=_cp(("parallel",)),
        name="rope_tables",
    )(positions.reshape(m, 1), jnp.asarray(inv), jnp.asarray(sgn))


def _rope(x, cos, sin):
    w = x.shape[1]
    n = w // LANE
    if n > 1:
        cos = jnp.concatenate([cos] * n, axis=1)
        sin = jnp.concatenate([sin] * n, axis=1)
    lane = lax.broadcasted_iota(jnp.int32, x.shape, 1)
    up = pltpu.roll(x, w - HEAD_DIM // 2, axis=1)
    dn = pltpu.roll(x, HEAD_DIM // 2, axis=1)
    rot = jnp.where((lane & (HEAD_DIM // 2)) == 0, up, dn)
    return x * cos + rot * sin


def _rope_qkv_kernel(q_ref, k_ref, v_ref, cos_ref, sin_ref, qo_ref, ko_ref, vo_ref, *, scale):
    cos = cos_ref[0]
    sin = sin_ref[0]
    hd = HEAD_DIM
    q = (_rope(q_ref[0], cos, sin) * scale).astype(qo_ref.dtype)
    k = _rope(k_ref[0], cos, sin).astype(ko_ref.dtype)
    for h in range(q.shape[1] // hd):
        qo_ref[0, h] = q[:, h * hd:(h + 1) * hd]
        ko_ref[0, h] = k[:, h * hd:(h + 1) * hd]
    vo_ref[0] = v_ref[0].astype(vo_ref.dtype)


def rope_qkv(pm, cos, sin, uq, uk, uv, scale, ts=512):
    b, s, _ = pm.shape
    wq = D_BRANCH
    nh = wq // HEAD_DIM
    tab = pl.BlockSpec((1, ts, LANE), lambda bi, i: (bi, i, 0))
    col = lambda u: pl.BlockSpec((1, ts, wq), lambda bi, i: (bi, i, u * LANE // wq))
    heads = pl.BlockSpec((1, nh, ts, HEAD_DIM), lambda bi, i: (bi, 0, i, 0))
    return pl.pallas_call(
        functools.partial(_rope_qkv_kernel, scale=scale),
        grid=(b, s // ts),
        in_specs=[col(uq), col(uk), col(uv), tab, tab],
        out_specs=[heads, heads, pl.BlockSpec((1, ts, wq), lambda bi, i: (bi, i, 0))],
        out_shape=[jax.ShapeDtypeStruct((b, nh, s, HEAD_DIM), BF16)] * 2
                  + [jax.ShapeDtypeStruct((b, s, wq), BF16)],
        compiler_params=_cp(("parallel", "parallel")),
        name="rope_qkv",
    )(pm, pm, pm, cos, sin)


def _lane_fold(x, op):
    out = x[..., 0:LANE]
    for c in range(1, x.shape[-1] // LANE):
        out = op(out, x[..., c * LANE:(c + 1) * LANE])
    return out


def _paired_loop(n, step):
    def body(jj, carry):
        step(2 * jj)
        step(2 * jj + 1)
        return carry
    lax.fori_loop(0, n // 2, body, 0)

    @pl.when(n % 2 == 1)
    def _():
        step(n - 1)


def _diff_attn_kernel(q_ref, k_ref, v_ref, lam_ref, g_ref, o_ref, s_ref, m_ref, l_ref, acc_ref, *, tq, tk, lam_init):
    i = pl.program_id(2)
    nfull = (i * tq) // tk
    row = lax.broadcasted_iota(jnp.int32, (tq, tk), 0)
    col = lax.broadcasted_iota(jnp.int32, (tq, tk), 1)
    vis = (col >> 6) <= (row >> 6)

    m_ref[...] = jnp.full_like(m_ref, NEG)

    def max_step(j, masked=False):
        for mm in range(2):
            kb = k_ref[0, mm, pl.ds(pl.multiple_of(j * tk, tk), tk), :]
            s = _dot(q_ref[0, mm], kb, NT_DIMS)
            if masked:
                s = jnp.where(vis, s, NEG)
            s_ref[j, mm] = s
            m_ref[mm] = jnp.maximum(m_ref[mm], _lane_fold(s, jnp.maximum))

    _paired_loop(nfull, max_step)
    max_step(nfull, True)
    for mm in range(2):
        m_ref[mm] = jnp.broadcast_to(jnp.max(m_ref[mm], axis=1, keepdims=True), (tq, LANE))
    l_ref[...] = jnp.zeros_like(l_ref)
    acc_ref[...] = jnp.zeros_like(acc_ref)

    def acc_step(j):
        vb = v_ref[0, pl.ds(pl.multiple_of(j * tk, tk), tk), :]
        for mm in range(2):
            s = s_ref[j, mm]
            mb = m_ref[mm]
            ps = [jnp.exp2(s[:, c * LANE:(c + 1) * LANE] - mb) for c in range(tk // LANE)]
            l_ref[mm] += functools.reduce(jnp.add, ps)
            acc_ref[mm] += _dot(jnp.concatenate(ps, axis=1).astype(BF16), vb)

    _paired_loop(nfull + 1, acc_step)

    lv = lam_ref[...]
    lam = (jnp.exp(jnp.sum(lv[0:1] * lv[1:2], keepdims=True))
           - jnp.exp(jnp.sum(lv[2:3] * lv[3:4], keepdims=True)) + lam_init)
    l0 = jnp.sum(l_ref[0], axis=1, keepdims=True)
    l1 = jnp.sum(l_ref[1], axis=1, keepdims=True)
    o = acc_ref[0] / l0 - lam * (acc_ref[1] / l1)
    ms = jnp.mean(o * o, axis=-1, keepdims=True)
    o = o * lax.rsqrt(ms + EPS) * g_ref[...] * (1.0 - lam_init)
    o_ref[0] = o.astype(o_ref.dtype)


def diff_attention(pm, cos, sin, lam_vecs, subln_g, layer, tq=512):
    b, s, _ = pm.shape
    tk = tq
    qa, ka, va = rope_qkv(pm, cos, sin, U_AQ, U_AK, U_AV, HEAD_DIM ** -0.5 * LOG2E)
    lam_init = 0.8 - 0.6 * math.exp(-0.3 * layer)
    return pl.pallas_call(
        functools.partial(_diff_attn_kernel, tq=tq, tk=tk, lam_init=lam_init),
        grid=(b, DIFF_HEADS, s // tq),
        in_specs=[pl.BlockSpec((1, 2, tq, HEAD_DIM), lambda bi, h, i: (bi, h, i, 0)),
                  pl.BlockSpec((1, 2, s, HEAD_DIM), lambda bi, h, i: (bi, h, 0, 0)),
                  pl.BlockSpec((1, s, LANE), lambda bi, h, i: (bi, 0, h)),
                  pl.BlockSpec((4, HEAD_DIM), lambda bi, h, i: (0, 0)),
                  pl.BlockSpec((1, LANE), lambda bi, h, i: (0, 0))],
        out_specs=pl.BlockSpec((1, tq, LANE), lambda bi, h, i: (bi, i, h)),
        out_shape=jax.ShapeDtypeStruct((b, s, D_BRANCH), BF16),
        scratch_shapes=[pltpu.VMEM((s // tk, 2, tq, tk), F32),
                        pltpu.VMEM((2, tq, LANE), F32), pltpu.VMEM((2, tq, LANE), F32),
                        pltpu.VMEM((2, tq, LANE), F32)],
        compiler_params=_cp(("parallel", "parallel", "parallel")),
        name="diff_attention",
    )(qa, ka, va, lam_vecs, subln_g.reshape(1, LANE))


def _softplus(z):
    return jnp.maximum(z, 0.0) + jnp.log(1.0 + jnp.exp(-jnp.abs(z)))


def _rwkv_prep_kernel(x_ref, prev_ref, mu_ref, wup_ref, aup_ref, gup_ref, w0_ref, a0_ref, kkg_ref, kag_ref,
                      seg_ref, r_ref, k_ref, v_ref, al_ref, be_ref, lw_ref, g_ref):
    i = pl.program_id(1)
    c0 = U_B * LANE
    p = x_ref[0, :, c0:c0 + B_WIDTH]
    row = lax.broadcasted_iota(jnp.int32, p.shape, 0)
    last = prev_ref[0, 7:8, c0:c0 + B_WIDTH]
    last = jnp.where(i == 0, jnp.zeros_like(last), last)
    prev = jnp.where(row == 0, jnp.broadcast_to(last, p.shape), pltpu.roll(p, 1, axis=0))
    ps = p + (prev - p) * mu_ref[...]
    d = D_BRANCH
    r, k, v = ps[:, 0:d], ps[:, d:2 * d], ps[:, 2 * d:3 * d]
    wd = ps[:, 3 * d:3 * d + 64]
    ad = ps[:, 3 * d + 64:3 * d + 128]
    gd = ps[:, 3 * d + 128:3 * d + 256]
    w_log = -_softplus(-(w0_ref[...] + _dot_hi(jnp.tanh(wd), wup_ref[...]))) - 0.5
    a = jax.nn.sigmoid(a0_ref[...] + _dot_hi(ad, aup_ref[...]))
    g = _dot_hi(jax.nn.sigmoid(gd), gup_ref[...])
    kk = k * kkg_ref[...]
    nrm = jnp.sqrt(_dot_hi(kk * kk, seg_ref[...]))
    kk = kk / jnp.maximum(nrm, 1e-12)
    r_ref[0] = r
    k_ref[0] = k * (1.0 + (a - 1.0) * kag_ref[...])
    v_ref[0] = v
    al_ref[0] = -kk
    be_ref[0] = kk * a
    lw_ref[0] = -jnp.exp(w_log)
    g_ref[0] = g


def _bdot(a, b, dims):
    return lax.dot_general(a, b, ((dims[0], dims[1]), ((0,), (0,))), preferred_element_type=F32)


BNN = ((2,), (1,))
BNT = ((2,), (2,))


def _rwkv_chunk_kernel(r_ref, k_ref, v_ref, al_ref, be_ref, lw_ref, q_ref, en_ref, wl_ref, *, rows, sub):
    hd = HEAD_DIM
    nch = rows // CHUNK
    nsb = rows // sub
    lw = lw_ref[0]
    rowc = lax.broadcasted_iota(jnp.int32, lw.shape, 0) & (CHUNK - 1)
    cum = lw
    for dd in (1, 2, 4, 8, 16, 32):
        cum = cum + jnp.where(rowc >= dd, pltpu.roll(cum, dd, axis=0), 0.0)
    cl = jnp.concatenate(
        [jnp.broadcast_to(cum[(c + 1) * CHUNK - 1:(c + 1) * CHUNK, :], (CHUNK, LANE)) for c in range(nch)], axis=0)
    e_in = jnp.exp(cum)
    e_out = jnp.exp(-cum)
    e_end = jnp.exp(cl - cum)
    r2, k2, v2, al2, be2 = r_ref[0], k_ref[0], v_ref[0], al_ref[0], be_ref[0]
    at2 = al2 * jnp.exp(cum - lw)
    rt2 = r2 * e_in
    bt2 = be2 * e_out
    kt2 = k2 * e_out
    bh2 = be2 * e_end
    kh2 = k2 * e_end
    wl2 = jnp.exp(cl)

    row = lax.broadcasted_iota(jnp.int32, (sub, sub), 0)
    col = lax.broadcasted_iota(jnp.int32, (sub, sub), 1)
    same = (row >> 6) == (col >> 6)
    m_strict = same & (row > col)
    m_incl = same & (row >= col)
    eye = (row == col).astype(F32)
    m_blk8 = (row >> 3) == (col >> 3)
    m_lvls = [((row >> (sh + 1)) == (col >> (sh + 1))) & ((row >> sh) != (col >> sh)) for sh in (3, 4, 5)]
    wl_ref[0] = jnp.concatenate([wl2[c * CHUNK:c * CHUNK + 1, :] for c in range(nch)], axis=0)

    def stack(a):
        return jnp.stack([a[sb * sub:(sb + 1) * sub, hh * hd:(hh + 1) * hd]
                          for hh in range(2) for sb in range(nsb)])

    at, rt, vb = stack(at2), stack(rt2), stack(v2).astype(BF16)
    lhs = jnp.concatenate([at, rt], axis=1).astype(BF16)
    rhs = jnp.concatenate([stack(bt2), stack(kt2)], axis=1).astype(BF16)
    gm = _bdot(lhs, rhs, BNT)
    n_ab = jnp.where(m_strict[None], gm[:, :sub, :sub], 0.0)
    a_ak = jnp.where(m_strict[None], gm[:, :sub, sub:], 0.0).astype(BF16)
    a_rb = jnp.where(m_incl[None], gm[:, sub:, :sub], 0.0).astype(BF16)
    a_rk = jnp.where(m_incl[None], gm[:, sub:, sub:], 0.0).astype(BF16)
    n8 = jnp.where(m_blk8[None], n_ab, 0.0)
    n8b = n8.astype(BF16)
    n_2 = _bdot(n8b, n8b, BNN)
    n2b = n_2.astype(BF16)
    n_3 = _bdot(n2b, n8b, BNN)
    n_4 = _bdot(n2b, n2b, BNN)
    tinv = eye[None] + n8 + n_2 + n_3
    tinv = tinv + _bdot(tinv.astype(BF16), n_4.astype(BF16), BNN)
    for m_lvl in m_lvls:
        nl = jnp.where(m_lvl[None], n_ab, 0.0).astype(BF16)
        tb = tinv.astype(BF16)
        tinv = tinv + _bdot(_bdot(tb, nl, BNN).astype(BF16), tb, BNN)
    tb = tinv.astype(BF16)
    akv = _bdot(a_ak, vb, BNN)
    pmat = _bdot(tb, jnp.concatenate([at, akv], axis=2).astype(BF16), BNN)
    qmat = _bdot(a_rb, pmat.astype(BF16), BNN) + jnp.concatenate([rt, _bdot(a_rk, vb, BNN)], axis=2)
    pb = pmat.astype(BF16)
    bhb = stack(bh2).astype(BF16)
    khb = stack(kh2).astype(BF16)
    for hh in range(2):
        for sb in range(nsb):
            bi = hh * nsb + sb
            q_ref[0, hh, sb * sub:(sb + 1) * sub, :] = qmat[bi]
            for cc in range(sub // CHUNK):
                cs = slice(cc * CHUNK, (cc + 1) * CHUNK)
                mn = _dot(pb[bi, cs], bhb[bi, cs], TN_DIMS)
                n_c = mn[hd:] + _dot(vb[bi, cs], khb[bi, cs], TN_DIMS)
                en_ref[0, hh, sb * (sub // CHUNK) + cc] = jnp.concatenate([mn[:hd], n_c], axis=0)


def _rwkv_state_kernel(q_ref, en_ref, wl_ref, r_ref, k_ref, v_ref, g_ref, rk_ref, gng_ref, gnb_ref,
                       o_ref, st_ref, *, rows):
    t = pl.program_id(2)
    hd = HEAD_DIM
    nch = rows // CHUNK

    @pl.when(t == 0)
    def _():
        st_ref[...] = jnp.zeros_like(st_ref)

    wl = wl_ref[0]
    r2, k2, v2, g2 = r_ref[0], k_ref[0], v_ref[0], g_ref[0]
    sts = [st_ref[0], st_ref[1]]
    ys = [[], []]
    for c in range(nch):
        for hh in range(2):
            qc = q_ref[0, hh, c * CHUNK:(c + 1) * CHUNK, :]
            en = en_ref[0, hh, c]
            stb = sts[hh].astype(BF16)
            ys[hh].append(_dot(qc[:, :hd].astype(BF16), stb, NT_DIMS) + qc[:, hd:])
            sts[hh] = (sts[hh] * wl[c:c + 1, hh * hd:(hh + 1) * hd]
                       + _dot(stb, en[:hd].astype(BF16)) + en[hd:])
    outs = []
    for hh in range(2):
        sl = slice(hh * hd, (hh + 1) * hd)
        st_ref[hh] = sts[hh]
        v, r, kp = v2[:, sl], r2[:, sl], k2[:, sl]
        y = jnp.concatenate(ys[hh], axis=0)
        mu = jnp.mean(y, axis=-1, keepdims=True)
        var = jnp.mean(jnp.square(y - mu), axis=-1, keepdims=True)
        yn = (y - mu) * lax.rsqrt(var + RWKV_GN_EPS) * gng_ref[:, sl] + gnb_ref[:, sl]
        bonus = jnp.sum(r * kp * rk_ref[0, hh:hh + 1, :], axis=-1, keepdims=True) * v
        outs.append((yn + bonus) * g2[:, sl])
    o_ref[0] = jnp.concatenate(outs, axis=1).astype(o_ref.dtype)


def rwkv7(pm, mu, w_up, a_up, g_up, w0, a0, kk_gain, ka_gain, r_k, gn_g, gn_b, ts=256, rows=512, sub=256):
    b, s, _ = pm.shape
    d = D_BRANCH
    seg = np.kron(np.eye(d // HEAD_DIM, dtype=np.float32), np.ones((HEAD_DIM, HEAD_DIM), np.float32))
    row1 = lambda a: a.reshape(1, -1)
    cst = lambda shape: pl.BlockSpec(shape, lambda bi, i: (0,) * len(shape))
    blk = pl.BlockSpec((1, ts, d), lambda bi, i: (bi, i, 0))
    r, k, v, al, be, lw, g = pl.pallas_call(
        _rwkv_prep_kernel,
        grid=(b, s // ts),
        in_specs=[pl.BlockSpec((1, ts, B_BLOCK), lambda bi, i: (bi, i, 0)),
                  pl.BlockSpec((1, 8, B_BLOCK), lambda bi, i: (bi, jnp.maximum(i * (ts // 8) - 1, 0), 0)),
                  cst((1, B_WIDTH)), cst((64, d)), cst((64, d)), cst((128, d)),
                  cst((1, d)), cst((1, d)), cst((1, d)), cst((1, d)), cst((d, d))],
        out_specs=[blk] * 7,
        out_shape=[jax.ShapeDtypeStruct((b, s, d), F32)] * 7,
        compiler_params=_cp(("parallel", "arbitrary")),
        name="rwkv_prep",
    )(pm, pm, row1(mu), w_up, a_up, g_up, row1(w0), row1(a0), row1(kk_gain), row1(ka_gain), jnp.asarray(seg))

    hp = pl.BlockSpec((1, rows, LANE), lambda bi, h, t: (bi, t, h))
    par = pl.BlockSpec((1, LANE), lambda bi, h, t: (0, h))
    nh = d // HEAD_DIM
    nch = rows // CHUNK
    q_spec = pl.BlockSpec((1, 2, rows, LANE), lambda bi, h, t: (bi, h, t, 0))
    en_spec = pl.BlockSpec((1, 2, nch, LANE, HEAD_DIM), lambda bi, h, t: (bi, h, t, 0, 0))
    wl_spec = pl.BlockSpec((1, nch, LANE), lambda bi, h, t: (bi, t, h))
    grid = (b, d // LANE, s // rows)
    qm, en, wl = pl.pallas_call(
        functools.partial(_rwkv_chunk_kernel, rows=rows, sub=sub),
        grid=grid,
        in_specs=[hp] * 6,
        out_specs=[q_spec, en_spec, wl_spec],
        out_shape=[jax.ShapeDtypeStruct((b, nh, s, LANE), F32),
                   jax.ShapeDtypeStruct((b, nh, s // CHUNK, LANE, HEAD_DIM), F32),
                   jax.ShapeDtypeStruct((b, s // CHUNK, d), F32)],
        compiler_params=_cp(("parallel", "parallel", "parallel")),
        name="rwkv_chunk",
    )(r, k, v, al, be, lw)
    return pl.pallas_call(
        functools.partial(_rwkv_state_kernel, rows=rows),
        grid=grid,
        in_specs=[q_spec, en_spec, wl_spec, hp, hp, hp, hp,
                  pl.BlockSpec((1, 2, HEAD_DIM), lambda bi, h, t: (h, 0, 0)), par, par],
        out_specs=hp,
        out_shape=jax.ShapeDtypeStruct((b, s, d), BF16),
        scratch_shapes=[pltpu.VMEM((2, HEAD_DIM, HEAD_DIM), F32)],
        compiler_params=_cp(("parallel", "parallel", "arbitrary")),
        name="rwkv_state",
    )(qm, en, wl, r, k, v, g, r_k.reshape(-1, 2, HEAD_DIM), row1(gn_g), row1(gn_b))


def _hgrn_kernel(q_ref, f_ref, i_ref, g_ref, lb_ref, ng_ref, o_ref, st_ref, *, rows):
    t = pl.program_id(2)
    sub = 16

    @pl.when(t == 0)
    def _():
        st_ref[...] = jnp.zeros_like(st_ref)

    q = jax.nn.silu(q_ref[0])
    lb = lb_ref[...]
    fg = lb + (1.0 - lb) * jax.nn.sigmoid(f_ref[0])
    lf = jnp.log(fg)
    kf = 1.0 - fg
    iv = i_ref[0]
    rowc = lax.broadcasted_iota(jnp.int32, lf.shape, 0) & (CHUNK - 1)
    bcum = lf
    for dd in (1, 2, 4, 8, 16, 32):
        bcum = bcum + jnp.where(rowc >= dd, pltpu.roll(bcum, dd, axis=0), 0.0)
    ivb = iv.astype(BF16)
    rsub = lax.broadcasted_iota(jnp.int32, (sub, LANE), 0)
    st = st_ref[...]
    outs = []
    for c in range(rows // CHUNK):
        c0 = c * CHUNK
        bc = bcum[c0:c0 + CHUNK]
        qc = q[c0:c0 + CHUNK]
        kc = kf[c0:c0 + CHUNK]
        ic = iv[c0:c0 + CHUNK]
        icb = ivb[c0:c0 + CHUNK]
        o_inter = _dot((qc * jnp.exp(bc)).astype(BF16), st.astype(BF16), NT_DIMS)
        blocks = []
        for ib in range(CHUNK // sub):
            r0 = ib * sub
            bi = bc[r0:r0 + sub]
            qi = qc[r0:r0 + sub]
            ki = kc[r0:r0 + sub]
            ii = ic[r0:r0 + sub]
            acc = jnp.zeros((sub, LANE), F32)
            for s_ in range(sub):
                e = jnp.where(rsub >= s_, jnp.exp(jnp.minimum(bi - bi[s_:s_ + 1], 0.0)), 0.0)
                a = jnp.sum(qi * (ki[s_:s_ + 1] * e), axis=-1, keepdims=True)
                acc = acc + a * ii[s_:s_ + 1]
            if ib > 0:
                ref_row = bc[r0 - 1:r0]
                qt = (qi * jnp.exp(bi - ref_row)).astype(BF16)
                kt = (kc[0:r0] * jnp.exp(ref_row - bc[0:r0])).astype(BF16)
                a_off = _dot(qt, kt, NT_DIMS).astype(BF16)
                acc = acc + _dot(a_off, icb[0:r0])
            blocks.append(acc)
        outs.append(o_inter + jnp.concatenate(blocks, axis=0))
        bl = bc[CHUNK - 1:CHUNK]
        kdec = (kc * jnp.exp(bl - bc)).astype(BF16)
        st = st * jnp.exp(bl) + _dot(icb, kdec, TN_DIMS)
    st_ref[...] = st
    o = jnp.concatenate(outs, axis=0)
    ms = jnp.mean(o * o, axis=-1, keepdims=True)
    o = o * lax.rsqrt(ms + EPS) * ng_ref[...]
    o_ref[0] = (o * jax.nn.silu(g_ref[0])).astype(o_ref.dtype)


def hgrn2(pm, lb, norm_g, rows=256):
    b, s, _ = pm.shape
    heads = D_BRANCH // LANE
    spec = lambda u: pl.BlockSpec((1, rows, LANE), lambda bi, h, t: (bi, t, u + h))
    par = pl.BlockSpec((1, LANE), lambda bi, h, t: (0, h))
    return pl.pallas_call(
        functools.partial(_hgrn_kernel, rows=rows),
        grid=(b, heads, s // rows),
        in_specs=[spec(U_CQ), spec(U_CF), spec(U_CI), spec(U_CG), par, par],
        out_specs=spec(0),
        out_shape=jax.ShapeDtypeStruct((b, s, D_BRANCH), BF16),
        scratch_shapes=[pltpu.VMEM((LANE, LANE), F32)],
        compiler_params=_cp(("parallel", "parallel", "arbitrary")),
        name="hgrn2",
    )(pm, pm, pm, pm, lb.reshape(1, -1), norm_g.reshape(1, -1))


def _dsa_prep_kernel(lo_ref, hi_ref, cos_ref, sin_ref,
                     qo_ref, iqo_ref, ko_ref, vo_ref, iko_ref, wo_ref, *, scale, wscale):
    cos = cos_ref[0]
    sin = sin_ref[0]
    hd = HEAD_DIM
    lo = lo_ref[0]
    hi = hi_ref[0]
    col = lambda u: (u - U_D_LO) * LANE
    q = (_rope(lo[:, col(U_DQ):col(U_DKV)], cos, sin) * scale).astype(BF16)
    iq_in = jnp.concatenate([lo[:, col(U_DIQ):], hi[:, :(U_DIKW - U_D_HI) * LANE]], axis=1)
    iq = _rope(iq_in, cos, sin).astype(BF16)
    for h in range(DSA_HEADS):
        qo_ref[0, h] = q[:, h * hd:(h + 1) * hd]
        iqo_ref[0, h] = iq[:, h * hd:(h + 1) * hd]
    kv = lo[:, col(U_DKV):col(U_DIQ)]
    ko_ref[0] = _rope(kv, cos, sin)[:, :hd].astype(BF16)
    vo_ref[0] = kv[:, hd:].astype(BF16)
    ikw = hi[:, (U_DIKW - U_D_HI) * LANE:]
    iko_ref[0] = _rope(ikw, cos, sin)[:, :hd].astype(BF16)
    wo_ref[0] = ikw[:, hd:hd + IDX_HEADS] * wscale


def _to_key(x):
    x = jnp.where(x == 0.0, 0.0, x)
    bits = pltpu.bitcast(x, jnp.int32)
    return jnp.where(bits < 0, bits ^ jnp.int32(0x7FFFFFFF), bits)


def _dsa_index_kernel(iq_ref, ik_ref, wt_ref, mask_ref, key_ref, *, tq, tk, top_k):
    i = pl.program_id(1)
    nk = key_ref.shape[0]
    nvis = (i * tq + tq - 1) // tk + 1
    krow = lax.broadcasted_iota(jnp.int32, (tk, tq), 0)
    qcol = i * tq + lax.broadcasted_iota(jnp.int32, (tk, tq), 1)
    wt = wt_ref[0]

    def score_block(jb):
        ikb = ik_ref[0, pl.ds(pl.multiple_of(jb * tk, tk), tk), :]
        s_all = _dot(ikb, iq_ref[0].reshape(IDX_HEADS * tq, HEAD_DIM), NT_DIMS)
        acc = jnp.zeros((tk, tq), F32)
        for h in range(IDX_HEADS):
            acc = acc + jnp.maximum(s_all[:, h * tq:(h + 1) * tq], 0.0) * wt[h:h + 1, :]
        vis = ((jb * tk + krow) >> 6) <= (qcol >> 6)
        key_ref[jb] = jnp.where(vis, _to_key(acc), INT_MIN)

    _paired_loop(nvis, score_block)

    def count(pred_fn):
        def blk(jb, cnt):
            ind = jnp.where(pred_fn(key_ref[jb], jb), 1, 0)
            return cnt + jnp.sum(ind.reshape(tk // 8, 8, tq), axis=0)
        cnt = lax.fori_loop(0, nvis, blk, jnp.zeros((8, tq), jnp.int32))
        return jnp.sum(cnt, axis=0, keepdims=True)

    def bit_step(bi, prefix):
        cand = prefix | lax.shift_left(jnp.int32(1), 31 - bi)
        cand_s = cand ^ jnp.int32(INT_MIN)
        c = count(lambda kb, jb: kb >= cand_s)
        return jnp.where(c >= top_k, cand, prefix)

    prefix = lax.fori_loop(0, 32, bit_step, jnp.zeros((1, tq), jnp.int32))
    tau = prefix ^ jnp.int32(INT_MIN)
    n_gt = count(lambda kb, jb: kb > tau)
    n_eq = count(lambda kb, jb: kb == tau)
    need = top_k - n_gt
    tied = (n_eq > need) & (tau != INT_MIN)
    n_cols = nk * tk

    idx_bits = int(n_cols).bit_length()

    def tie_break():
        def idx_step(bi, pre):
            cand = pre | lax.shift_left(jnp.int32(1), idx_bits - 1 - bi)
            c = count(lambda kb, jb: (kb == tau) & ((jb * tk + krow) < cand))
            return jnp.where(c < need, cand, pre)
        cut = lax.fori_loop(0, idx_bits, idx_step, jnp.zeros((1, tq), jnp.int32))
        return jnp.where(tied, cut, n_cols)

    cut = lax.cond(jnp.max(tied.astype(jnp.int32)) > 0, tie_break,
                   lambda: jnp.full((1, tq), n_cols, jnp.int32))

    def write_block(jb, carry):
        kb = key_ref[jb]
        sel = (kb > tau) | ((kb == tau) & ((jb * tk + krow) <= cut))
        sel = sel & (kb != INT_MIN)
        mask_ref[0, jb] = jnp.where(sel, 1.0, 0.0).T.astype(mask_ref.dtype)
        return carry

    def zero_block(jb, carry):
        mask_ref[0, jb] = jnp.zeros((tq, tk), mask_ref.dtype)
        return carry

    lax.fori_loop(0, nvis, write_block, 0)
    lax.fori_loop(nvis, nk, zero_block, 0)


def _dsa_attn_kernel(q_ref, k_ref, v_ref, mask_ref, o_ref, s_ref, m_ref, l_ref, acc_ref, *, tq, tk):
    i = pl.program_id(1)
    nh = DSA_HEADS
    hd = HEAD_DIM
    nvis = (i * tq + tq - 1) // tk + 1
    nchunk = tk // LANE

    m_ref[...] = jnp.full_like(m_ref, NEG)

    def max_step(j):
        kb = k_ref[0, pl.ds(pl.multiple_of(j * tk, tk), tk), :]
        s = _dot(q_ref[0].reshape(nh * tq, hd), kb, NT_DIMS).reshape(nh, tq, tk)
        s = jnp.where((mask_ref[0, j] > 0)[None], s, NEG)
        s_ref[j] = s
        m_ref[...] = jnp.maximum(m_ref[...], _lane_fold(s, jnp.maximum))

    _paired_loop(nvis, max_step)
    m_ref[...] = jnp.broadcast_to(jnp.max(m_ref[...], axis=2, keepdims=True), m_ref.shape)
    l_ref[...] = jnp.zeros_like(l_ref)
    acc_ref[...] = jnp.zeros_like(acc_ref)

    def acc_step(j):
        s = s_ref[j]
        mb = m_ref[...]
        ps = [jnp.exp2(s[:, :, c * LANE:(c + 1) * LANE] - mb) for c in range(nchunk)]
        l_ref[...] += functools.reduce(jnp.add, ps)
        p = jnp.concatenate(ps, axis=2).astype(BF16).reshape(nh * tq, tk)
        vb = v_ref[0, pl.ds(pl.multiple_of(j * tk, tk), tk), :]
        acc_ref[...] += _dot(p, vb).reshape(nh, tq, hd)

    _paired_loop(nvis, acc_step)
    o = acc_ref[...] / jnp.sum(l_ref[...], axis=2, keepdims=True)
    o_ref[0] = jnp.concatenate([o[h] for h in range(nh)], axis=1).astype(o_ref.dtype)


def dsa_attention(pm, cos, sin, ts=512, tq=128, tk=512):
    b, s, _ = pm.shape
    top_k = min(TOPK_MAX, s // 4)
    hd = HEAD_DIM
    nk = s // tk
    tab = pl.BlockSpec((1, ts, LANE), lambda bi, i: (bi, i, 0))
    heads_out = pl.BlockSpec((1, DSA_HEADS, ts, hd), lambda bi, i: (bi, 0, i, 0))
    narrow = lambda w: pl.BlockSpec((1, ts, w), lambda bi, i: (bi, i, 0))
    w_lo = (U_D_HI - U_D_LO) * LANE
    w_hi = N_MAIN - U_D_HI * LANE
    q, iq, k, v, ik, w = pl.pallas_call(
        functools.partial(_dsa_prep_kernel, scale=hd ** -0.5 * LOG2E, wscale=(IDX_HEADS ** -0.5) * (hd ** -0.5)),
        grid=(b, s // ts),
        in_specs=[pl.BlockSpec((1, ts, w_lo), lambda bi, i: (bi, i, U_D_LO * LANE // w_lo)),
                  pl.BlockSpec((1, ts, w_hi), lambda bi, i: (bi, i, U_D_HI * LANE // w_hi)),
                  tab, tab],
        out_specs=[heads_out, heads_out, narrow(hd), narrow(hd), narrow(hd), narrow(IDX_HEADS)],
        out_shape=[jax.ShapeDtypeStruct((b, DSA_HEADS, s, hd), BF16)] * 2
                  + [jax.ShapeDtypeStruct((b, s, hd), BF16)] * 3
                  + [jax.ShapeDtypeStruct((b, s, IDX_HEADS), F32)],
        compiler_params=_cp(("parallel", "parallel")),
        name="dsa_prep",
    )(pm, pm, cos, sin)

    mask_spec = pl.BlockSpec((1, nk, tq, tk), lambda bi, i: (bi, 0, i, 0))
    whole = pl.BlockSpec((1, s, hd), lambda bi, i: (bi, 0, 0))
    mask = pl.pallas_call(
        functools.partial(_dsa_index_kernel, tq=tq, tk=tk, top_k=top_k),
        grid=(b, s // tq),
        in_specs=[pl.BlockSpec((1, IDX_HEADS, tq, hd), lambda bi, i: (bi, 0, i, 0)),
                  whole,
                  pl.BlockSpec((1, IDX_HEADS, tq), lambda bi, i: (bi, 0, i))],
        out_specs=mask_spec,
        out_shape=jax.ShapeDtypeStruct((b, nk, s, tk), BF16),
        scratch_shapes=[pltpu.VMEM((nk, tk, tq), jnp.int32)],
        compiler_params=_cp(("parallel", "parallel")),
        name="dsa_index",
    )(iq, ik, jnp.swapaxes(w, 1, 2))

    return pl.pallas_call(
        functools.partial(_dsa_attn_kernel, tq=tq, tk=tk),
        grid=(b, s // tq),
        in_specs=[pl.BlockSpec((1, DSA_HEADS, tq, hd), lambda bi, i: (bi, 0, i, 0)),
                  whole, whole, mask_spec],
        out_specs=pl.BlockSpec((1, tq, D_BRANCH), lambda bi, i: (bi, i, 0)),
        out_shape=jax.ShapeDtypeStruct((b, s, D_BRANCH), BF16),
        scratch_shapes=[pltpu.VMEM((nk, DSA_HEADS, tq, tk), F32),
                        pltpu.VMEM((DSA_HEADS, tq, LANE), F32), pltpu.VMEM((DSA_HEADS, tq, LANE), F32),
                        pltpu.VMEM((DSA_HEADS, tq, hd), F32)],
        compiler_params=_cp(("parallel", "parallel")),
        name="dsa_attention",
    )(q, k, v, mask)


GATE_COL = 6600


def kernel(x, positions, norm_g, w_in, diff_lambda, diff_subln_g, rwkv_mu, rwkv_w_up, rwkv_a_up, rwkv_g_up, rwkv_w0, rwkv_a0, rwkv_k_k, rwkv_k_a, rwkv_r_k, rwkv_gn_g, rwkv_gn_b, hgrn_lb_logits, hgrn_norm_g, w_branch, w_out, mlp_w1, mlp_w2):
    b, s, d = x.shape
    m = b * s
    depth = w_in.shape[0]
    lb = jax.nn.softmax(hgrn_lb_logits.astype(F32), axis=0)
    lb = jnp.cumsum(lb, axis=0) - lb[0]
    w_in_t = jnp.swapaxes(w_in, 1, 2)
    w_out_b = w_out.astype(BF16)
    w2_b = mlp_w2.astype(BF16)
    cos, sin = rope_tables(positions)
    cos3 = cos.reshape(b, s, LANE)
    sin3 = sin.reshape(b, s, LANE)
    x2 = x.reshape(m, d)
    h = rmsnorm_bf16(x2, norm_g[0, 0])
    for l in range(depth):
        pm = matmul_wt(h, w_in_t, l, 0, N_MAIN, None, F32, name="proj_main").reshape(b, s, N_MAIN)
        gate = matmul_wt(h, w_in_t, l, GATE_COL, N_BRANCH * D_MODEL, "sigmoid", BF16, name="proj_gate")
        y_a = diff_attention(pm, cos3, sin3, diff_lambda[l], diff_subln_g[l], l)
        y_b = rwkv7(pm, rwkv_mu[l], rwkv_w_up[l], rwkv_a_up[l], rwkv_g_up[l], rwkv_w0[l], rwkv_a0[l],
                    rwkv_k_k[l], rwkv_k_a[l], rwkv_r_k[l], rwkv_gn_g[l], rwkv_gn_b[l])
        y_c = hgrn2(pm, lb[l], hgrn_norm_g[l])
        y_d = dsa_attention(pm, cos3, sin3)
        ys = [y.reshape(m, D_BRANCH) for y in (y_a, y_b, y_c, y_d)]
        merged = gated_merge(ys, w_branch, l, gate)
        x2, h2 = matmul_norm_residual(merged, w_out_b, l, x2, norm_g[l, 1], norm_g[l, 2], name="out_proj")
        ff = matmul(h2, mlp_w1, l, "relu2", BF16, name="mlp_up")
        if l + 1 < depth:
            x2, h = matmul_norm_residual(ff, w2_b, l, x2, norm_g[l, 3], norm_g[l + 1, 0], name="mlp_down")
        else:
            x2 = matmul_norm_residual(ff, w2_b, l, x2, norm_g[l, 3], name="mlp_down")
    return x2.reshape(b, s, d)
```

```python
import functools
import math

import numpy as np
import jax
import jax.numpy as jnp
from jax import lax
from jax.experimental import pallas as pl
from jax.experimental.pallas import tpu as pltpu

F32 = jnp.float32
BF16 = jnp.bfloat16

D_MODEL = 2048
D_BRANCH = 512
D_FF = 8192
N_BRANCH = 4
CHUNK = 64
EPS = 1e-6
ROPE_THETA = 10000.0
HEAD_DIM = 64
DIFF_HEADS = 4
RWKV_GN_EPS = 64e-5
DSA_HEADS = 8
IDX_HEADS = 8
TOPK_MAX = 256

LANE = 128
VMEM_LIMIT = 56 * 1024 * 1024

U_AQ, U_AK, U_AV = 0, 4, 8
U_B = 12
U_CQ, U_CF, U_CI, U_CG = 26, 30, 34, 38
U_DQ, U_DKV, U_DIQ, U_DIKW = 42, 46, 47, 51
U_D_LO, U_D_HI = 40, 48
N_MAIN = 52 * LANE
B_WIDTH = 1792
B_BLOCK = 3584

NEG = -1e30
INT_MIN = -2147483648
LOG2E = 1.4426950408889634

NT_DIMS = (((1,), (1,)), ((), ()))
TN_DIMS = (((0,), (0,)), ((), ()))


def _cp(sem, vmem=VMEM_LIMIT):
    return pltpu.CompilerParams(dimension_semantics=sem, vmem_limit_bytes=vmem)


def _dot(a, b, dims=None):
    if dims is None:
        return jnp.dot(a, b, preferred_element_type=F32)
    return lax.dot_general(a, b, dims, preferred_element_type=F32)


def _split2(a):
    hi = a.astype(BF16)
    return hi, (a - hi.astype(F32)).astype(BF16)


def _dot_split(a, b):
    a_hi, a_lo = _split2(a)
    b_hi, b_lo = _split2(b)
    return _dot(a_hi, b_hi) + (_dot(a_hi, b_lo) + _dot(a_lo, b_hi))


def _rmsnorm_kernel(x_ref, g_ref, o_ref):
    x = x_ref[...]
    ms = jnp.mean(x * x, axis=-1, keepdims=True)
    o_ref[...] = (x * lax.rsqrt(ms + EPS) * g_ref[...]).astype(o_ref.dtype)


def rmsnorm_bf16(x, g, tm=512):
    m, d = x.shape
    return pl.pallas_call(
        _rmsnorm_kernel,
        grid=(m // tm,),
        in_specs=[pl.BlockSpec((tm, d), lambda i: (i, 0)),
                  pl.BlockSpec((1, d), lambda i: (0, 0))],
        out_specs=pl.BlockSpec((tm, d), lambda i: (i, 0)),
        out_shape=jax.ShapeDtypeStruct((m, d), BF16),
        compiler_params=_cp(("parallel",)),
        name="rmsnorm",
    )(x, g.reshape(1, d))


def _mm_kernel(x_ref, w_ref, o_ref, *, act, dims):
    w = w_ref[...] if len(w_ref.shape) == 2 else w_ref[0]
    acc = _dot(x_ref[...], w.astype(BF16), dims)
    if act == "sigmoid":
        acc = 0.5 * jnp.tanh(0.5 * acc) + 0.5
    elif act == "relu2":
        acc = jnp.square(jnp.maximum(acc, 0.0))
    o_ref[...] = acc.astype(o_ref.dtype)


def matmul(x, w, layer, act, out_dtype, tm=2048, tn=512, name="mm"):
    m, k = x.shape
    n = w.shape[2]
    tm = min(tm, m)
    return pl.pallas_call(
        functools.partial(_mm_kernel, act=act, dims=None),
        grid=(m // tm, n // tn),
        in_specs=[pl.BlockSpec((tm, k), lambda i, j: (i, 0)),
                  pl.BlockSpec((None, k, tn), lambda i, j: (layer, 0, j))],
        out_specs=pl.BlockSpec((tm, tn), lambda i, j: (i, j)),
        out_shape=jax.ShapeDtypeStruct((m, n), out_dtype),
        compiler_params=_cp(("parallel", "parallel")),
        name=name,
    )(x, w)


def matmul_wt(x, wt, layer, row0, n, act, out_dtype, tm=2048, tn=512, name="mm_wt"):
    m, k = x.shape
    tm = min(tm, m)
    return pl.pallas_call(
        functools.partial(_mm_kernel, act=act, dims=NT_DIMS),
        grid=(m // tm, n // tn),
        in_specs=[pl.BlockSpec((tm, k), lambda i, j: (i, 0)),
                  pl.BlockSpec((pl.Element(1), pl.Element(tn), pl.Element(k)),
                               lambda i, j: (layer, pl.multiple_of(row0 + j * tn, 8), 0))],
        out_specs=pl.BlockSpec((tm, tn), lambda i, j: (i, j)),
        out_shape=jax.ShapeDtypeStruct((m, n), out_dtype),
        compiler_params=_cp(("parallel", "parallel")),
        name=name,
    )(x, wt)


def _norm_residual(y, x_ref, g_ref, o_ref, h_ref):
    ms = jnp.mean(y * y, axis=-1, keepdims=True)
    o = x_ref[...] + y * lax.rsqrt(ms + EPS) * g_ref[0:1, :]
    o_ref[...] = o
    if h_ref is not None:
        ms = jnp.mean(o * o, axis=-1, keepdims=True)
        h_ref[...] = (o * lax.rsqrt(ms + EPS) * g_ref[1:2, :]).astype(h_ref.dtype)


def _mm_norm_res_kernel(a_ref, w_ref, x_ref, g_ref, o_ref, *rest):
    h_ref, acc_ref = (rest[0], rest[1]) if len(rest) == 2 else (None, rest[0])
    kk = pl.program_id(1)

    @pl.when(kk == 0)
    def _():
        acc_ref[...] = jnp.zeros_like(acc_ref)

    acc_ref[...] += _dot(a_ref[...], w_ref[...])

    @pl.when(kk == pl.num_programs(1) - 1)
    def _():
        _norm_residual(acc_ref[...], x_ref, g_ref, o_ref, h_ref)


def _mm_norm_res_fullk_kernel(a_ref, w_ref, x_ref, g_ref, o_ref, *rest):
    _norm_residual(_dot(a_ref[...], w_ref[...]), x_ref, g_ref, o_ref, rest[0] if rest else None)


def matmul_norm_residual(a, w, layer, x, g, g_next=None, tm=512, tk=2048, name="mm_norm_res"):
    m, k = a.shape
    n = w.shape[2]
    gs = jnp.stack([g, g if g_next is None else g_next])
    out_shape = [jax.ShapeDtypeStruct((m, n), F32)]
    if g_next is not None:
        out_shape.append(jax.ShapeDtypeStruct((m, n), BF16))
    if k <= tk:
        row = lambda i: (i, 0)
        outs = pl.pallas_call(
            _mm_norm_res_fullk_kernel,
            grid=(m // tm,),
            in_specs=[pl.BlockSpec((tm, k), row), pl.BlockSpec((None, k, n), lambda i: (layer, 0, 0)),
                      pl.BlockSpec((tm, n), row), pl.BlockSpec((2, n), lambda i: (0, 0))],
            out_specs=[pl.BlockSpec((tm, n), row)] * len(out_shape),
            out_shape=out_shape,
            compiler_params=_cp(("parallel",)),
            name=name,
        )(a, w, x, gs)
    else:
        row = lambda i, kk: (i, 0)
        outs = pl.pallas_call(
            _mm_norm_res_kernel,
            grid=(m // tm, k // tk),
            in_specs=[pl.BlockSpec((tm, tk), lambda i, kk: (i, kk)),
                      pl.BlockSpec((None, tk, n), lambda i, kk: (layer, kk, 0)),
                      pl.BlockSpec((tm, n), row),
                      pl.BlockSpec((2, n), lambda i, kk: (0, 0))],
            out_specs=[pl.BlockSpec((tm, n), row)] * len(out_shape),
            out_shape=out_shape,
            scratch_shapes=[pltpu.VMEM((tm, n), F32)],
            compiler_params=_cp(("parallel", "arbitrary")),
            name=name,
        )(a, w, x, gs)
    return outs if g_next is not None else outs[0]


def _merge_kernel(ya, yb, yc, yd, wb, ga, gb, gc, gd, o_ref):
    acc = None
    for n, (y, g) in enumerate(((ya, ga), (yb, gb), (yc, gc), (yd, gd))):
        t = g[...].astype(F32) * _dot(y[...], wb[n].astype(BF16))
        acc = t if acc is None else acc + t
    o_ref[...] = acc.astype(o_ref.dtype)


def gated_merge(ys, wb, layer, gate, tm=512, tn=1024):
    m = ys[0].shape[0]
    nj = D_MODEL // tn
    y_spec = pl.BlockSpec((tm, D_BRANCH), lambda j, i: (i, 0))
    g_specs = [pl.BlockSpec((tm, tn), functools.partial(lambda j, i, n: (i, n * nj + j), n=n))
               for n in range(N_BRANCH)]
    return pl.pallas_call(
        _merge_kernel,
        grid=(nj, m // tm),
        in_specs=[y_spec] * 4 + [pl.BlockSpec((None, N_BRANCH, D_BRANCH, tn), lambda j, i: (layer, 0, 0, j))]
                 + g_specs,
        out_specs=pl.BlockSpec((tm, tn), lambda j, i: (i, j)),
        out_shape=jax.ShapeDtypeStruct((m, D_MODEL), BF16),
        compiler_params=_cp(("parallel", "parallel")),
        name="gated_merge",
    )(*ys, wb, gate, gate, gate, gate)


def _rope_table_kernel(pos_ref, inv_ref, sgn_ref, cos_ref, sin_ref):
    ang = pos_ref[...].astype(F32) * inv_ref[...]
    cos_ref[...] = jnp.cos(ang)
    sin_ref[...] = jnp.sin(ang) * sgn_ref[...]


def rope_tables(positions, tm=512):
    m = positions.size
    half = HEAD_DIM // 2
    inv = ROPE_THETA ** (-np.arange(0, HEAD_DIM, 2, dtype=np.float32) / HEAD_DIM)
    inv = np.tile(inv.astype(np.float32), 4).reshape(1, LANE)
    sgn = np.tile(np.concatenate([-np.ones(half, np.float32), np.ones(half, np.float32)]), 2).reshape(1, LANE)
    spec = pl.BlockSpec((tm, LANE), lambda i: (i, 0))
    cst = pl.BlockSpec((1, LANE), lambda i: (0, 0))
    return pl.pallas_call(
        _rope_table_kernel,
        grid=(m // tm,),
        in_specs=[pl.BlockSpec((tm, 1), lambda i: (i, 0)), cst, cst],
        out_specs=[spec, spec],
        out_shape=[jax.ShapeDtypeStruct((m, LANE), F32)] * 2,
        compiler_params=_cp(("parallel",)),
        name="rope_tables",
    )(positions.reshape(m, 1), jnp.asarray(inv), jnp.asarray(sgn))


def _rope(x, cos, sin):
    w = x.shape[1]
    n = w // LANE
    if n > 1:
        cos = jnp.concatenate([cos] * n, axis=1)
        sin = jnp.concatenate([sin] * n, axis=1)
    lane = lax.broadcasted_iota(jnp.int32, x.shape, 1)
    up = pltpu.roll(x, w - HEAD_DIM // 2, axis=1)
    dn = pltpu.roll(x, HEAD_DIM // 2, axis=1)
    rot = jnp.where((lane & (HEAD_DIM // 2)) == 0, up, dn)
    return x * cos + rot * sin


def _rope_qkv_kernel(q_ref, k_ref, v_ref, cos_ref, sin_ref, qo_ref, ko_ref, vo_ref, *, scale):
    cos = cos_ref[0]
    sin = sin_ref[0]
    hd = HEAD_DIM
    q = (_rope(q_ref[0], cos, sin) * scale).astype(qo_ref.dtype)
    k = _rope(k_ref[0], cos, sin).astype(ko_ref.dtype)
    for h in range(q.shape[1] // hd):
        qo_ref[0, h] = q[:, h * hd:(h + 1) * hd]
        ko_ref[0, h] = k[:, h * hd:(h + 1) * hd]
    vo_ref[0] = v_ref[0].astype(vo_ref.dtype)


def rope_qkv(pm, cos, sin, uq, uk, uv, scale, ts=512):
    b, s, _ = pm.shape
    wq = D_BRANCH
    nh = wq // HEAD_DIM
    tab = pl.BlockSpec((1, ts, LANE), lambda bi, i: (bi, i, 0))
    col = lambda u: pl.BlockSpec((1, ts, wq), lambda bi, i: (bi, i, u * LANE // wq))
    heads = pl.BlockSpec((1, nh, ts, HEAD_DIM), lambda bi, i: (bi, 0, i, 0))
    return pl.pallas_call(
        functools.partial(_rope_qkv_kernel, scale=scale),
        grid=(b, s // ts),
        in_specs=[col(uq), col(uk), col(uv), tab, tab],
        out_specs=[heads, heads, pl.BlockSpec((1, ts, wq), lambda bi, i: (bi, i, 0))],
        out_shape=[jax.ShapeDtypeStruct((b, nh, s, HEAD_DIM), BF16)] * 2
                  + [jax.ShapeDtypeStruct((b, s, wq), BF16)],
        compiler_params=_cp(("parallel", "parallel")),
        name="rope_qkv",
    )(pm, pm, pm, cos, sin)


def _lane_fold(x, op):
    out = x[..., 0:LANE]
    for c in range(1, x.shape[-1] // LANE):
        out = op(out, x[..., c * LANE:(c + 1) * LANE])
    return out


def _paired_loop(n, step):
    def body(jj, carry):
        step(2 * jj)
        step(2 * jj + 1)
        return carry
    lax.fori_loop(0, n // 2, body, 0)

    @pl.when(n % 2 == 1)
    def _():
        step(n - 1)


def _diff_attn_kernel(q_ref, k_ref, v_ref, lam_ref, g_ref, o_ref, s_ref, m_ref, l_ref, acc_ref, *, tq, tk, lam_init):
    i = pl.program_id(2)
    nfull = (i * tq) // tk
    row = lax.broadcasted_iota(jnp.int32, (tq, tk), 0)
    col = lax.broadcasted_iota(jnp.int32, (tq, tk), 1)
    vis = (col >> 6) <= (row >> 6)

    m_ref[...] = jnp.full_like(m_ref, NEG)

    def max_step(j, masked=False):
        for mm in range(2):
            kb = k_ref[0, mm, pl.ds(pl.multiple_of(j * tk, tk), tk), :]
            s = _dot(q_ref[0, mm], kb, NT_DIMS)
            if masked:
                s = jnp.where(vis, s, NEG)
            s_ref[j, mm] = s
            m_ref[mm] = jnp.maximum(m_ref[mm], _lane_fold(s, jnp.maximum))

    _paired_loop(nfull, max_step)
    max_step(nfull, True)
    for mm in range(2):
        m_ref[mm] = jnp.broadcast_to(jnp.max(m_ref[mm], axis=1, keepdims=True), (tq, LANE))
    l_ref[...] = jnp.zeros_like(l_ref)
    acc_ref[...] = jnp.zeros_like(acc_ref)

    def acc_step(j):
        vb = v_ref[0, pl.ds(pl.multiple_of(j * tk, tk), tk), :]
        for mm in range(2):
            s = s_ref[j, mm]
            mb = m_ref[mm]
            ps = [jnp.exp2(s[:, c * LANE:(c + 1) * LANE] - mb) for c in range(tk // LANE)]
            l_ref[mm] += functools.reduce(jnp.add, ps)
            acc_ref[mm] += _dot(jnp.concatenate(ps, axis=1).astype(BF16), vb)

    _paired_loop(nfull + 1, acc_step)

    lv = lam_ref[...]
    lam = (jnp.exp(jnp.sum(lv[0:1] * lv[1:2], keepdims=True))
           - jnp.exp(jnp.sum(lv[2:3] * lv[3:4], keepdims=True)) + lam_init)
    l0 = jnp.sum(l_ref[0], axis=1, keepdims=True)
    l1 = jnp.sum(l_ref[1], axis=1, keepdims=True)
    o = acc_ref[0] / l0 - lam * (acc_ref[1] / l1)
    ms = jnp.mean(o * o, axis=-1, keepdims=True)
    o = o * lax.rsqrt(ms + EPS) * g_ref[...] * (1.0 - lam_init)
    o_ref[0] = o.astype(o_ref.dtype)


def diff_attention(pm, cos, sin, lam_vecs, subln_g, layer, tq=512):
    b, s, _ = pm.shape
    tk = tq
    qa, ka, va = rope_qkv(pm, cos, sin, U_AQ, U_AK, U_AV, HEAD_DIM ** -0.5 * LOG2E)
    lam_init = 0.8 - 0.6 * math.exp(-0.3 * layer)
    return pl.pallas_call(
        functools.partial(_diff_attn_kernel, tq=tq, tk=tk, lam_init=lam_init),
        grid=(b, DIFF_HEADS, s // tq),
        in_specs=[pl.BlockSpec((1, 2, tq, HEAD_DIM), lambda bi, h, i: (bi, h, i, 0)),
                  pl.BlockSpec((1, 2, s, HEAD_DIM), lambda bi, h, i: (bi, h, 0, 0)),
                  pl.BlockSpec((1, s, LANE), lambda bi, h, i: (bi, 0, h)),
                  pl.BlockSpec((4, HEAD_DIM), lambda bi, h, i: (0, 0)),
                  pl.BlockSpec((1, LANE), lambda bi, h, i: (0, 0))],
        out_specs=pl.BlockSpec((1, tq, LANE), lambda bi, h, i: (bi, i, h)),
        out_shape=jax.ShapeDtypeStruct((b, s, D_BRANCH), BF16),
        scratch_shapes=[pltpu.VMEM((s // tk, 2, tq, tk), F32),
                        pltpu.VMEM((2, tq, LANE), F32), pltpu.VMEM((2, tq, LANE), F32),
                        pltpu.VMEM((2, tq, LANE), F32)],
        compiler_params=_cp(("parallel", "parallel", "parallel")),
        name="diff_attention",
    )(qa, ka, va, lam_vecs, subln_g.reshape(1, LANE))


def _softplus(z):
    return jnp.maximum(z, 0.0) + jnp.log(1.0 + jnp.exp(-jnp.abs(z)))


def _rwkv_prep_kernel(x_ref, prev_ref, mu_ref, wup_ref, aup_ref, gup_ref, w0_ref, a0_ref, kkg_ref, kag_ref,
                      seg_ref, r_ref, k_ref, v_ref, al_ref, be_ref, lw_ref, g_ref):
    i = pl.program_id(1)
    c0 = U_B * LANE
    p = x_ref[0, :, c0:c0 + B_WIDTH]
    row = lax.broadcasted_iota(jnp.int32, p.shape, 0)
    last = prev_ref[0, 7:8, c0:c0 + B_WIDTH]
    last = jnp.where(i == 0, jnp.zeros_like(last), last)
    prev = jnp.where(row == 0, jnp.broadcast_to(last, p.shape), pltpu.roll(p, 1, axis=0))
    ps = p + (prev - p) * mu_ref[...]
    d = D_BRANCH
    r, k, v = ps[:, 0:d], ps[:, d:2 * d], ps[:, 2 * d:3 * d]
    wd = ps[:, 3 * d:3 * d + 64]
    ad = ps[:, 3 * d + 64:3 * d + 128]
    gd = ps[:, 3 * d + 128:3 * d + 256]
    w_log = -_softplus(-(w0_ref[...] + _dot_split(jnp.tanh(wd), wup_ref[...]))) - 0.5
    a = jax.nn.sigmoid(a0_ref[...] + _dot_split(ad, aup_ref[...]))
    g = _dot_split(jax.nn.sigmoid(gd), gup_ref[...])
    kk = k * kkg_ref[...]
    sq_hi, sq_lo = _split2(kk * kk)
    nrm = jnp.sqrt(_dot(sq_hi, seg_ref[...]) + _dot(sq_lo, seg_ref[...]))
    kk = kk / jnp.maximum(nrm, 1e-12)
    r_ref[0] = r
    k_ref[0] = k * (1.0 + (a - 1.0) * kag_ref[...])
    v_ref[0] = v
    al_ref[0] = -kk
    be_ref[0] = kk * a
    lw_ref[0] = -jnp.exp(w_log)
    g_ref[0] = g


def _bdot(a, b, dims):
    return lax.dot_general(a, b, ((dims[0], dims[1]), ((0,), (0,))), preferred_element_type=F32)


BNN = ((2,), (1,))
BNT = ((2,), (2,))


def _rwkv_chunk_kernel(r_ref, k_ref, v_ref, al_ref, be_ref, lw_ref, q_ref, en_ref, wl_ref, *, rows, sub):
    hd = HEAD_DIM
    nch = rows // CHUNK
    nsb = rows // sub
    lw = lw_ref[0]
    rowc = lax.broadcasted_iota(jnp.int32, lw.shape, 0) & (CHUNK - 1)
    cum = lw
    for dd in (1, 2, 4, 8, 16, 32):
        cum = cum + jnp.where(rowc >= dd, pltpu.roll(cum, dd, axis=0), 0.0)
    cl = jnp.concatenate(
        [jnp.broadcast_to(cum[(c + 1) * CHUNK - 1:(c + 1) * CHUNK, :], (CHUNK, LANE)) for c in range(nch)], axis=0)
    e_in = jnp.exp(cum)
    e_out = jnp.exp(-cum)
    e_end = jnp.exp(cl - cum)
    r2, k2, v2, al2, be2 = r_ref[0], k_ref[0], v_ref[0], al_ref[0], be_ref[0]
    at2 = al2 * jnp.exp(cum - lw)
    rt2 = r2 * e_in
    bt2 = be2 * e_out
    kt2 = k2 * e_out
    bh2 = be2 * e_end
    kh2 = k2 * e_end
    wl2 = jnp.exp(cl)

    row = lax.broadcasted_iota(jnp.int32, (sub, sub), 0)
    col = lax.broadcasted_iota(jnp.int32, (sub, sub), 1)
    same = (row >> 6) == (col >> 6)
    m_strict = same & (row > col)
    m_incl = same & (row >= col)
    eye = (row == col).astype(F32)
    m_blk8 = (row >> 3) == (col >> 3)
    m_lvls = [((row >> (sh + 1)) == (col >> (sh + 1))) & ((row >> sh) != (col >> sh)) for sh in (3, 4, 5)]
    wl_ref[0] = jnp.concatenate([wl2[c * CHUNK:c * CHUNK + 1, :] for c in range(nch)], axis=0)

    def stack(a):
        return jnp.stack([a[sb * sub:(sb + 1) * sub, hh * hd:(hh + 1) * hd]
                          for hh in range(2) for sb in range(nsb)])

    at, rt, vb = stack(at2), stack(rt2), stack(v2).astype(BF16)
    lhs = jnp.concatenate([at, rt], axis=1).astype(BF16)
    rhs = jnp.concatenate([stack(bt2), stack(kt2)], axis=1).astype(BF16)
    gm = _bdot(lhs, rhs, BNT)
    n_ab = jnp.where(m_strict[None], gm[:, :sub, :sub], 0.0)
    a_ak = jnp.where(m_strict[None], gm[:, :sub, sub:], 0.0).astype(BF16)
    a_rb = jnp.where(m_incl[None], gm[:, sub:, :sub], 0.0).astype(BF16)
    a_rk = jnp.where(m_incl[None], gm[:, sub:, sub:], 0.0).astype(BF16)
    n8 = jnp.where(m_blk8[None], n_ab, 0.0)
    n8b = n8.astype(BF16)
    n_2 = _bdot(n8b, n8b, BNN)
    n2b = n_2.astype(BF16)
    n_3 = _bdot(n2b, n8b, BNN)
    n_4 = _bdot(n2b, n2b, BNN)
    tinv = eye[None] + n8 + n_2 + n_3
    tinv = tinv + _bdot(tinv.astype(BF16), n_4.astype(BF16), BNN)
    for m_lvl in m_lvls:
        nl = jnp.where(m_lvl[None], n_ab, 0.0).astype(BF16)
        tb = tinv.astype(BF16)
        tinv = tinv + _bdot(_bdot(tb, nl, BNN).astype(BF16), tb, BNN)
    tb = tinv.astype(BF16)
    akv = _bdot(a_ak, vb, BNN)
    pmat = _bdot(tb, jnp.concatenate([at, akv], axis=2).astype(BF16), BNN)
    qmat = _bdot(a_rb, pmat.astype(BF16), BNN) + jnp.concatenate([rt, _bdot(a_rk, vb, BNN)], axis=2)
    pb = pmat.astype(BF16)
    bhb = stack(bh2).astype(BF16)
    khb = stack(kh2).astype(BF16)
    for hh in range(2):
        for sb in range(nsb):
            bi = hh * nsb + sb
            q_ref[0, hh, sb * sub:(sb + 1) * sub, :] = qmat[bi]
            for cc in range(sub // CHUNK):
                cs = slice(cc * CHUNK, (cc + 1) * CHUNK)
                mn = _dot(pb[bi, cs], bhb[bi, cs], TN_DIMS)
                n_c = mn[hd:] + _dot(vb[bi, cs], khb[bi, cs], TN_DIMS)
                en_ref[0, hh, sb * (sub // CHUNK) + cc] = jnp.concatenate([mn[:hd], n_c], axis=0)


def _rwkv_state_kernel(q_ref, en_ref, wl_ref, r_ref, k_ref, v_ref, g_ref, rk_ref, gng_ref, gnb_ref,
                       o_ref, st_ref, *, rows):
    t = pl.program_id(2)
    hd = HEAD_DIM
    nch = rows // CHUNK

    @pl.when(t == 0)
    def _():
        st_ref[...] = jnp.zeros_like(st_ref)

    wl = wl_ref[0]
    r2, k2, v2, g2 = r_ref[0], k_ref[0], v_ref[0], g_ref[0]
    sts = [st_ref[0], st_ref[1]]
    ys = [[], []]
    for c in range(nch):
        for hh in range(2):
            qc = q_ref[0, hh, c * CHUNK:(c + 1) * CHUNK, :]
            en = en_ref[0, hh, c]
            stb = sts[hh].astype(BF16)
            ys[hh].append(_dot(qc[:, :hd].astype(BF16), stb, NT_DIMS) + qc[:, hd:])
            sts[hh] = (sts[hh] * wl[c:c + 1, hh * hd:(hh + 1) * hd]
                       + _dot(stb, en[:hd].astype(BF16)) + en[hd:])
    outs = []
    for hh in range(2):
        sl = slice(hh * hd, (hh + 1) * hd)
        st_ref[hh] = sts[hh]
        v, r, kp = v2[:, sl], r2[:, sl], k2[:, sl]
        y = jnp.concatenate(ys[hh], axis=0)
        mu = jnp.mean(y, axis=-1, keepdims=True)
        var = jnp.mean(jnp.square(y - mu), axis=-1, keepdims=True)
        yn = (y - mu) * lax.rsqrt(var + RWKV_GN_EPS) * gng_ref[:, sl] + gnb_ref[:, sl]
        bonus = jnp.sum(r * kp * rk_ref[0, hh:hh + 1, :], axis=-1, keepdims=True) * v
        outs.append((yn + bonus) * g2[:, sl])
    o_ref[0] = jnp.concatenate(outs, axis=1).astype(o_ref.dtype)


def rwkv7(pm, mu, w_up, a_up, g_up, w0, a0, kk_gain, ka_gain, r_k, gn_g, gn_b, ts=256, rows=512, sub=256):
    b, s, _ = pm.shape
    d = D_BRANCH
    seg = np.kron(np.eye(d // HEAD_DIM, dtype=np.float32), np.ones((HEAD_DIM, HEAD_DIM), np.float32))
    row1 = lambda a: a.reshape(1, -1)
    cst = lambda shape: pl.BlockSpec(shape, lambda bi, i: (0,) * len(shape))
    blk = pl.BlockSpec((1, ts, d), lambda bi, i: (bi, i, 0))
    r, k, v, al, be, lw, g = pl.pallas_call(
        _rwkv_prep_kernel,
        grid=(b, s // ts),
        in_specs=[pl.BlockSpec((1, ts, B_BLOCK), lambda bi, i: (bi, i, 0)),
                  pl.BlockSpec((1, 8, B_BLOCK), lambda bi, i: (bi, jnp.maximum(i * (ts // 8) - 1, 0), 0)),
                  cst((1, B_WIDTH)), cst((64, d)), cst((64, d)), cst((128, d)),
                  cst((1, d)), cst((1, d)), cst((1, d)), cst((1, d)), cst((d, d))],
        out_specs=[blk] * 7,
        out_shape=[jax.ShapeDtypeStruct((b, s, d), F32)] * 7,
        compiler_params=_cp(("parallel", "arbitrary")),
        name="rwkv_prep",
    )(pm, pm, row1(mu), w_up, a_up, g_up, row1(w0), row1(a0), row1(kk_gain), row1(ka_gain), jnp.asarray(seg, BF16))

    hp = pl.BlockSpec((1, rows, LANE), lambda bi, h, t: (bi, t, h))
    par = pl.BlockSpec((1, LANE), lambda bi, h, t: (0, h))
    nh = d // HEAD_DIM
    nch = rows // CHUNK
    q_spec = pl.BlockSpec((1, 2, rows, LANE), lambda bi, h, t: (bi, h, t, 0))
    en_spec = pl.BlockSpec((1, 2, nch, LANE, HEAD_DIM), lambda bi, h, t: (bi, h, t, 0, 0))
    wl_spec = pl.BlockSpec((1, nch, LANE), lambda bi, h, t: (bi, t, h))
    grid = (b, d // LANE, s // rows)
    qm, en, wl = pl.pallas_call(
        functools.partial(_rwkv_chunk_kernel, rows=rows, sub=sub),
        grid=grid,
        in_specs=[hp] * 6,
        out_specs=[q_spec, en_spec, wl_spec],
        out_shape=[jax.ShapeDtypeStruct((b, nh, s, LANE), F32),
                   jax.ShapeDtypeStruct((b, nh, s // CHUNK, LANE, HEAD_DIM), F32),
                   jax.ShapeDtypeStruct((b, s // CHUNK, d), F32)],
        compiler_params=_cp(("parallel", "parallel", "parallel")),
        name="rwkv_chunk",
    )(r, k, v, al, be, lw)
    return pl.pallas_call(
        functools.partial(_rwkv_state_kernel, rows=rows),
        grid=grid,
        in_specs=[q_spec, en_spec, wl_spec, hp, hp, hp, hp,
                  pl.BlockSpec((1, 2, HEAD_DIM), lambda bi, h, t: (h, 0, 0)), par, par],
        out_specs=hp,
        out_shape=jax.ShapeDtypeStruct((b, s, d), BF16),
        scratch_shapes=[pltpu.VMEM((2, HEAD_DIM, HEAD_DIM), F32)],
        compiler_params=_cp(("parallel", "parallel", "arbitrary")),
        name="rwkv_state",
    )(qm, en, wl, r, k, v, g, r_k.reshape(-1, 2, HEAD_DIM), row1(gn_g), row1(gn_b))


def _hgrn_kernel(q_ref, f_ref, i_ref, g_ref, lb_ref, ng_ref, o_ref, st_ref, *, rows):
    t = pl.program_id(2)
    sub = 16

    @pl.when(t == 0)
    def _():
        st_ref[...] = jnp.zeros_like(st_ref)

    q = jax.nn.silu(q_ref[0])
    lb = lb_ref[...]
    fg = lb + (1.0 - lb) * jax.nn.sigmoid(f_ref[0])
    lf = jnp.log(fg)
    kf = 1.0 - fg
    iv = i_ref[0]
    rowc = lax.broadcasted_iota(jnp.int32, lf.shape, 0) & (CHUNK - 1)
    bcum = lf
    for dd in (1, 2, 4, 8, 16, 32):
        bcum = bcum + jnp.where(rowc >= dd, pltpu.roll(bcum, dd, axis=0), 0.0)
    ivb = iv.astype(BF16)
    rsub = lax.broadcasted_iota(jnp.int32, (sub, LANE), 0)
    st = st_ref[...]
    outs = []
    for c in range(rows // CHUNK):
        c0 = c * CHUNK
        bc = bcum[c0:c0 + CHUNK]
        qc = q[c0:c0 + CHUNK]
        kc = kf[c0:c0 + CHUNK]
        ic = iv[c0:c0 + CHUNK]
        icb = ivb[c0:c0 + CHUNK]
        o_inter = _dot((qc * jnp.exp(bc)).astype(BF16), st.astype(BF16), NT_DIMS)
        blocks = []
        for ib in range(CHUNK // sub):
            r0 = ib * sub
            bi = bc[r0:r0 + sub]
            qi = qc[r0:r0 + sub]
            ki = kc[r0:r0 + sub]
            ii = ic[r0:r0 + sub]
            acc = jnp.zeros((sub, LANE), F32)
            for s_ in range(sub):
                e = jnp.where(rsub >= s_, jnp.exp(jnp.minimum(bi - bi[s_:s_ + 1], 0.0)), 0.0)
                a = jnp.sum(qi * (ki[s_:s_ + 1] * e), axis=-1, keepdims=True)
                acc = acc + a * ii[s_:s_ + 1]
            if ib > 0:
                ref_row = bc[r0 - 1:r0]
                qt = (qi * jnp.exp(bi - ref_row)).astype(BF16)
                kt = (kc[0:r0] * jnp.exp(ref_row - bc[0:r0])).astype(BF16)
                a_off = _dot(qt, kt, NT_DIMS).astype(BF16)
                acc = acc + _dot(a_off, icb[0:r0])
            blocks.append(acc)
        outs.append(o_inter + jnp.concatenate(blocks, axis=0))
        bl = bc[CHUNK - 1:CHUNK]
        kdec = (kc * jnp.exp(bl - bc)).astype(BF16)
        st = st * jnp.exp(bl) + _dot(icb, kdec, TN_DIMS)
    st_ref[...] = st
    o = jnp.concatenate(outs, axis=0)
    ms = jnp.mean(o * o, axis=-1, keepdims=True)
    o = o * lax.rsqrt(ms + EPS) * ng_ref[...]
    o_ref[0] = (o * jax.nn.silu(g_ref[0])).astype(o_ref.dtype)


def hgrn2(pm, lb, norm_g, rows=256):
    b, s, _ = pm.shape
    heads = D_BRANCH // LANE
    spec = lambda u: pl.BlockSpec((1, rows, LANE), lambda bi, h, t: (bi, t, u + h))
    par = pl.BlockSpec((1, LANE), lambda bi, h, t: (0, h))
    return pl.pallas_call(
        functools.partial(_hgrn_kernel, rows=rows),
        grid=(b, heads, s // rows),
        in_specs=[spec(U_CQ), spec(U_CF), spec(U_CI), spec(U_CG), par, par],
        out_specs=spec(0),
        out_shape=jax.ShapeDtypeStruct((b, s, D_BRANCH), BF16),
        scratch_shapes=[pltpu.VMEM((LANE, LANE), F32)],
        compiler_params=_cp(("parallel", "parallel", "arbitrary")),
        name="hgrn2",
    )(pm, pm, pm, pm, lb.reshape(1, -1), norm_g.reshape(1, -1))


def _dsa_prep_kernel(lo_ref, hi_ref, cos_ref, sin_ref,
                     qo_ref, iqo_ref, ko_ref, vo_ref, iko_ref, wo_ref, *, scale, wscale):
    cos = cos_ref[0]
    sin = sin_ref[0]
    hd = HEAD_DIM
    lo = lo_ref[0]
    hi = hi_ref[0]
    col = lambda u: (u - U_D_LO) * LANE
    q = (_rope(lo[:, col(U_DQ):col(U_DKV)], cos, sin) * scale).astype(BF16)
    iq_in = jnp.concatenate([lo[:, col(U_DIQ):], hi[:, :(U_DIKW - U_D_HI) * LANE]], axis=1)
    iq = _rope(iq_in, cos, sin).astype(BF16)
    for h in range(DSA_HEADS):
        qo_ref[0, h] = q[:, h * hd:(h + 1) * hd]
        iqo_ref[0, h] = iq[:, h * hd:(h + 1) * hd]
    kv = lo[:, col(U_DKV):col(U_DIQ)]
    ko_ref[0] = _rope(kv, cos, sin)[:, :hd].astype(BF16)
    vo_ref[0] = kv[:, hd:].astype(BF16)
    ikw = hi[:, (U_DIKW - U_D_HI) * LANE:]
    iko_ref[0] = _rope(ikw, cos, sin)[:, :hd].astype(BF16)
    wo_ref[0] = ikw[:, hd:hd + IDX_HEADS] * wscale


def _to_key(x):
    x = jnp.where(x == 0.0, 0.0, x)
    bits = pltpu.bitcast(x, jnp.int32)
    return jnp.where(bits < 0, bits ^ jnp.int32(0x7FFFFFFF), bits)


def _dsa_index_kernel(iq_ref, ik_ref, wt_ref, mask_ref, key_ref, half_ref, *, tq, tk, top_k):
    i = pl.program_id(1)
    nk = key_ref.shape[0]
    nvis = (i * tq + tq - 1) // tk + 1
    krow = lax.broadcasted_iota(jnp.int32, (tk, tq), 0)
    qcol = i * tq + lax.broadcasted_iota(jnp.int32, (tk, tq), 1)
    wt = wt_ref[0]

    def score_block(jb):
        ikb = ik_ref[0, pl.ds(pl.multiple_of(jb * tk, tk), tk), :]
        s_all = _dot(ikb, iq_ref[0].reshape(IDX_HEADS * tq, HEAD_DIM), NT_DIMS)
        acc = jnp.zeros((tk, tq), F32)
        for h in range(IDX_HEADS):
            acc = acc + jnp.maximum(s_all[:, h * tq:(h + 1) * tq], 0.0) * wt[h:h + 1, :]
        vis = ((jb * tk + krow) >> 6) <= (qcol >> 6)
        key = jnp.where(vis, _to_key(acc), INT_MIN)
        key_ref[jb] = key
        half_ref[jb] = (key >> 16).astype(jnp.int16)

    _paired_loop(nvis, score_block)

    def count(pred_fn):
        def blk(jb, cnt):
            ind = jnp.where(pred_fn(key_ref[jb], jb), 1, 0)
            return cnt + jnp.sum(ind.reshape(tk // 8, 8, tq), axis=0)
        cnt = lax.fori_loop(0, nvis, blk, jnp.zeros((8, tq), jnp.int32))
        return jnp.sum(cnt, axis=0, keepdims=True)

    one16 = jnp.ones((tk, tq), jnp.int16)
    zero16 = jnp.zeros((tk, tq), jnp.int16)

    def count16(pred_fn):
        def blk(jb, cnt):
            ind = jnp.where(pred_fn(half_ref[jb]), one16, zero16)
            parts = [ind[r * 16:(r + 1) * 16] for r in range(tk // 16)]
            while len(parts) > 1:
                parts = [a + b for a, b in zip(parts[::2], parts[1::2])]
            return cnt + parts[0]
        cnt = lax.fori_loop(0, nvis, blk, jnp.zeros((16, tq), jnp.int16))
        return jnp.sum(cnt.astype(jnp.int32), axis=0, keepdims=True)

    def kth_largest16(k_wanted):
        def bit_step(bi, prefix):
            cand = prefix | lax.shift_left(jnp.int32(1), 15 - bi)
            cand16 = (cand - 32768).astype(jnp.int16)
            c = count16(lambda hb: hb >= cand16)
            return jnp.where(c >= k_wanted, cand, prefix)
        return lax.fori_loop(0, 16, bit_step, jnp.zeros((1, tq), jnp.int32))

    tau_hi = kth_largest16(top_k) - 32768
    tau_hi16 = tau_hi.astype(jnp.int16)
    k_low = top_k - count16(lambda hb: hb > tau_hi16)

    def low_block(jb, carry):
        low = ((key_ref[jb] & 0xFFFF) - 32768).astype(jnp.int16)
        half_ref[jb] = jnp.where(half_ref[jb] == tau_hi16, low, jnp.int16(-32768))
        return carry

    lax.fori_loop(0, nvis, low_block, 0)
    tau = lax.shift_left(tau_hi, 16) | kth_largest16(k_low)
    n_gt = count(lambda kb, jb: kb > tau)
    n_eq = count(lambda kb, jb: kb == tau)
    need = top_k - n_gt
    tied = (n_eq > need) & (tau != INT_MIN)
    n_cols = nk * tk

    idx_bits = int(n_cols).bit_length()

    def tie_break():
        def idx_step(bi, pre):
            cand = pre | lax.shift_left(jnp.int32(1), idx_bits - 1 - bi)
            c = count(lambda kb, jb: (kb == tau) & ((jb * tk + krow) < cand))
            return jnp.where(c < need, cand, pre)
        cut = lax.fori_loop(0, idx_bits, idx_step, jnp.zeros((1, tq), jnp.int32))
        return jnp.where(tied, cut, n_cols)

    cut = lax.cond(jnp.max(tied.astype(jnp.int32)) > 0, tie_break,
                   lambda: jnp.full((1, tq), n_cols, jnp.int32))

    def write_block(jb, carry):
        kb = key_ref[jb]
        sel = (kb > tau) | ((kb == tau) & ((jb * tk + krow) <= cut))
        sel = sel & (kb != INT_MIN)
        mask_ref[0, jb] = jnp.where(sel, 1.0, 0.0).astype(mask_ref.dtype).T
        return carry

    def zero_block(jb, carry):
        mask_ref[0, jb] = jnp.zeros((tq, tk), mask_ref.dtype)
        return carry

    lax.fori_loop(0, nvis, write_block, 0)
    lax.fori_loop(nvis, nk, zero_block, 0)


def _dsa_attn_kernel(q_ref, k_ref, v_ref, mask_ref, o_ref, s_ref, m_ref, l_ref, acc_ref, *, tq, tk):
    i = pl.program_id(1)
    nh = DSA_HEADS
    hd = HEAD_DIM
    nvis = (i * tq + tq - 1) // tk + 1
    nchunk = tk // LANE

    m_ref[...] = jnp.full_like(m_ref, NEG)

    def max_step(j):
        kb = k_ref[0, pl.ds(pl.multiple_of(j * tk, tk), tk), :]
        s = _dot(q_ref[0].reshape(nh * tq, hd), kb, NT_DIMS).reshape(nh, tq, tk)
        s = jnp.where((mask_ref[0, j] > 0)[None], s, NEG)
        s_ref[j] = s
        m_ref[...] = jnp.maximum(m_ref[...], _lane_fold(s, jnp.maximum))

    _paired_loop(nvis, max_step)
    m_ref[...] = jnp.broadcast_to(jnp.max(m_ref[...], axis=2, keepdims=True), m_ref.shape)
    l_ref[...] = jnp.zeros_like(l_ref)
    acc_ref[...] = jnp.zeros_like(acc_ref)

    def acc_step(j):
        s = s_ref[j]
        mb = m_ref[...]
        ps = [jnp.exp2(s[:, :, c * LANE:(c + 1) * LANE] - mb) for c in range(nchunk)]
        l_ref[...] += functools.reduce(jnp.add, ps)
        p = jnp.concatenate(ps, axis=2).astype(BF16).reshape(nh * tq, tk)
        vb = v_ref[0, pl.ds(pl.multiple_of(j * tk, tk), tk), :]
        acc_ref[...] += _dot(p, vb).reshape(nh, tq, hd)

    _paired_loop(nvis, acc_step)
    o = acc_ref[...] / jnp.sum(l_ref[...], axis=2, keepdims=True)
    o_ref[0] = jnp.concatenate([o[h] for h in range(nh)], axis=1).astype(o_ref.dtype)


def dsa_attention(pm, cos, sin, ts=512, tq=128, tk=512):
    b, s, _ = pm.shape
    top_k = min(TOPK_MAX, s // 4)
    hd = HEAD_DIM
    nk = s // tk
    tab = pl.BlockSpec((1, ts, LANE), lambda bi, i: (bi, i, 0))
    heads_out = pl.BlockSpec((1, DSA_HEADS, ts, hd), lambda bi, i: (bi, 0, i, 0))
    narrow = lambda w: pl.BlockSpec((1, ts, w), lambda bi, i: (bi, i, 0))
    w_lo = (U_D_HI - U_D_LO) * LANE
    w_hi = N_MAIN - U_D_HI * LANE
    q, iq, k, v, ik, w = pl.pallas_call(
        functools.partial(_dsa_prep_kernel, scale=hd ** -0.5 * LOG2E, wscale=(IDX_HEADS ** -0.5) * (hd ** -0.5)),
        grid=(b, s // ts),
        in_specs=[pl.BlockSpec((1, ts, w_lo), lambda bi, i: (bi, i, U_D_LO * LANE // w_lo)),
                  pl.BlockSpec((1, ts, w_hi), lambda bi, i: (bi, i, U_D_HI * LANE // w_hi)),
                  tab, tab],
        out_specs=[heads_out, heads_out, narrow(hd), narrow(hd), narrow(hd), narrow(IDX_HEADS)],
        out_shape=[jax.ShapeDtypeStruct((b, DSA_HEADS, s, hd), BF16)] * 2
                  + [jax.ShapeDtypeStruct((b, s, hd), BF16)] * 3
                  + [jax.ShapeDtypeStruct((b, s, IDX_HEADS), F32)],
        compiler_params=_cp(("parallel", "parallel")),
        name="dsa_prep",
    )(pm, pm, cos, sin)

    mask_spec = pl.BlockSpec((1, nk, tq, tk), lambda bi, i: (bi, 0, i, 0))
    whole = pl.BlockSpec((1, s, hd), lambda bi, i: (bi, 0, 0))
    mask = pl.pallas_call(
        functools.partial(_dsa_index_kernel, tq=tq, tk=tk, top_k=top_k),
        grid=(b, s // tq),
        in_specs=[pl.BlockSpec((1, IDX_HEADS, tq, hd), lambda bi, i: (bi, 0, i, 0)),
                  whole,
                  pl.BlockSpec((1, IDX_HEADS, tq), lambda bi, i: (bi, 0, i))],
        out_specs=mask_spec,
        out_shape=jax.ShapeDtypeStruct((b, nk, s, tk), BF16),
        scratch_shapes=[pltpu.VMEM((nk, tk, tq), jnp.int32), pltpu.VMEM((nk, tk, tq), jnp.int16)],
        compiler_params=_cp(("parallel", "parallel")),
        name="dsa_index",
    )(iq, ik, jnp.swapaxes(w, 1, 2))

    return pl.pallas_call(
        functools.partial(_dsa_attn_kernel, tq=tq, tk=tk),
        grid=(b, s // tq),
        in_specs=[pl.BlockSpec((1, DSA_HEADS, tq, hd), lambda bi, i: (bi, 0, i, 0)),
                  whole, whole, mask_spec],
        out_specs=pl.BlockSpec((1, tq, D_BRANCH), lambda bi, i: (bi, i, 0)),
        out_shape=jax.ShapeDtypeStruct((b, s, D_BRANCH), BF16),
        scratch_shapes=[pltpu.VMEM((nk, DSA_HEADS, tq, tk), F32),
                        pltpu.VMEM((DSA_HEADS, tq, LANE), F32), pltpu.VMEM((DSA_HEADS, tq, LANE), F32),
                        pltpu.VMEM((DSA_HEADS, tq, hd), F32)],
        compiler_params=_cp(("parallel", "parallel")),
        name="dsa_attention",
    )(q, k, v, mask)


GATE_COL = 6600


def kernel(x, positions, norm_g, w_in, diff_lambda, diff_subln_g, rwkv_mu, rwkv_w_up, rwkv_a_up, rwkv_g_up, rwkv_w0, rwkv_a0, rwkv_k_k, rwkv_k_a, rwkv_r_k, rwkv_gn_g, rwkv_gn_b, hgrn_lb_logits, hgrn_norm_g, w_branch, w_out, mlp_w1, mlp_w2):
    b, s, d = x.shape
    m = b * s
    depth = w_in.shape[0]
    lb = jax.nn.softmax(hgrn_lb_logits.astype(F32), axis=0)
    lb = jnp.cumsum(lb, axis=0) - lb[0]
    w_in_t = jnp.swapaxes(w_in, 1, 2)
    w_out_b = w_out.astype(BF16)
    w2_b = mlp_w2.astype(BF16)
    cos, sin = rope_tables(positions)
    cos3 = cos.reshape(b, s, LANE)
    sin3 = sin.reshape(b, s, LANE)
    x2 = x.reshape(m, d)
    h = rmsnorm_bf16(x2, norm_g[0, 0])
    for l in range(depth):
        pm = matmul_wt(h, w_in_t, l, 0, N_MAIN, None, F32, name="proj_main").reshape(b, s, N_MAIN)
        gate = matmul_wt(h, w_in_t, l, GATE_COL, N_BRANCH * D_MODEL, "sigmoid", BF16, name="proj_gate")
        y_a = diff_attention(pm, cos3, sin3, diff_lambda[l], diff_subln_g[l], l)
        y_b = rwkv7(pm, rwkv_mu[l], rwkv_w_up[l], rwkv_a_up[l], rwkv_g_up[l], rwkv_w0[l], rwkv_a0[l],
                    rwkv_k_k[l], rwkv_k_a[l], rwkv_r_k[l], rwkv_gn_g[l], rwkv_gn_b[l])
        y_c = hgrn2(pm, lb[l], hgrn_norm_g[l])
        y_d = dsa_attention(pm, cos3, sin3)
        ys = [y.reshape(m, D_BRANCH) for y in (y_a, y_b, y_c, y_d)]
        merged = gated_merge(ys, w_branch, l, gate)
        x2, h2 = matmul_norm_residual(merged, w_out_b, l, x2, norm_g[l, 1], norm_g[l, 2], name="out_proj")
        ff = matmul(h2, mlp_w1, l, "relu2", BF16, name="mlp_up")
        if l + 1 < depth:
            x2, h = matmul_norm_residual(ff, w2_b, l, x2, norm_g[l, 3], norm_g[l + 1, 0], name="mlp_down")
        else:
            x2 = matmul_norm_residual(ff, w2_b, l, x2, norm_g[l, 3], name="mlp_down")
    return x2.reshape(b, s, d)
```

```python
import functools
import math

import numpy as np
import jax
import jax.numpy as jnp
from jax import lax
from jax.experimental import pallas as pl
from jax.experimental.pallas import tpu as pltpu

F32 = jnp.float32
BF16 = jnp.bfloat16

D_MODEL = 2048
D_BRANCH = 512
D_FF = 8192
N_BRANCH = 4
CHUNK = 64
EPS = 1e-6
ROPE_THETA = 10000.0
HEAD_DIM = 64
DIFF_HEADS = 4
RWKV_GN_EPS = 64e-5
DSA_HEADS = 8
IDX_HEADS = 8
TOPK_MAX = 256

LANE = 128
VMEM_LIMIT = 56 * 1024 * 1024

U_AQ, U_AK, U_AV = 0, 4, 8
U_B = 12
U_CQ, U_CF, U_CI, U_CG = 26, 30, 34, 38
U_DQ, U_DKV, U_DIQ, U_DIKW = 42, 46, 47, 51
U_D_LO, U_D_HI = 40, 48
N_MAIN = 52 * LANE
B_WIDTH = 1792
B_BLOCK = 3584

NEG = -1e30
INT_MIN = -2147483648
LOG2E = 1.4426950408889634

NT_DIMS = (((1,), (1,)), ((), ()))
TN_DIMS = (((0,), (0,)), ((), ()))


def _cp(sem, vmem=VMEM_LIMIT):
    return pltpu.CompilerParams(dimension_semantics=sem, vmem_limit_bytes=vmem)


def _dot(a, b, dims=None):
    if dims is None:
        return jnp.dot(a, b, preferred_element_type=F32)
    return lax.dot_general(a, b, dims, preferred_element_type=F32)


def _split2(a):
    hi = a.astype(BF16)
    return hi, (a - hi.astype(F32)).astype(BF16)


def _dot_split(a, b):
    a_hi, a_lo = _split2(a)
    b_hi, b_lo = _split2(b)
    return _dot(a_hi, b_hi) + (_dot(a_hi, b_lo) + _dot(a_lo, b_hi))


def _rmsnorm_kernel(x_ref, g_ref, o_ref):
    x = x_ref[...]
    ms = jnp.mean(x * x, axis=-1, keepdims=True)
    o_ref[...] = (x * lax.rsqrt(ms + EPS) * g_ref[...]).astype(o_ref.dtype)


def rmsnorm_bf16(x, g, tm=512):
    m, d = x.shape
    return pl.pallas_call(
        _rmsnorm_kernel,
        grid=(m // tm,),
        in_specs=[pl.BlockSpec((tm, d), lambda i: (i, 0)),
                  pl.BlockSpec((1, d), lambda i: (0, 0))],
        out_specs=pl.BlockSpec((tm, d), lambda i: (i, 0)),
        out_shape=jax.ShapeDtypeStruct((m, d), BF16),
        compiler_params=_cp(("parallel",)),
        name="rmsnorm",
    )(x, g.reshape(1, d))


def _mm_kernel(x_ref, w_ref, o_ref, *, act, dims):
    w = w_ref[...] if len(w_ref.shape) == 2 else w_ref[0]
    acc = _dot(x_ref[...], w.astype(BF16), dims)
    if act == "sigmoid":
        acc = 0.5 * jnp.tanh(0.5 * acc) + 0.5
    elif act == "relu2":
        acc = jnp.square(jnp.maximum(acc, 0.0))
    o_ref[...] = acc.astype(o_ref.dtype)


def matmul(x, w, layer, act, out_dtype, tm=2048, tn=512, name="mm"):
    m, k = x.shape
    n = w.shape[2]
    tm = min(tm, m)
    return pl.pallas_call(
        functools.partial(_mm_kernel, act=act, dims=None),
        grid=(m // tm, n // tn),
        in_specs=[pl.BlockSpec((tm, k), lambda i, j: (i, 0)),
                  pl.BlockSpec((None, k, tn), lambda i, j: (layer, 0, j))],
        out_specs=pl.BlockSpec((tm, tn), lambda i, j: (i, j)),
        out_shape=jax.ShapeDtypeStruct((m, n), out_dtype),
        compiler_params=_cp(("parallel", "parallel")),
        name=name,
    )(x, w)


def matmul_wt(x, wt, layer, row0, n, act, out_dtype, tm=2048, tn=512, name="mm_wt"):
    m, k = x.shape
    tm = min(tm, m)
    return pl.pallas_call(
        functools.partial(_mm_kernel, act=act, dims=NT_DIMS),
        grid=(m // tm, n // tn),
        in_specs=[pl.BlockSpec((tm, k), lambda i, j: (i, 0)),
                  pl.BlockSpec((pl.Element(1), pl.Element(tn), pl.Element(k)),
                               lambda i, j: (layer, pl.multiple_of(row0 + j * tn, 8), 0))],
        out_specs=pl.BlockSpec((tm, tn), lambda i, j: (i, j)),
        out_shape=jax.ShapeDtypeStruct((m, n), out_dtype),
        compiler_params=_cp(("parallel", "parallel")),
        name=name,
    )(x, wt)


def _norm_residual(y, x_ref, g_ref, o_ref, h_ref):
    ms = jnp.mean(y * y, axis=-1, keepdims=True)
    o = x_ref[...] + y * lax.rsqrt(ms + EPS) * g_ref[0:1, :]
    o_ref[...] = o
    if h_ref is not None:
        ms = jnp.mean(o * o, axis=-1, keepdims=True)
        h_ref[...] = (o * lax.rsqrt(ms + EPS) * g_ref[1:2, :]).astype(h_ref.dtype)


def _mm_norm_res_kernel(a_ref, w_ref, x_ref, g_ref, o_ref, *rest):
    h_ref, acc_ref = (rest[0], rest[1]) if len(rest) == 2 else (None, rest[0])
    kk = pl.program_id(1)

    @pl.when(kk == 0)
    def _():
        acc_ref[...] = jnp.zeros_like(acc_ref)

    acc_ref[...] += _dot(a_ref[...], w_ref[...])

    @pl.when(kk == pl.num_programs(1) - 1)
    def _():
        _norm_residual(acc_ref[...], x_ref, g_ref, o_ref, h_ref)


def _mm_norm_res_fullk_kernel(a_ref, w_ref, x_ref, g_ref, o_ref, *rest):
    _norm_residual(_dot(a_ref[...], w_ref[...]), x_ref, g_ref, o_ref, rest[0] if rest else None)


def matmul_norm_residual(a, w, layer, x, g, g_next=None, tm=512, tk=2048, name="mm_norm_res"):
    m, k = a.shape
    n = w.shape[2]
    gs = jnp.stack([g, g if g_next is None else g_next])
    out_shape = [jax.ShapeDtypeStruct((m, n), F32)]
    if g_next is not None:
        out_shape.append(jax.ShapeDtypeStruct((m, n), BF16))
    if k <= tk:
        row = lambda i: (i, 0)
        outs = pl.pallas_call(
            _mm_norm_res_fullk_kernel,
            grid=(m // tm,),
            in_specs=[pl.BlockSpec((tm, k), row), pl.BlockSpec((None, k, n), lambda i: (layer, 0, 0)),
                      pl.BlockSpec((tm, n), row), pl.BlockSpec((2, n), lambda i: (0, 0))],
            out_specs=[pl.BlockSpec((tm, n), row)] * len(out_shape),
            out_shape=out_shape,
            compiler_params=_cp(("parallel",)),
            name=name,
        )(a, w, x, gs)
    else:
        row = lambda i, kk: (i, 0)
        outs = pl.pallas_call(
            _mm_norm_res_kernel,
            grid=(m // tm, k // tk),
            in_specs=[pl.BlockSpec((tm, tk), lambda i, kk: (i, kk)),
                      pl.BlockSpec((None, tk, n), lambda i, kk: (layer, kk, 0)),
                      pl.BlockSpec((tm, n), row),
                      pl.BlockSpec((2, n), lambda i, kk: (0, 0))],
            out_specs=[pl.BlockSpec((tm, n), row)] * len(out_shape),
            out_shape=out_shape,
            scratch_shapes=[pltpu.VMEM((tm, n), F32)],
            compiler_params=_cp(("parallel", "arbitrary")),
            name=name,
        )(a, w, x, gs)
    return outs if g_next is not None else outs[0]


def _merge_kernel(ya, yb, yc, yd, wb, ga, gb, gc, gd, o_ref):
    acc = None
    for n, (y, g) in enumerate(((ya, ga), (yb, gb), (yc, gc), (yd, gd))):
        t = g[...].astype(F32) * _dot(y[...], wb[n].astype(BF16))
        acc = t if acc is None else acc + t
    o_ref[...] = acc.astype(o_ref.dtype)


def gated_merge(ys, wb, layer, gate, tm=512, tn=1024):
    m = ys[0].shape[0]
    nj = D_MODEL // tn
    y_spec = pl.BlockSpec((tm, D_BRANCH), lambda j, i: (i, 0))
    g_specs = [pl.BlockSpec((tm, tn), functools.partial(lambda j, i, n: (i, n * nj + j), n=n))
               for n in range(N_BRANCH)]
    return pl.pallas_call(
        _merge_kernel,
        grid=(nj, m // tm),
        in_specs=[y_spec] * 4 + [pl.BlockSpec((None, N_BRANCH, D_BRANCH, tn), lambda j, i: (layer, 0, 0, j))]
                 + g_specs,
        out_specs=pl.BlockSpec((tm, tn), lambda j, i: (i, j)),
        out_shape=jax.ShapeDtypeStruct((m, D_MODEL), BF16),
        compiler_params=_cp(("parallel", "parallel")),
        name="gated_merge",
    )(*ys, wb, gate, gate, gate, gate)


def _rope_table_kernel(pos_ref, inv_ref, sgn_ref, cos_ref, sin_ref):
    ang = pos_ref[...].astype(F32) * inv_ref[...]
    cos_ref[...] = jnp.cos(ang)
    sin_ref[...] = jnp.sin(ang) * sgn_ref[...]


def rope_tables(positions, tm=512):
    m = positions.size
    half = HEAD_DIM // 2
    inv = ROPE_THETA ** (-np.arange(0, HEAD_DIM, 2, dtype=np.float32) / HEAD_DIM)
    inv = np.tile(inv.astype(np.float32), 4).reshape(1, LANE)
    sgn = np.tile(np.concatenate([-np.ones(half, np.float32), np.ones(half, np.float32)]), 2).reshape(1, LANE)
    spec = pl.BlockSpec((tm, LANE), lambda i: (i, 0))
    cst = pl.BlockSpec((1, LANE), lambda i: (0, 0))
    return pl.pallas_call(
        _rope_table_kernel,
        grid=(m // tm,),
        in_specs=[pl.BlockSpec((tm, 1), lambda i: (i, 0)), cst, cst],
        out_specs=[spec, spec],
        out_shape=[jax.ShapeDtypeStruct((m, LANE), F32)] * 2,
        compiler_params=_cp(("parallel",)),
        name="rope_tables",
    )(positions.reshape(m, 1), jnp.asarray(inv), jnp.asarray(sgn))


def _rope(x, cos, sin):
    w = x.shape[1]
    n = w // LANE
    if n > 1:
        cos = jnp.concatenate([cos] * n, axis=1)
        sin = jnp.concatenate([sin] * n, axis=1)
    lane = lax.broadcasted_iota(jnp.int32, x.shape, 1)
    up = pltpu.roll(x, w - HEAD_DIM // 2, axis=1)
    dn = pltpu.roll(x, HEAD_DIM // 2, axis=1)
    rot = jnp.where((lane & (HEAD_DIM // 2)) == 0, up, dn)
    return x * cos + rot * sin


def _rope_qkv_kernel(q_ref, k_ref, v_ref, cos_ref, sin_ref, qo_ref, ko_ref, vo_ref, *, scale):
    cos = cos_ref[0]
    sin = sin_ref[0]
    hd = HEAD_DIM
    q = (_rope(q_ref[0], cos, sin) * scale).astype(qo_ref.dtype)
    k = _rope(k_ref[0], cos, sin).astype(ko_ref.dtype)
    for h in range(q.shape[1] // hd):
        qo_ref[0, h] = q[:, h * hd:(h + 1) * hd]
        ko_ref[0, h] = k[:, h * hd:(h + 1) * hd]
    vo_ref[0] = v_ref[0].astype(vo_ref.dtype)


def rope_qkv(pm, cos, sin, uq, uk, uv, scale, ts=512):
    b, s, _ = pm.shape
    wq = D_BRANCH
    nh = wq // HEAD_DIM
    tab = pl.BlockSpec((1, ts, LANE), lambda bi, i: (bi, i, 0))
    col = lambda u: pl.BlockSpec((1, ts, wq), lambda bi, i: (bi, i, u * LANE // wq))
    heads = pl.BlockSpec((1, nh, ts, HEAD_DIM), lambda bi, i: (bi, 0, i, 0))
    return pl.pallas_call(
        functools.partial(_rope_qkv_kernel, scale=scale),
        grid=(b, s // ts),
        in_specs=[col(uq), col(uk), col(uv), tab, tab],
        out_specs=[heads, heads, pl.BlockSpec((1, ts, wq), lambda bi, i: (bi, i, 0))],
        out_shape=[jax.ShapeDtypeStruct((b, nh, s, HEAD_DIM), BF16)] * 2
                  + [jax.ShapeDtypeStruct((b, s, wq), BF16)],
        compiler_params=_cp(("parallel", "parallel")),
        name="rope_qkv",
    )(pm, pm, pm, cos, sin)


def _lane_fold(x, op):
    out = x[..., 0:LANE]
    for c in range(1, x.shape[-1] // LANE):
        out = op(out, x[..., c * LANE:(c + 1) * LANE])
    return out


def _paired_loop(n, step):
    def body(jj, carry):
        step(2 * jj)
        step(2 * jj + 1)
        return carry
    lax.fori_loop(0, n // 2, body, 0)

    @pl.when(n % 2 == 1)
    def _():
        step(n - 1)


def _diff_attn_kernel(q_ref, k_ref, v_ref, lam_ref, g_ref, o_ref, s_ref, m_ref, l_ref, acc_ref, *, tq, tk, lam_init):
    i = pl.program_id(2)
    nfull = (i * tq) // tk
    row = lax.broadcasted_iota(jnp.int32, (tq, tk), 0)
    col = lax.broadcasted_iota(jnp.int32, (tq, tk), 1)
    vis = (col >> 6) <= (row >> 6)

    m_ref[...] = jnp.full_like(m_ref, NEG)

    def max_step(j, masked=False):
        for mm in range(2):
            kb = k_ref[0, mm, pl.ds(pl.multiple_of(j * tk, tk), tk), :]
            s = _dot(q_ref[0, mm], kb, NT_DIMS)
            if masked:
                s = jnp.where(vis, s, NEG)
            s_ref[j, mm] = s
            m_ref[mm] = jnp.maximum(m_ref[mm], _lane_fold(s, jnp.maximum))

    _paired_loop(nfull, max_step)
    max_step(nfull, True)
    for mm in range(2):
        m_ref[mm] = jnp.broadcast_to(jnp.max(m_ref[mm], axis=1, keepdims=True), (tq, LANE))
    l_ref[...] = jnp.zeros_like(l_ref)
    acc_ref[...] = jnp.zeros_like(acc_ref)

    def acc_step(j):
        vb = v_ref[0, pl.ds(pl.multiple_of(j * tk, tk), tk), :]
        for mm in range(2):
            s = s_ref[j, mm]
            mb = m_ref[mm]
            ps = [jnp.exp2(s[:, c * LANE:(c + 1) * LANE] - mb) for c in range(tk // LANE)]
            l_ref[mm] += functools.reduce(jnp.add, ps)
            acc_ref[mm] += _dot(jnp.concatenate(ps, axis=1).astype(BF16), vb)

    _paired_loop(nfull + 1, acc_step)

    lv = lam_ref[...]
    lam = (jnp.exp(jnp.sum(lv[0:1] * lv[1:2], keepdims=True))
           - jnp.exp(jnp.sum(lv[2:3] * lv[3:4], keepdims=True)) + lam_init)
    l0 = jnp.sum(l_ref[0], axis=1, keepdims=True)
    l1 = jnp.sum(l_ref[1], axis=1, keepdims=True)
    o = acc_ref[0] / l0 - lam * (acc_ref[1] / l1)
    ms = jnp.mean(o * o, axis=-1, keepdims=True)
    o = o * lax.rsqrt(ms + EPS) * g_ref[...] * (1.0 - lam_init)
    o_ref[0] = o.astype(o_ref.dtype)


def diff_attention(pm, cos, sin, lam_vecs, subln_g, layer, tq=512):
    b, s, _ = pm.shape
    tk = tq
    qa, ka, va = rope_qkv(pm, cos, sin, U_AQ, U_AK, U_AV, HEAD_DIM ** -0.5 * LOG2E)
    lam_init = 0.8 - 0.6 * math.exp(-0.3 * layer)
    return pl.pallas_call(
        functools.partial(_diff_attn_kernel, tq=tq, tk=tk, lam_init=lam_init),
        grid=(b, DIFF_HEADS, s // tq),
        in_specs=[pl.BlockSpec((1, 2, tq, HEAD_DIM), lambda bi, h, i: (bi, h, i, 0)),
                  pl.BlockSpec((1, 2, s, HEAD_DIM), lambda bi, h, i: (bi, h, 0, 0)),
                  pl.BlockSpec((1, s, LANE), lambda bi, h, i: (bi, 0, h)),
                  pl.BlockSpec((4, HEAD_DIM), lambda bi, h, i: (0, 0)),
                  pl.BlockSpec((1, LANE), lambda bi, h, i: (0, 0))],
        out_specs=pl.BlockSpec((1, tq, LANE), lambda bi, h, i: (bi, i, h)),
        out_shape=jax.ShapeDtypeStruct((b, s, D_BRANCH), BF16),
        scratch_shapes=[pltpu.VMEM((s // tk, 2, tq, tk), F32),
                        pltpu.VMEM((2, tq, LANE), F32), pltpu.VMEM((2, tq, LANE), F32),
                        pltpu.VMEM((2, tq, LANE), F32)],
        compiler_params=_cp(("parallel", "parallel", "parallel")),
        name="diff_attention",
    )(qa, ka, va, lam_vecs, subln_g.reshape(1, LANE))


def _softplus(z):
    return jnp.maximum(z, 0.0) + jnp.log(1.0 + jnp.exp(-jnp.abs(z)))


def _rwkv_prep_kernel(x_ref, prev_ref, mu_ref, wup_ref, aup_ref, gup_ref, w0_ref, a0_ref, kkg_ref, kag_ref,
                      seg_ref, r_ref, k_ref, v_ref, al_ref, be_ref, lw_ref, g_ref):
    i = pl.program_id(1)
    c0 = U_B * LANE
    p = x_ref[0, :, c0:c0 + B_WIDTH]
    row = lax.broadcasted_iota(jnp.int32, p.shape, 0)
    last = prev_ref[0, 7:8, c0:c0 + B_WIDTH]
    last = jnp.where(i == 0, jnp.zeros_like(last), last)
    prev = jnp.where(row == 0, jnp.broadcast_to(last, p.shape), pltpu.roll(p, 1, axis=0))
    ps = p + (prev - p) * mu_ref[...]
    d = D_BRANCH
    r, k, v = ps[:, 0:d], ps[:, d:2 * d], ps[:, 2 * d:3 * d]
    wd = ps[:, 3 * d:3 * d + 64]
    ad = ps[:, 3 * d + 64:3 * d + 128]
    gd = ps[:, 3 * d + 128:3 * d + 256]
    w_log = -_softplus(-(w0_ref[...] + _dot_split(jnp.tanh(wd), wup_ref[...]))) - 0.5
    a = jax.nn.sigmoid(a0_ref[...] + _dot_split(ad, aup_ref[...]))
    g = _dot_split(jax.nn.sigmoid(gd), gup_ref[...])
    kk = k * kkg_ref[...]
    sq_hi, sq_lo = _split2(kk * kk)
    nrm = jnp.sqrt(_dot(sq_hi, seg_ref[...]) + _dot(sq_lo, seg_ref[...]))
    kk = kk / jnp.maximum(nrm, 1e-12)
    r_ref[0] = r
    k_ref[0] = k * (1.0 + (a - 1.0) * kag_ref[...])
    v_ref[0] = v
    al_ref[0] = -kk
    be_ref[0] = kk * a
    lw_ref[0] = -jnp.exp(w_log)
    g_ref[0] = g


def _bdot(a, b, dims):
    return lax.dot_general(a, b, ((dims[0], dims[1]), ((0,), (0,))), preferred_element_type=F32)


BNN = ((2,), (1,))
BNT = ((2,), (2,))


def _rwkv_chunk_kernel(r_ref, k_ref, v_ref, al_ref, be_ref, lw_ref, q_ref, en_ref, wl_ref, *, rows, sub):
    hd = HEAD_DIM
    nch = rows // CHUNK
    nsb = rows // sub
    lw = lw_ref[0]
    rowc = lax.broadcasted_iota(jnp.int32, lw.shape, 0) & (CHUNK - 1)
    cum = lw
    for dd in (1, 2, 4, 8, 16, 32):
        cum = cum + jnp.where(rowc >= dd, pltpu.roll(cum, dd, axis=0), 0.0)
    cl = jnp.concatenate(
        [jnp.broadcast_to(cum[(c + 1) * CHUNK - 1:(c + 1) * CHUNK, :], (CHUNK, LANE)) for c in range(nch)], axis=0)
    e_in = jnp.exp(cum)
    e_out = jnp.exp(-cum)
    e_end = jnp.exp(cl - cum)
    r2, k2, v2, al2, be2 = r_ref[0], k_ref[0], v_ref[0], al_ref[0], be_ref[0]
    at2 = al2 * jnp.exp(cum - lw)
    rt2 = r2 * e_in
    bt2 = be2 * e_out
    kt2 = k2 * e_out
    bh2 = be2 * e_end
    kh2 = k2 * e_end
    wl2 = jnp.exp(cl)

    row = lax.broadcasted_iota(jnp.int32, (sub, sub), 0)
    col = lax.broadcasted_iota(jnp.int32, (sub, sub), 1)
    same = (row >> 6) == (col >> 6)
    m_strict = same & (row > col)
    m_incl = same & (row >= col)
    eye = (row == col).astype(F32)
    m_blk8 = (row >> 3) == (col >> 3)
    m_lvls = [((row >> (sh + 1)) == (col >> (sh + 1))) & ((row >> sh) != (col >> sh)) for sh in (3, 4, 5)]
    wl_ref[0] = jnp.concatenate([wl2[c * CHUNK:c * CHUNK + 1, :] for c in range(nch)], axis=0)

    def stack(a):
        return jnp.stack([a[sb * sub:(sb + 1) * sub, hh * hd:(hh + 1) * hd]
                          for hh in range(2) for sb in range(nsb)])

    at, rt, vb = stack(at2), stack(rt2), stack(v2).astype(BF16)
    lhs = jnp.concatenate([at, rt], axis=1).astype(BF16)
    rhs = jnp.concatenate([stack(bt2), stack(kt2)], axis=1).astype(BF16)
    gm = _bdot(lhs, rhs, BNT)
    n_ab = jnp.where(m_strict[None], gm[:, :sub, :sub], 0.0)
    a_ak = jnp.where(m_strict[None], gm[:, :sub, sub:], 0.0).astype(BF16)
    a_rb = jnp.where(m_incl[None], gm[:, sub:, :sub], 0.0).astype(BF16)
    a_rk = jnp.where(m_incl[None], gm[:, sub:, sub:], 0.0).astype(BF16)
    n8 = jnp.where(m_blk8[None], n_ab, 0.0)
    n8b = n8.astype(BF16)
    n_2 = _bdot(n8b, n8b, BNN)
    n2b = n_2.astype(BF16)
    n_3 = _bdot(n2b, n8b, BNN)
    n_4 = _bdot(n2b, n2b, BNN)
    tinv = eye[None] + n8 + n_2 + n_3
    tinv = tinv + _bdot(tinv.astype(BF16), n_4.astype(BF16), BNN)
    for m_lvl in m_lvls:
        nl = jnp.where(m_lvl[None], n_ab, 0.0).astype(BF16)
        tb = tinv.astype(BF16)
        tinv = tinv + _bdot(_bdot(tb, nl, BNN).astype(BF16), tb, BNN)
    tb = tinv.astype(BF16)
    akv = _bdot(a_ak, vb, BNN)
    pmat = _bdot(tb, jnp.concatenate([at, akv], axis=2).astype(BF16), BNN)
    qmat = _bdot(a_rb, pmat.astype(BF16), BNN) + jnp.concatenate([rt, _bdot(a_rk, vb, BNN)], axis=2)
    pb = pmat.astype(BF16)
    bhb = stack(bh2).astype(BF16)
    khb = stack(kh2).astype(BF16)
    for hh in range(2):
        for sb in range(nsb):
            bi = hh * nsb + sb
            q_ref[0, hh, sb * sub:(sb + 1) * sub, :] = qmat[bi]
            for cc in range(sub // CHUNK):
                cs = slice(cc * CHUNK, (cc + 1) * CHUNK)
                mn = _dot(pb[bi, cs], bhb[bi, cs], TN_DIMS)
                n_c = mn[hd:] + _dot(vb[bi, cs], khb[bi, cs], TN_DIMS)
                en_ref[0, hh, sb * (sub // CHUNK) + cc] = jnp.concatenate([mn[:hd], n_c], axis=0)


def _rwkv_state_kernel(q_ref, en_ref, wl_ref, r_ref, k_ref, v_ref, g_ref, rk_ref, gng_ref, gnb_ref,
                       o_ref, st_ref, *, rows):
    t = pl.program_id(2)
    hd = HEAD_DIM
    nch = rows // CHUNK

    @pl.when(t == 0)
    def _():
        st_ref[...] = jnp.zeros_like(st_ref)

    wl = wl_ref[0]
    r2, k2, v2, g2 = r_ref[0], k_ref[0], v_ref[0], g_ref[0]
    sts = [st_ref[0], st_ref[1]]
    ys = [[], []]
    for c in range(nch):
        for hh in range(2):
            qc = q_ref[0, hh, c * CHUNK:(c + 1) * CHUNK, :]
            en = en_ref[0, hh, c]
            stb = sts[hh].astype(BF16)
            ys[hh].append(_dot(qc[:, :hd].astype(BF16), stb, NT_DIMS) + qc[:, hd:])
            sts[hh] = (sts[hh] * wl[c:c + 1, hh * hd:(hh + 1) * hd]
                       + _dot(stb, en[:hd].astype(BF16)) + en[hd:])
    outs = []
    for hh in range(2):
        sl = slice(hh * hd, (hh + 1) * hd)
        st_ref[hh] = sts[hh]
        v, r, kp = v2[:, sl], r2[:, sl], k2[:, sl]
        y = jnp.concatenate(ys[hh], axis=0)
        mu = jnp.mean(y, axis=-1, keepdims=True)
        var = jnp.mean(jnp.square(y - mu), axis=-1, keepdims=True)
        yn = (y - mu) * lax.rsqrt(var + RWKV_GN_EPS) * gng_ref[:, sl] + gnb_ref[:, sl]
        bonus = jnp.sum(r * kp * rk_ref[0, hh:hh + 1, :], axis=-1, keepdims=True) * v
        outs.append((yn + bonus) * g2[:, sl])
    o_ref[0] = jnp.concatenate(outs, axis=1).astype(o_ref.dtype)


def rwkv7(pm, mu, w_up, a_up, g_up, w0, a0, kk_gain, ka_gain, r_k, gn_g, gn_b, ts=256, rows=512, sub=256):
    b, s, _ = pm.shape
    d = D_BRANCH
    seg = np.kron(np.eye(d // HEAD_DIM, dtype=np.float32), np.ones((HEAD_DIM, HEAD_DIM), np.float32))
    row1 = lambda a: a.reshape(1, -1)
    cst = lambda shape: pl.BlockSpec(shape, lambda bi, i: (0,) * len(shape))
    blk = pl.BlockSpec((1, ts, d), lambda bi, i: (bi, i, 0))
    r, k, v, al, be, lw, g = pl.pallas_call(
        _rwkv_prep_kernel,
        grid=(b, s // ts),
        in_specs=[pl.BlockSpec((1, ts, B_BLOCK), lambda bi, i: (bi, i, 0)),
                  pl.BlockSpec((1, 8, B_BLOCK), lambda bi, i: (bi, jnp.maximum(i * (ts // 8) - 1, 0), 0)),
                  cst((1, B_WIDTH)), cst((64, d)), cst((64, d)), cst((128, d)),
                  cst((1, d)), cst((1, d)), cst((1, d)), cst((1, d)), cst((d, d))],
        out_specs=[blk] * 7,
        out_shape=[jax.ShapeDtypeStruct((b, s, d), F32)] * 7,
        compiler_params=_cp(("parallel", "arbitrary")),
        name="rwkv_prep",
    )(pm, pm, row1(mu), w_up, a_up, g_up, row1(w0), row1(a0), row1(kk_gain), row1(ka_gain), jnp.asarray(seg, BF16))

    hp = pl.BlockSpec((1, rows, LANE), lambda bi, h, t: (bi, t, h))
    par = pl.BlockSpec((1, LANE), lambda bi, h, t: (0, h))
    nh = d // HEAD_DIM
    nch = rows // CHUNK
    q_spec = pl.BlockSpec((1, 2, rows, LANE), lambda bi, h, t: (bi, h, t, 0))
    en_spec = pl.BlockSpec((1, 2, nch, LANE, HEAD_DIM), lambda bi, h, t: (bi, h, t, 0, 0))
    wl_spec = pl.BlockSpec((1, nch, LANE), lambda bi, h, t: (bi, t, h))
    grid = (b, d // LANE, s // rows)
    qm, en, wl = pl.pallas_call(
        functools.partial(_rwkv_chunk_kernel, rows=rows, sub=sub),
        grid=grid,
        in_specs=[hp] * 6,
        out_specs=[q_spec, en_spec, wl_spec],
        out_shape=[jax.ShapeDtypeStruct((b, nh, s, LANE), F32),
                   jax.ShapeDtypeStruct((b, nh, s // CHUNK, LANE, HEAD_DIM), F32),
                   jax.ShapeDtypeStruct((b, s // CHUNK, d), F32)],
        compiler_params=_cp(("parallel", "parallel", "parallel")),
        name="rwkv_chunk",
    )(r, k, v, al, be, lw)
    return pl.pallas_call(
        functools.partial(_rwkv_state_kernel, rows=rows),
        grid=grid,
        in_specs=[q_spec, en_spec, wl_spec, hp, hp, hp, hp,
                  pl.BlockSpec((1, 2, HEAD_DIM), lambda bi, h, t: (h, 0, 0)), par, par],
        out_specs=hp,
        out_shape=jax.ShapeDtypeStruct((b, s, d), BF16),
        scratch_shapes=[pltpu.VMEM((2, HEAD_DIM, HEAD_DIM), F32)],
        compiler_params=_cp(("parallel", "parallel", "arbitrary")),
        name="rwkv_state",
    )(qm, en, wl, r, k, v, g, r_k.reshape(-1, 2, HEAD_DIM), row1(gn_g), row1(gn_b))


def _hgrn_kernel(q_ref, f_ref, i_ref, g_ref, lb_ref, ng_ref, o_ref, st_ref, *, rows, nh):
    t = pl.program_id(2)
    sub = 16

    @pl.when(t == 0)
    def _():
        st_ref[...] = jnp.zeros_like(st_ref)

    q = jax.nn.silu(q_ref[0])
    lb = lb_ref[...]
    fg = lb + (1.0 - lb) * jax.nn.sigmoid(f_ref[0])
    lf = jnp.log(fg) * LOG2E
    kf = 1.0 - fg
    iv = i_ref[0]
    rowc = lax.broadcasted_iota(jnp.int32, lf.shape, 0) & (CHUNK - 1)
    bcum = lf
    for dd in (1, 2, 4, 8, 16, 32):
        bcum = bcum + jnp.where(rowc >= dd, pltpu.roll(bcum, dd, axis=0), 0.0)
    ivb = iv.astype(BF16)
    rsub = lax.broadcasted_iota(jnp.int32, (sub, LANE), 0)
    sts = [st_ref[hh] for hh in range(nh)]
    outs = [[] for _ in range(nh)]
    for c in range(rows // CHUNK):
        for hh in range(nh):
            rs = slice(c * CHUNK, (c + 1) * CHUNK)
            ls = slice(hh * LANE, (hh + 1) * LANE)
            bc, qc, kc, ic, icb = bcum[rs, ls], q[rs, ls], kf[rs, ls], iv[rs, ls], ivb[rs, ls]
            o_inter = _dot((qc * jnp.exp2(bc)).astype(BF16), sts[hh].astype(BF16), NT_DIMS)
            blocks = []
            for ib in range(CHUNK // sub):
                r0 = ib * sub
                bi = bc[r0:r0 + sub]
                qi = qc[r0:r0 + sub]
                ki = kc[r0:r0 + sub]
                ii = ic[r0:r0 + sub]
                acc = jnp.zeros((sub, LANE), F32)
                for s_ in range(sub):
                    e = jnp.where(rsub >= s_, jnp.exp2(jnp.minimum(bi - bi[s_:s_ + 1], 0.0)), 0.0)
                    a = jnp.sum(qi * (ki[s_:s_ + 1] * e), axis=-1, keepdims=True)
                    acc = acc + a * ii[s_:s_ + 1]
                if ib > 0:
                    ref_row = bc[r0 - 1:r0]
                    qt = (qi * jnp.exp2(bi - ref_row)).astype(BF16)
                    kt = (kc[0:r0] * jnp.exp2(ref_row - bc[0:r0])).astype(BF16)
                    a_off = _dot(qt, kt, NT_DIMS).astype(BF16)
                    acc = acc + _dot(a_off, icb[0:r0])
                blocks.append(acc)
            outs[hh].append(o_inter + jnp.concatenate(blocks, axis=0))
            bl = bc[CHUNK - 1:CHUNK]
            kdec = (kc * jnp.exp2(bl - bc)).astype(BF16)
            sts[hh] = sts[hh] * jnp.exp2(bl) + _dot(icb, kdec, TN_DIMS)
    res = []
    for hh in range(nh):
        st_ref[hh] = sts[hh]
        o = jnp.concatenate(outs[hh], axis=0)
        ms = jnp.mean(o * o, axis=-1, keepdims=True)
        res.append(o * lax.rsqrt(ms + EPS) * ng_ref[:, hh * LANE:(hh + 1) * LANE])
    o_ref[0] = (jnp.concatenate(res, axis=1) * jax.nn.silu(g_ref[0])).astype(o_ref.dtype)


def hgrn2(pm, lb, norm_g, rows=256, nh=2):
    b, s, _ = pm.shape
    width = nh * LANE
    spec = lambda u: pl.BlockSpec((1, rows, width), lambda bi, h, t: (bi, t, u // nh + h))
    par = pl.BlockSpec((1, width), lambda bi, h, t: (0, h))
    return pl.pallas_call(
        functools.partial(_hgrn_kernel, rows=rows, nh=nh),
        grid=(b, D_BRANCH // width, s // rows),
        in_specs=[spec(U_CQ), spec(U_CF), spec(U_CI), spec(U_CG), par, par],
        out_specs=spec(0),
        out_shape=jax.ShapeDtypeStruct((b, s, D_BRANCH), BF16),
        scratch_shapes=[pltpu.VMEM((nh, LANE, LANE), F32)],
        compiler_params=_cp(("parallel", "parallel", "arbitrary")),
        name="hgrn2",
    )(pm, pm, pm, pm, lb.reshape(1, -1), norm_g.reshape(1, -1))


def _dsa_prep_kernel(lo_ref, hi_ref, cos_ref, sin_ref,
                     qo_ref, iqo_ref, ko_ref, vo_ref, iko_ref, wo_ref, *, scale, wscale):
    cos = cos_ref[0]
    sin = sin_ref[0]
    hd = HEAD_DIM
    lo = lo_ref[0]
    hi = hi_ref[0]
    col = lambda u: (u - U_D_LO) * LANE
    q = (_rope(lo[:, col(U_DQ):col(U_DKV)], cos, sin) * scale).astype(BF16)
    iq_in = jnp.concatenate([lo[:, col(U_DIQ):], hi[:, :(U_DIKW - U_D_HI) * LANE]], axis=1)
    iq = _rope(iq_in, cos, sin).astype(BF16)
    for h in range(DSA_HEADS):
        qo_ref[0, h] = q[:, h * hd:(h + 1) * hd]
        iqo_ref[0, h] = iq[:, h * hd:(h + 1) * hd]
    kv = lo[:, col(U_DKV):col(U_DIQ)]
    ko_ref[0] = _rope(kv, cos, sin)[:, :hd].astype(BF16)
    vo_ref[0] = kv[:, hd:].astype(BF16)
    ikw = hi[:, (U_DIKW - U_D_HI) * LANE:]
    iko_ref[0] = _rope(ikw, cos, sin)[:, :hd].astype(BF16)
    wo_ref[0] = ikw[:, hd:hd + IDX_HEADS] * wscale


def _to_key(x):
    x = jnp.where(x == 0.0, 0.0, x)
    bits = pltpu.bitcast(x, jnp.int32)
    return jnp.where(bits < 0, bits ^ jnp.int32(0x7FFFFFFF), bits)


def _dsa_index_kernel(iq_ref, ik_ref, wt_ref, mask_ref, key_ref, *, tq, tk, top_k):
    i = pl.program_id(1)
    nk = key_ref.shape[0]
    nvis = (i * tq + tq - 1) // tk + 1
    krow = lax.broadcasted_iota(jnp.int32, (tk, tq), 0)
    qcol = i * tq + lax.broadcasted_iota(jnp.int32, (tk, tq), 1)
    wt = wt_ref[0]

    def score_block(jb):
        ikb = ik_ref[0, pl.ds(pl.multiple_of(jb * tk, tk), tk), :]
        s_all = _dot(ikb, iq_ref[0].reshape(IDX_HEADS * tq, HEAD_DIM), NT_DIMS)
        acc = jnp.zeros((tk, tq), F32)
        for h in range(IDX_HEADS):
            acc = acc + jnp.maximum(s_all[:, h * tq:(h + 1) * tq], 0.0) * wt[h:h + 1, :]
        vis = ((jb * tk + krow) >> 6) <= (qcol >> 6)
        key_ref[jb] = jnp.where(vis, _to_key(acc), INT_MIN)

    _paired_loop(nvis, score_block)

    def count(pred_fn):
        def blk(jb, cnt):
            ind = jnp.where(pred_fn(key_ref[jb], jb), 1, 0)
            return cnt + jnp.sum(ind.reshape(tk // 8, 8, tq), axis=0)
        cnt = lax.fori_loop(0, nvis, blk, jnp.zeros((8, tq), jnp.int32))
        return jnp.sum(cnt, axis=0, keepdims=True)

    def bit_step(bi, prefix):
        cand = prefix | lax.shift_left(jnp.int32(1), 31 - bi)
        cand_s = cand ^ jnp.int32(INT_MIN)
        c = count(lambda kb, jb: kb >= cand_s)
        return jnp.where(c >= top_k, cand, prefix)

    prefix = lax.fori_loop(0, 32, bit_step, jnp.zeros((1, tq), jnp.int32))
    tau = prefix ^ jnp.int32(INT_MIN)
    n_gt = count(lambda kb, jb: kb > tau)
    n_eq = count(lambda kb, jb: kb == tau)
    need = top_k - n_gt
    tied = (n_eq > need) & (tau != INT_MIN)
    n_cols = nk * tk

    idx_bits = int(n_cols).bit_length()

    def tie_break():
        def idx_step(bi, pre):
            cand = pre | lax.shift_left(jnp.int32(1), idx_bits - 1 - bi)
            c = count(lambda kb, jb: (kb == tau) & ((jb * tk + krow) < cand))
            return jnp.where(c < need, cand, pre)
        cut = lax.fori_loop(0, idx_bits, idx_step, jnp.zeros((1, tq), jnp.int32))
        return jnp.where(tied, cut, n_cols)

    cut = lax.cond(jnp.max(tied.astype(jnp.int32)) > 0, tie_break,
                   lambda: jnp.full((1, tq), n_cols, jnp.int32))

    def write_block(jb, carry):
        kb = key_ref[jb]
        sel = (kb > tau) | ((kb == tau) & ((jb * tk + krow) <= cut))
        sel = sel & (kb != INT_MIN)
        mask_ref[0, jb] = jnp.where(sel, 1.0, 0.0).T.astype(mask_ref.dtype)
        return carry

    def zero_block(jb, carry):
        mask_ref[0, jb] = jnp.zeros((tq, tk), mask_ref.dtype)
        return carry

    lax.fori_loop(0, nvis, write_block, 0)
    lax.fori_loop(nvis, nk, zero_block, 0)


def _dsa_attn_kernel(q_ref, k_ref, v_ref, mask_ref, o_ref, s_ref, m_ref, l_ref, acc_ref, *, tq, tk):
    i = pl.program_id(1)
    nh = DSA_HEADS
    hd = HEAD_DIM
    nvis = (i * tq + tq - 1) // tk + 1
    nchunk = tk // LANE

    m_ref[...] = jnp.full_like(m_ref, NEG)

    def max_step(j):
        kb = k_ref[0, pl.ds(pl.multiple_of(j * tk, tk), tk), :]
        s = _dot(q_ref[0].reshape(nh * tq, hd), kb, NT_DIMS).reshape(nh, tq, tk)
        s = jnp.where((mask_ref[0, j] > 0)[None], s, NEG)
        s_ref[j] = s
        m_ref[...] = jnp.maximum(m_ref[...], _lane_fold(s, jnp.maximum))

    _paired_loop(nvis, max_step)
    m_ref[...] = jnp.broadcast_to(jnp.max(m_ref[...], axis=2, keepdims=True), m_ref.shape)
    l_ref[...] = jnp.zeros_like(l_ref)
    acc_ref[...] = jnp.zeros_like(acc_ref)

    def acc_step(j):
        s = s_ref[j]
        mb = m_ref[...]
        ps = [jnp.exp2(s[:, :, c * LANE:(c + 1) * LANE] - mb) for c in range(nchunk)]
        l_ref[...] += functools.reduce(jnp.add, ps)
        p = jnp.concatenate(ps, axis=2).astype(BF16).reshape(nh * tq, tk)
        vb = v_ref[0, pl.ds(pl.multiple_of(j * tk, tk), tk), :]
        acc_ref[...] += _dot(p, vb).reshape(nh, tq, hd)

    _paired_loop(nvis, acc_step)
    o = acc_ref[...] / jnp.sum(l_ref[...], axis=2, keepdims=True)
    o_ref[0] = jnp.concatenate([o[h] for h in range(nh)], axis=1).astype(o_ref.dtype)


def dsa_attention(pm, cos, sin, ts=512, tq=128, tk=512):
    b, s, _ = pm.shape
    top_k = min(TOPK_MAX, s // 4)
    hd = HEAD_DIM
    nk = s // tk
    tab = pl.BlockSpec((1, ts, LANE), lambda bi, i: (bi, i, 0))
    heads_out = pl.BlockSpec((1, DSA_HEADS, ts, hd), lambda bi, i: (bi, 0, i, 0))
    narrow = lambda w: pl.BlockSpec((1, ts, w), lambda bi, i: (bi, i, 0))
    w_lo = (U_D_HI - U_D_LO) * LANE
    w_hi = N_MAIN - U_D_HI * LANE
    q, iq, k, v, ik, w = pl.pallas_call(
        functools.partial(_dsa_prep_kernel, scale=hd ** -0.5 * LOG2E, wscale=(IDX_HEADS ** -0.5) * (hd ** -0.5)),
        grid=(b, s // ts),
        in_specs=[pl.BlockSpec((1, ts, w_lo), lambda bi, i: (bi, i, U_D_LO * LANE // w_lo)),
                  pl.BlockSpec((1, ts, w_hi), lambda bi, i: (bi, i, U_D_HI * LANE // w_hi)),
                  tab, tab],
        out_specs=[heads_out, heads_out, narrow(hd), narrow(hd), narrow(hd), narrow(IDX_HEADS)],
        out_shape=[jax.ShapeDtypeStruct((b, DSA_HEADS, s, hd), BF16)] * 2
                  + [jax.ShapeDtypeStruct((b, s, hd), BF16)] * 3
                  + [jax.ShapeDtypeStruct((b, s, IDX_HEADS), F32)],
        compiler_params=_cp(("parallel", "parallel")),
        name="dsa_prep",
    )(pm, pm, cos, sin)

    mask_spec = pl.BlockSpec((1, nk, tq, tk), lambda bi, i: (bi, 0, i, 0))
    whole = pl.BlockSpec((1, s, hd), lambda bi, i: (bi, 0, 0))
    mask = pl.pallas_call(
        functools.partial(_dsa_index_kernel, tq=tq, tk=tk, top_k=top_k),
        grid=(b, s // tq),
        in_specs=[pl.BlockSpec((1, IDX_HEADS, tq, hd), lambda bi, i: (bi, 0, i, 0)),
                  whole,
                  pl.BlockSpec((1, IDX_HEADS, tq), lambda bi, i: (bi, 0, i))],
        out_specs=mask_spec,
        out_shape=jax.ShapeDtypeStruct((b, nk, s, tk), BF16),
        scratch_shapes=[pltpu.VMEM((nk, tk, tq), jnp.int32)],
        compiler_params=_cp(("parallel", "parallel")),
        name="dsa_index",
    )(iq, ik, jnp.swapaxes(w, 1, 2))

    return pl.pallas_call(
        functools.partial(_dsa_attn_kernel, tq=tq, tk=tk),
        grid=(b, s // tq),
        in_specs=[pl.BlockSpec((1, DSA_HEADS, tq, hd), lambda bi, i: (bi, 0, i, 0)),
                  whole, whole, mask_spec],
        out_specs=pl.BlockSpec((1, tq, D_BRANCH), lambda bi, i: (bi, i, 0)),
        out_shape=jax.ShapeDtypeStruct((b, s, D_BRANCH), BF16),
        scratch_shapes=[pltpu.VMEM((nk, DSA_HEADS, tq, tk), F32),
                        pltpu.VMEM((DSA_HEADS, tq, LANE), F32), pltpu.VMEM((DSA_HEADS, tq, LANE), F32),
                        pltpu.VMEM((DSA_HEADS, tq, hd), F32)],
        compiler_params=_cp(("parallel", "parallel")),
        name="dsa_attention",
    )(q, k, v, mask)


GATE_COL = 6600


def kernel(x, positions, norm_g, w_in, diff_lambda, diff_subln_g, rwkv_mu, rwkv_w_up, rwkv_a_up, rwkv_g_up, rwkv_w0, rwkv_a0, rwkv_k_k, rwkv_k_a, rwkv_r_k, rwkv_gn_g, rwkv_gn_b, hgrn_lb_logits, hgrn_norm_g, w_branch, w_out, mlp_w1, mlp_w2):
    b, s, d = x.shape
    m = b * s
    depth = w_in.shape[0]
    lb = jax.nn.softmax(hgrn_lb_logits.astype(F32), axis=0)
    lb = jnp.cumsum(lb, axis=0) - lb[0]
    w_in_t = jnp.swapaxes(w_in, 1, 2)
    w_out_b = w_out.astype(BF16)
    w2_b = mlp_w2.astype(BF16)
    cos, sin = rope_tables(positions)
    cos3 = cos.reshape(b, s, LANE)
    sin3 = sin.reshape(b, s, LANE)
    x2 = x.reshape(m, d)
    h = rmsnorm_bf16(x2, norm_g[0, 0])
    for l in range(depth):
        pm = matmul_wt(h, w_in_t, l, 0, N_MAIN, None, F32, name="proj_main").reshape(b, s, N_MAIN)
        gate = matmul_wt(h, w_in_t, l, GATE_COL, N_BRANCH * D_MODEL, "sigmoid", BF16, name="proj_gate")
        y_a = diff_attention(pm, cos3, sin3, diff_lambda[l], diff_subln_g[l], l)
        y_b = rwkv7(pm, rwkv_mu[l], rwkv_w_up[l], rwkv_a_up[l], rwkv_g_up[l], rwkv_w0[l], rwkv_a0[l],
                    rwkv_k_k[l], rwkv_k_a[l], rwkv_r_k[l], rwkv_gn_g[l], rwkv_gn_b[l])
        y_c = hgrn2(pm, lb[l], hgrn_norm_g[l])
        y_d = dsa_attention(pm, cos3, sin3)
        ys = [y.reshape(m, D_BRANCH) for y in (y_a, y_b, y_c, y_d)]
        merged = gated_merge(ys, w_branch, l, gate)
        x2, h2 = matmul_norm_residual(merged, w_out_b, l, x2, norm_g[l, 1], norm_g[l, 2], name="out_proj")
        ff = matmul(h2, mlp_w1, l, "relu2", BF16, name="mlp_up")
        if l + 1 < depth:
            x2, h = matmul_norm_residual(ff, w2_b, l, x2, norm_g[l, 3], norm_g[l + 1, 0], name="mlp_down")
        else:
            x2 = matmul_norm_residual(ff, w2_b, l, x2, norm_g[l, 3], name="mlp_down")
    return x2.reshape(b, s, d)
```

```python
import functools
import math

import numpy as np
import jax
import jax.numpy as jnp
from jax import lax
from jax.experimental import pallas as pl
from jax.experimental.pallas import tpu as pltpu

F32 = jnp.float32
BF16 = jnp.bfloat16

D_MODEL = 2048
D_BRANCH = 512
D_FF = 8192
N_BRANCH = 4
CHUNK = 64
EPS = 1e-6
ROPE_THETA = 10000.0
HEAD_DIM = 64
DIFF_HEADS = 4
RWKV_GN_EPS = 64e-5
DSA_HEADS = 8
IDX_HEADS = 8
TOPK_MAX = 256

LANE = 128
VMEM_LIMIT = 56 * 1024 * 1024

U_AQ, U_AK, U_AV = 0, 4, 8
U_B = 12
U_CQ, U_CF, U_CI, U_CG = 26, 30, 34, 38
U_DQ, U_DKV, U_DIQ, U_DIKW = 42, 46, 47, 51
U_D_LO, U_D_HI = 40, 48
N_MAIN = 52 * LANE
B_WIDTH = 1792
B_BLOCK = 3584

NEG = -1e30
INT_MIN = -2147483648
LOG2E = 1.4426950408889634

NT_DIMS = (((1,), (1,)), ((), ()))
TN_DIMS = (((0,), (0,)), ((), ()))


def _cp(sem, vmem=VMEM_LIMIT):
    return pltpu.CompilerParams(dimension_semantics=sem, vmem_limit_bytes=vmem)


def _dot(a, b, dims=None):
    if dims is None:
        return jnp.dot(a, b, preferred_element_type=F32)
    return lax.dot_general(a, b, dims, preferred_element_type=F32)


def _split2(a):
    hi = a.astype(BF16)
    return hi, (a - hi.astype(F32)).astype(BF16)


def _dot_split(a, b):
    a_hi, a_lo = _split2(a)
    b_hi, b_lo = _split2(b)
    return _dot(a_hi, b_hi) + (_dot(a_hi, b_lo) + _dot(a_lo, b_hi))


def _rmsnorm_kernel(x_ref, g_ref, o_ref):
    x = x_ref[...]
    ms = jnp.mean(x * x, axis=-1, keepdims=True)
    o_ref[...] = (x * lax.rsqrt(ms + EPS) * g_ref[...]).astype(o_ref.dtype)


def rmsnorm_bf16(x, g, tm=512):
    m, d = x.shape
    return pl.pallas_call(
        _rmsnorm_kernel,
        grid=(m // tm,),
        in_specs=[pl.BlockSpec((tm, d), lambda i: (i, 0)),
                  pl.BlockSpec((1, d), lambda i: (0, 0))],
        out_specs=pl.BlockSpec((tm, d), lambda i: (i, 0)),
        out_shape=jax.ShapeDtypeStruct((m, d), BF16),
        compiler_params=_cp(("parallel",)),
        name="rmsnorm",
    )(x, g.reshape(1, d))


def _mm_kernel(x_ref, w_ref, o_ref, *, act, dims):
    w = w_ref[...] if len(w_ref.shape) == 2 else w_ref[0]
    acc = _dot(x_ref[...], w.astype(BF16), dims)
    if act == "sigmoid":
        acc = 0.5 * jnp.tanh(0.5 * acc) + 0.5
    elif act == "relu2":
        acc = jnp.square(jnp.maximum(acc, 0.0))
    o_ref[...] = acc.astype(o_ref.dtype)


def matmul(x, w, layer, act, out_dtype, tm=2048, tn=512, name="mm"):
    m, k = x.shape
    n = w.shape[2]
    tm = min(tm, m)
    return pl.pallas_call(
        functools.partial(_mm_kernel, act=act, dims=None),
        grid=(m // tm, n // tn),
        in_specs=[pl.BlockSpec((tm, k), lambda i, j: (i, 0)),
                  pl.BlockSpec((None, k, tn), lambda i, j: (layer, 0, j))],
        out_specs=pl.BlockSpec((tm, tn), lambda i, j: (i, j)),
        out_shape=jax.ShapeDtypeStruct((m, n), out_dtype),
        compiler_params=_cp(("parallel", "parallel")),
        name=name,
    )(x, w)


def matmul_wt(x, wt, layer, row0, n, act, out_dtype, tm=2048, tn=512, name="mm_wt"):
    m, k = x.shape
    tm = min(tm, m)
    return pl.pallas_call(
        functools.partial(_mm_kernel, act=act, dims=NT_DIMS),
        grid=(m // tm, n // tn),
        in_specs=[pl.BlockSpec((tm, k), lambda i, j: (i, 0)),
                  pl.BlockSpec((pl.Element(1), pl.Element(tn), pl.Element(k)),
                               lambda i, j: (layer, pl.multiple_of(row0 + j * tn, 8), 0))],
        out_specs=pl.BlockSpec((tm, tn), lambda i, j: (i, j)),
        out_shape=jax.ShapeDtypeStruct((m, n), out_dtype),
        compiler_params=_cp(("parallel", "parallel")),
        name=name,
    )(x, wt)


def _norm_residual(y, x_ref, g_ref, o_ref, h_ref):
    ms = jnp.mean(y * y, axis=-1, keepdims=True)
    o = x_ref[...] + y * lax.rsqrt(ms + EPS) * g_ref[0:1, :]
    o_ref[...] = o
    if h_ref is not None:
        ms = jnp.mean(o * o, axis=-1, keepdims=True)
        h_ref[...] = (o * lax.rsqrt(ms + EPS) * g_ref[1:2, :]).astype(h_ref.dtype)


def _mm_norm_res_kernel(a_ref, w_ref, x_ref, g_ref, o_ref, *rest):
    h_ref, acc_ref = (rest[0], rest[1]) if len(rest) == 2 else (None, rest[0])
    kk = pl.program_id(1)

    @pl.when(kk == 0)
    def _():
        acc_ref[...] = jnp.zeros_like(acc_ref)

    acc_ref[...] += _dot(a_ref[...], w_ref[...])

    @pl.when(kk == pl.num_programs(1) - 1)
    def _():
        _norm_residual(acc_ref[...], x_ref, g_ref, o_ref, h_ref)


def _mm_norm_res_fullk_kernel(a_ref, w_ref, x_ref, g_ref, o_ref, *rest):
    _norm_residual(_dot(a_ref[...], w_ref[...]), x_ref, g_ref, o_ref, rest[0] if rest else None)


def matmul_norm_residual(a, w, layer, x, g, g_next=None, tm=512, tk=2048, name="mm_norm_res"):
    m, k = a.shape
    n = w.shape[2]
    gs = jnp.stack([g, g if g_next is None else g_next])
    out_shape = [jax.ShapeDtypeStruct((m, n), F32)]
    if g_next is not None:
        out_shape.append(jax.ShapeDtypeStruct((m, n), BF16))
    if k <= tk:
        row = lambda i: (i, 0)
        outs = pl.pallas_call(
            _mm_norm_res_fullk_kernel,
            grid=(m // tm,),
            in_specs=[pl.BlockSpec((tm, k), row), pl.BlockSpec((None, k, n), lambda i: (layer, 0, 0)),
                      pl.BlockSpec((tm, n), row), pl.BlockSpec((2, n), lambda i: (0, 0))],
            out_specs=[pl.BlockSpec((tm, n), row)] * len(out_shape),
            out_shape=out_shape,
            compiler_params=_cp(("parallel",)),
            name=name,
        )(a, w, x, gs)
    else:
        row = lambda i, kk: (i, 0)
        outs = pl.pallas_call(
            _mm_norm_res_kernel,
            grid=(m // tm, k // tk),
            in_specs=[pl.BlockSpec((tm, tk), lambda i, kk: (i, kk)),
                      pl.BlockSpec((None, tk, n), lambda i, kk: (layer, kk, 0)),
                      pl.BlockSpec((tm, n), row),
                      pl.BlockSpec((2, n), lambda i, kk: (0, 0))],
            out_specs=[pl.BlockSpec((tm, n), row)] * len(out_shape),
            out_shape=out_shape,
            scratch_shapes=[pltpu.VMEM((tm, n), F32)],
            compiler_params=_cp(("parallel", "arbitrary")),
            name=name,
        )(a, w, x, gs)
    return outs if g_next is not None else outs[0]


def _merge_kernel(ya, yb, yc, yd, wb, ga, gb, gc, gd, o_ref):
    acc = None
    for n, (y, g) in enumerate(((ya, ga), (yb, gb), (yc, gc), (yd, gd))):
        t = g[...].astype(F32) * _dot(y[...], wb[n].astype(BF16))
        acc = t if acc is None else acc + t
    o_ref[...] = acc.astype(o_ref.dtype)


def gated_merge(ys, wb, layer, gate, tm=512, tn=1024):
    m = ys[0].shape[0]
    nj = D_MODEL // tn
    y_spec = pl.BlockSpec((tm, D_BRANCH), lambda j, i: (i, 0))
    g_specs = [pl.BlockSpec((tm, tn), functools.partial(lambda j, i, n: (i, n * nj + j), n=n))
               for n in range(N_BRANCH)]
    return pl.pallas_call(
        _merge_kernel,
        grid=(nj, m // tm),
        in_specs=[y_spec] * 4 + [pl.BlockSpec((None, N_BRANCH, D_BRANCH, tn), lambda j, i: (layer, 0, 0, j))]
                 + g_specs,
        out_specs=pl.BlockSpec((tm, tn), lambda j, i: (i, j)),
        out_shape=jax.ShapeDtypeStruct((m, D_MODEL), BF16),
        compiler_params=_cp(("parallel", "parallel")),
        name="gated_merge",
    )(*ys, wb, gate, gate, gate, gate)


def _rope_table_kernel(pos_ref, inv_ref, sgn_ref, cos_ref, sin_ref):
    ang = pos_ref[...].astype(F32) * inv_ref[...]
    cos_ref[...] = jnp.cos(ang)
    sin_ref[...] = jnp.sin(ang) * sgn_ref[...]


def rope_tables(positions, tm=512):
    m = positions.size
    half = HEAD_DIM // 2
    inv = ROPE_THETA ** (-np.arange(0, HEAD_DIM, 2, dtype=np.float32) / HEAD_DIM)
    inv = np.tile(inv.astype(np.float32), 4).reshape(1, LANE)
    sgn = np.tile(np.concatenate([-np.ones(half, np.float32), np.ones(half, np.float32)]), 2).reshape(1, LANE)
    spec = pl.BlockSpec((tm, LANE), lambda i: (i, 0))
    cst = pl.BlockSpec((1, LANE), lambda i: (0, 0))
    return pl.pallas_call(
        _rope_table_kernel,
        grid=(m // tm,),
        in_specs=[pl.BlockSpec((tm, 1), lambda i: (i, 0)), cst, cst],
        out_specs=[spec, spec],
        out_shape=[jax.ShapeDtypeStruct((m, LANE), F32)] * 2,
        compiler_params=_cp(("parallel",)),
        name="rope_tables",
    )(positions.reshape(m, 1), jnp.asarray(inv), jnp.asarray(sgn))


def _rope(x, cos, sin):
    w = x.shape[1]
    n = w // LANE
    if n > 1:
        cos = jnp.concatenate([cos] * n, axis=1)
        sin = jnp.concatenate([sin] * n, axis=1)
    lane = lax.broadcasted_iota(jnp.int32, x.shape, 1)
    up = pltpu.roll(x, w - HEAD_DIM // 2, axis=1)
    dn = pltpu.roll(x, HEAD_DIM // 2, axis=1)
    rot = jnp.where((lane & (HEAD_DIM // 2)) == 0, up, dn)
    return x * cos + rot * sin


def _rope_qkv_kernel(q_ref, k_ref, v_ref, cos_ref, sin_ref, qo_ref, ko_ref, vo_ref, *, scale):
    cos = cos_ref[0]
    sin = sin_ref[0]
    hd = HEAD_DIM
    q = (_rope(q_ref[0], cos, sin) * scale).astype(qo_ref.dtype)
    k = _rope(k_ref[0], cos, sin).astype(ko_ref.dtype)
    for h in range(q.shape[1] // hd):
        qo_ref[0, h] = q[:, h * hd:(h + 1) * hd]
        ko_ref[0, h] = k[:, h * hd:(h + 1) * hd]
    vo_ref[0] = v_ref[0].astype(vo_ref.dtype)


def rope_qkv(pm, cos, sin, uq, uk, uv, scale, ts=512):
    b, s, _ = pm.shape
    wq = D_BRANCH
    nh = wq // HEAD_DIM
    tab = pl.BlockSpec((1, ts, LANE), lambda bi, i: (bi, i, 0))
    col = lambda u: pl.BlockSpec((1, ts, wq), lambda bi, i: (bi, i, u * LANE // wq))
    heads = pl.BlockSpec((1, nh, ts, HEAD_DIM), lambda bi, i: (bi, 0, i, 0))
    return pl.pallas_call(
        functools.partial(_rope_qkv_kernel, scale=scale),
        grid=(b, s // ts),
        in_specs=[col(uq), col(uk), col(uv), tab, tab],
        out_specs=[heads, heads, pl.BlockSpec((1, ts, wq), lambda bi, i: (bi, i, 0))],
        out_shape=[jax.ShapeDtypeStruct((b, nh, s, HEAD_DIM), BF16)] * 2
                  + [jax.ShapeDtypeStruct((b, s, wq), BF16)],
        compiler_params=_cp(("parallel", "parallel")),
        name="rope_qkv",
    )(pm, pm, pm, cos, sin)


def _lane_fold(x, op):
    out = x[..., 0:LANE]
    for c in range(1, x.shape[-1] // LANE):
        out = op(out, x[..., c * LANE:(c + 1) * LANE])
    return out


def _paired_loop(n, step):
    def body(jj, carry):
        step(2 * jj)
        step(2 * jj + 1)
        return carry
    lax.fori_loop(0, n // 2, body, 0)

    @pl.when(n % 2 == 1)
    def _():
        step(n - 1)


def _diff_attn_kernel(q_ref, k_ref, v_ref, lam_ref, g_ref, o_ref, s_ref, m_ref, l_ref, acc_ref, *, tq, tk, lam_init):
    i = pl.program_id(2)
    nfull = (i * tq) // tk
    row = lax.broadcasted_iota(jnp.int32, (tq, tk), 0)
    col = lax.broadcasted_iota(jnp.int32, (tq, tk), 1)
    vis = (col >> 6) <= (row >> 6)

    m_ref[...] = jnp.full_like(m_ref, NEG)

    def max_step(j, masked=False):
        for mm in range(2):
            kb = k_ref[0, mm, pl.ds(pl.multiple_of(j * tk, tk), tk), :]
            s = _dot(q_ref[0, mm], kb, NT_DIMS)
            if masked:
                s = jnp.where(vis, s, NEG)
            s_ref[j, mm] = s
            m_ref[mm] = jnp.maximum(m_ref[mm], _lane_fold(s, jnp.maximum))

    _paired_loop(nfull, max_step)
    max_step(nfull, True)
    for mm in range(2):
        m_ref[mm] = jnp.broadcast_to(jnp.max(m_ref[mm], axis=1, keepdims=True), (tq, LANE))
    l_ref[...] = jnp.zeros_like(l_ref)
    acc_ref[...] = jnp.zeros_like(acc_ref)

    def acc_step(j):
        vb = v_ref[0, pl.ds(pl.multiple_of(j * tk, tk), tk), :]
        for mm in range(2):
            s = s_ref[j, mm]
            mb = m_ref[mm]
            ps = [jnp.exp2(s[:, c * LANE:(c + 1) * LANE] - mb) for c in range(tk // LANE)]
            l_ref[mm] += functools.reduce(jnp.add, ps)
            acc_ref[mm] += _dot(jnp.concatenate(ps, axis=1).astype(BF16), vb)

    _paired_loop(nfull + 1, acc_step)

    lv = lam_ref[...]
    lam = (jnp.exp(jnp.sum(lv[0:1] * lv[1:2], keepdims=True))
           - jnp.exp(jnp.sum(lv[2:3] * lv[3:4], keepdims=True)) + lam_init)
    l0 = jnp.sum(l_ref[0], axis=1, keepdims=True)
    l1 = jnp.sum(l_ref[1], axis=1, keepdims=True)
    o = acc_ref[0] / l0 - lam * (acc_ref[1] / l1)
    ms = jnp.mean(o * o, axis=-1, keepdims=True)
    o = o * lax.rsqrt(ms + EPS) * g_ref[...] * (1.0 - lam_init)
    o_ref[0] = o.astype(o_ref.dtype)


def diff_attention(pm, cos, sin, lam_vecs, subln_g, layer, tq=512):
    b, s, _ = pm.shape
    tk = tq
    qa, ka, va = rope_qkv(pm, cos, sin, U_AQ, U_AK, U_AV, HEAD_DIM ** -0.5 * LOG2E)
    lam_init = 0.8 - 0.6 * math.exp(-0.3 * layer)
    return pl.pallas_call(
        functools.partial(_diff_attn_kernel, tq=tq, tk=tk, lam_init=lam_init),
        grid=(b, DIFF_HEADS, s // tq),
        in_specs=[pl.BlockSpec((1, 2, tq, HEAD_DIM), lambda bi, h, i: (bi, h, i, 0)),
                  pl.BlockSpec((1, 2, s, HEAD_DIM), lambda bi, h, i: (bi, h, 0, 0)),
                  pl.BlockSpec((1, s, LANE), lambda bi, h, i: (bi, 0, h)),
                  pl.BlockSpec((4, HEAD_DIM), lambda bi, h, i: (0, 0)),
                  pl.BlockSpec((1, LANE), lambda bi, h, i: (0, 0))],
        out_specs=pl.BlockSpec((1, tq, LANE), lambda bi, h, i: (bi, i, h)),
        out_shape=jax.ShapeDtypeStruct((b, s, D_BRANCH), BF16),
        scratch_shapes=[pltpu.VMEM((s // tk, 2, tq, tk), F32),
                        pltpu.VMEM((2, tq, LANE), F32), pltpu.VMEM((2, tq, LANE), F32),
                        pltpu.VMEM((2, tq, LANE), F32)],
        compiler_params=_cp(("parallel", "parallel", "parallel")),
        name="diff_attention",
    )(qa, ka, va, lam_vecs, subln_g.reshape(1, LANE))


def _softplus(z):
    return jnp.maximum(z, 0.0) + jnp.log(1.0 + jnp.exp(-jnp.abs(z)))


def _rwkv_prep_kernel(x_ref, prev_ref, mu_ref, wup_ref, aup_ref, gup_ref, w0_ref, a0_ref, kkg_ref, kag_ref,
                      seg_ref, r_ref, k_ref, v_ref, al_ref, be_ref, lw_ref, g_ref):
    i = pl.program_id(1)
    c0 = U_B * LANE
    p = x_ref[0, :, c0:c0 + B_WIDTH]
    row = lax.broadcasted_iota(jnp.int32, p.shape, 0)
    last = prev_ref[0, 7:8, c0:c0 + B_WIDTH]
    last = jnp.where(i == 0, jnp.zeros_like(last), last)
    prev = jnp.where(row == 0, jnp.broadcast_to(last, p.shape), pltpu.roll(p, 1, axis=0))
    ps = p + (prev - p) * mu_ref[...]
    d = D_BRANCH
    r, k, v = ps[:, 0:d], ps[:, d:2 * d], ps[:, 2 * d:3 * d]
    wd = ps[:, 3 * d:3 * d + 64]
    ad = ps[:, 3 * d + 64:3 * d + 128]
    gd = ps[:, 3 * d + 128:3 * d + 256]
    w_log = -_softplus(-(w0_ref[...] + _dot_split(jnp.tanh(wd), wup_ref[...]))) - 0.5
    a = jax.nn.sigmoid(a0_ref[...] + _dot_split(ad, aup_ref[...]))
    g = _dot_split(jax.nn.sigmoid(gd), gup_ref[...])
    kk = k * kkg_ref[...]
    sq_hi, sq_lo = _split2(kk * kk)
    nrm = jnp.sqrt(_dot(sq_hi, seg_ref[...]) + _dot(sq_lo, seg_ref[...]))
    kk = kk / jnp.maximum(nrm, 1e-12)
    r_ref[0] = r
    k_ref[0] = k * (1.0 + (a - 1.0) * kag_ref[...])
    v_ref[0] = v
    al_ref[0] = -kk
    be_ref[0] = kk * a
    lw_ref[0] = -jnp.exp(w_log)
    g_ref[0] = g


def _bdot(a, b, dims):
    return lax.dot_general(a, b, ((dims[0], dims[1]), ((0,), (0,))), preferred_element_type=F32)


BNN = ((2,), (1,))
BNT = ((2,), (2,))


def _rwkv_chunk_kernel(r_ref, k_ref, v_ref, al_ref, be_ref, lw_ref, q_ref, en_ref, wl_ref, *, rows, sub):
    hd = HEAD_DIM
    nch = rows // CHUNK
    nsb = rows // sub
    lw = lw_ref[0]
    rowc = lax.broadcasted_iota(jnp.int32, lw.shape, 0) & (CHUNK - 1)
    cum = lw
    for dd in (1, 2, 4, 8, 16, 32):
        cum = cum + jnp.where(rowc >= dd, pltpu.roll(cum, dd, axis=0), 0.0)
    cl = jnp.concatenate(
        [jnp.broadcast_to(cum[(c + 1) * CHUNK - 1:(c + 1) * CHUNK, :], (CHUNK, LANE)) for c in range(nch)], axis=0)
    e_in = jnp.exp(cum)
    e_out = jnp.exp(-cum)
    e_end = jnp.exp(cl - cum)
    r2, k2, v2, al2, be2 = r_ref[0], k_ref[0], v_ref[0], al_ref[0], be_ref[0]
    at2 = al2 * jnp.exp(cum - lw)
    rt2 = r2 * e_in
    bt2 = be2 * e_out
    kt2 = k2 * e_out
    bh2 = be2 * e_end
    kh2 = k2 * e_end
    wl2 = jnp.exp(cl)

    row = lax.broadcasted_iota(jnp.int32, (sub, sub), 0)
    col = lax.broadcasted_iota(jnp.int32, (sub, sub), 1)
    same = (row >> 6) == (col >> 6)
    m_strict = same & (row > col)
    m_incl = same & (row >= col)
    eye = (row == col).astype(F32)
    m_blk8 = (row >> 3) == (col >> 3)
    m_lvls = [((row >> (sh + 1)) == (col >> (sh + 1))) & ((row >> sh) != (col >> sh)) for sh in (3, 4, 5)]
    wl_ref[0] = jnp.concatenate([wl2[c * CHUNK:c * CHUNK + 1, :] for c in range(nch)], axis=0)

    def stack(a):
        return jnp.stack([a[sb * sub:(sb + 1) * sub, hh * hd:(hh + 1) * hd]
                          for hh in range(2) for sb in range(nsb)])

    at, rt, vb = stack(at2), stack(rt2), stack(v2).astype(BF16)
    lhs = jnp.concatenate([at, rt], axis=1).astype(BF16)
    rhs = jnp.concatenate([stack(bt2), stack(kt2)], axis=1).astype(BF16)
    gm = _bdot(lhs, rhs, BNT)
    n_ab = jnp.where(m_strict[None], gm[:, :sub, :sub], 0.0)
    a_ak = jnp.where(m_strict[None], gm[:, :sub, sub:], 0.0).astype(BF16)
    a_rb = jnp.where(m_incl[None], gm[:, sub:, :sub], 0.0).astype(BF16)
    a_rk = jnp.where(m_incl[None], gm[:, sub:, sub:], 0.0).astype(BF16)
    n8 = jnp.where(m_blk8[None], n_ab, 0.0)
    n8b = n8.astype(BF16)
    n_2 = _bdot(n8b, n8b, BNN)
    n2b = n_2.astype(BF16)
    n_3 = _bdot(n2b, n8b, BNN)
    n_4 = _bdot(n2b, n2b, BNN)
    tinv = eye[None] + n8 + n_2 + n_3
    tinv = tinv + _bdot(tinv.astype(BF16), n_4.astype(BF16), BNN)
    for m_lvl in m_lvls:
        nl = jnp.where(m_lvl[None], n_ab, 0.0).astype(BF16)
        tb = tinv.astype(BF16)
        tinv = tinv + _bdot(_bdot(tb, nl, BNN).astype(BF16), tb, BNN)
    tb = tinv.astype(BF16)
    akv = _bdot(a_ak, vb, BNN)
    pmat = _bdot(tb, jnp.concatenate([at, akv], axis=2).astype(BF16), BNN)
    qmat = _bdot(a_rb, pmat.astype(BF16), BNN) + jnp.concatenate([rt, _bdot(a_rk, vb, BNN)], axis=2)
    pb = pmat.astype(BF16)
    bhb = stack(bh2).astype(BF16)
    khb = stack(kh2).astype(BF16)
    for hh in range(2):
        for sb in range(nsb):
            bi = hh * nsb + sb
            q_ref[0, hh, sb * sub:(sb + 1) * sub, :] = qmat[bi]
            for cc in range(sub // CHUNK):
                cs = slice(cc * CHUNK, (cc + 1) * CHUNK)
                mn = _dot(pb[bi, cs], bhb[bi, cs], TN_DIMS)
                n_c = mn[hd:] + _dot(vb[bi, cs], khb[bi, cs], TN_DIMS)
                en_ref[0, hh, sb * (sub // CHUNK) + cc] = jnp.concatenate([mn[:hd], n_c], axis=0)


def _rwkv_state_kernel(q_ref, en_ref, wl_ref, r_ref, k_ref, v_ref, g_ref, rk_ref, gng_ref, gnb_ref,
                       o_ref, st_ref, *, rows, nhs):
    t = pl.program_id(2)
    hd = HEAD_DIM
    nch = rows // CHUNK

    @pl.when(t == 0)
    def _():
        st_ref[...] = jnp.zeros_like(st_ref)

    wl = wl_ref[0]
    r2, k2, v2, g2 = r_ref[0], k_ref[0], v_ref[0], g_ref[0]
    sts = [st_ref[hh] for hh in range(nhs)]
    ys = [[] for _ in range(nhs)]
    for c in range(nch):
        for hh in range(nhs):
            qc = q_ref[0, hh, c * CHUNK:(c + 1) * CHUNK, :]
            en = en_ref[0, hh, c]
            stb = sts[hh].astype(BF16)
            ys[hh].append(_dot(qc[:, :hd].astype(BF16), stb, NT_DIMS) + qc[:, hd:])
            sts[hh] = (sts[hh] * wl[c:c + 1, hh * hd:(hh + 1) * hd]
                       + _dot(stb, en[:hd].astype(BF16)) + en[hd:])
    outs = []
    for hh in range(nhs):
        sl = slice(hh * hd, (hh + 1) * hd)
        st_ref[hh] = sts[hh]
        v, r, kp = v2[:, sl], r2[:, sl], k2[:, sl]
        y = jnp.concatenate(ys[hh], axis=0)
        mu = jnp.mean(y, axis=-1, keepdims=True)
        var = jnp.mean(jnp.square(y - mu), axis=-1, keepdims=True)
        yn = (y - mu) * lax.rsqrt(var + RWKV_GN_EPS) * gng_ref[:, sl] + gnb_ref[:, sl]
        bonus = jnp.sum(r * kp * rk_ref[hh:hh + 1, :], axis=-1, keepdims=True) * v
        outs.append((yn + bonus) * g2[:, sl])
    o_ref[0] = jnp.concatenate(outs, axis=1).astype(o_ref.dtype)


def rwkv7(pm, mu, w_up, a_up, g_up, w0, a0, kk_gain, ka_gain, r_k, gn_g, gn_b, ts=256, rows=512, sub=256, nhs=8):
    b, s, _ = pm.shape
    d = D_BRANCH
    seg = np.kron(np.eye(d // HEAD_DIM, dtype=np.float32), np.ones((HEAD_DIM, HEAD_DIM), np.float32))
    row1 = lambda a: a.reshape(1, -1)
    cst = lambda shape: pl.BlockSpec(shape, lambda bi, i: (0,) * len(shape))
    blk = pl.BlockSpec((1, ts, d), lambda bi, i: (bi, i, 0))
    r, k, v, al, be, lw, g = pl.pallas_call(
        _rwkv_prep_kernel,
        grid=(b, s // ts),
        in_specs=[pl.BlockSpec((1, ts, B_BLOCK), lambda bi, i: (bi, i, 0)),
                  pl.BlockSpec((1, 8, B_BLOCK), lambda bi, i: (bi, jnp.maximum(i * (ts // 8) - 1, 0), 0)),
                  cst((1, B_WIDTH)), cst((64, d)), cst((64, d)), cst((128, d)),
                  cst((1, d)), cst((1, d)), cst((1, d)), cst((1, d)), cst((d, d))],
        out_specs=[blk] * 7,
        out_shape=[jax.ShapeDtypeStruct((b, s, d), F32)] * 7,
        compiler_params=_cp(("parallel", "arbitrary")),
        name="rwkv_prep",
    )(pm, pm, row1(mu), w_up, a_up, g_up, row1(w0), row1(a0), row1(kk_gain), row1(ka_gain), jnp.asarray(seg, BF16))

    hp = pl.BlockSpec((1, rows, LANE), lambda bi, h, t: (bi, t, h))
    par = pl.BlockSpec((1, LANE), lambda bi, h, t: (0, h))
    nh = d // HEAD_DIM
    nch = rows // CHUNK
    q_spec = pl.BlockSpec((1, 2, rows, LANE), lambda bi, h, t: (bi, h, t, 0))
    en_spec = pl.BlockSpec((1, 2, nch, LANE, HEAD_DIM), lambda bi, h, t: (bi, h, t, 0, 0))
    wl_spec = pl.BlockSpec((1, nch, LANE), lambda bi, h, t: (bi, t, h))
    grid = (b, d // LANE, s // rows)
    qm, en, wl = pl.pallas_call(
        functools.partial(_rwkv_chunk_kernel, rows=rows, sub=sub),
        grid=grid,
        in_specs=[hp] * 6,
        out_specs=[q_spec, en_spec, wl_spec],
        out_shape=[jax.ShapeDtypeStruct((b, nh, s, LANE), F32),
                   jax.ShapeDtypeStruct((b, nh, s // CHUNK, LANE, HEAD_DIM), F32),
                   jax.ShapeDtypeStruct((b, s // CHUNK, d), F32)],
        compiler_params=_cp(("parallel", "parallel", "parallel")),
        name="rwkv_chunk",
    )(r, k, v, al, be, lw)
    wide = nhs * HEAD_DIM
    rows_s = pl.BlockSpec((1, rows, wide), lambda bi, h, t: (bi, t, h))
    par_s = pl.BlockSpec((1, wide), lambda bi, h, t: (0, h))
    return pl.pallas_call(
        functools.partial(_rwkv_state_kernel, rows=rows, nhs=nhs),
        grid=(b, nh // nhs, s // rows),
        in_specs=[pl.BlockSpec((1, nhs, rows, LANE), lambda bi, h, t: (bi, h, t, 0)),
                  pl.BlockSpec((1, nhs, nch, LANE, HEAD_DIM), lambda bi, h, t: (bi, h, t, 0, 0)),
                  pl.BlockSpec((1, nch, wide), lambda bi, h, t: (bi, t, h)),
                  rows_s, rows_s, rows_s, rows_s,
                  pl.BlockSpec((nhs, HEAD_DIM), lambda bi, h, t: (h, 0)), par_s, par_s],
        out_specs=rows_s,
        out_shape=jax.ShapeDtypeStruct((b, s, d), BF16),
        scratch_shapes=[pltpu.VMEM((nhs, HEAD_DIM, HEAD_DIM), F32)],
        compiler_params=_cp(("parallel", "parallel", "arbitrary")),
        name="rwkv_state",
    )(qm, en, wl, r, k, v, g, r_k, row1(gn_g), row1(gn_b))


def _hgrn_kernel(q_ref, f_ref, i_ref, g_ref, lb_ref, ng_ref, o_ref, st_ref, *, rows, nh):
    t = pl.program_id(2)
    sub = 16

    @pl.when(t == 0)
    def _():
        st_ref[...] = jnp.zeros_like(st_ref)

    q = jax.nn.silu(q_ref[0])
    lb = lb_ref[...]
    fg = lb + (1.0 - lb) * jax.nn.sigmoid(f_ref[0])
    lf = jnp.log(fg) * LOG2E
    kf = 1.0 - fg
    iv = i_ref[0]
    rowc = lax.broadcasted_iota(jnp.int32, lf.shape, 0) & (CHUNK - 1)
    bcum = lf
    for dd in (1, 2, 4, 8, 16, 32):
        bcum = bcum + jnp.where(rowc >= dd, pltpu.roll(bcum, dd, axis=0), 0.0)
    ivb = iv.astype(BF16)
    rsub = lax.broadcasted_iota(jnp.int32, (sub, LANE), 0)
    sts = [st_ref[hh] for hh in range(nh)]
    outs = [[] for _ in range(nh)]
    for c in range(rows // CHUNK):
        for hh in range(nh):
            rs = slice(c * CHUNK, (c + 1) * CHUNK)
            ls = slice(hh * LANE, (hh + 1) * LANE)
            bc, qc, kc, ic, icb = bcum[rs, ls], q[rs, ls], kf[rs, ls], iv[rs, ls], ivb[rs, ls]
            o_inter = _dot((qc * jnp.exp2(bc)).astype(BF16), sts[hh].astype(BF16), NT_DIMS)
            blocks = []
            for ib in range(CHUNK // sub):
                r0 = ib * sub
                bi = bc[r0:r0 + sub]
                qi = qc[r0:r0 + sub]
                ki = kc[r0:r0 + sub]
                ii = ic[r0:r0 + sub]
                acc = jnp.zeros((sub, LANE), F32)
                for s_ in range(sub):
                    e = jnp.where(rsub >= s_, jnp.exp2(jnp.minimum(bi - bi[s_:s_ + 1], 0.0)), 0.0)
                    a = jnp.sum(qi * (ki[s_:s_ + 1] * e), axis=-1, keepdims=True)
                    acc = acc + a * ii[s_:s_ + 1]
                if ib > 0:
                    ref_row = bc[r0 - 1:r0]
                    qt = (qi * jnp.exp2(bi - ref_row)).astype(BF16)
                    kt = (kc[0:r0] * jnp.exp2(ref_row - bc[0:r0])).astype(BF16)
                    a_off = _dot(qt, kt, NT_DIMS).astype(BF16)
                    acc = acc + _dot(a_off, icb[0:r0])
                blocks.append(acc)
            outs[hh].append(o_inter + jnp.concatenate(blocks, axis=0))
            bl = bc[CHUNK - 1:CHUNK]
            kdec = (kc * jnp.exp2(bl - bc)).astype(BF16)
            sts[hh] = sts[hh] * jnp.exp2(bl) + _dot(icb, kdec, TN_DIMS)
    res = []
    for hh in range(nh):
        st_ref[hh] = sts[hh]
        o = jnp.concatenate(outs[hh], axis=0)
        ms = jnp.mean(o * o, axis=-1, keepdims=True)
        res.append(o * lax.rsqrt(ms + EPS) * ng_ref[:, hh * LANE:(hh + 1) * LANE])
    o_ref[0] = (jnp.concatenate(res, axis=1) * jax.nn.silu(g_ref[0])).astype(o_ref.dtype)


def hgrn2(pm, lb, norm_g, rows=256, nh=2):
    b, s, _ = pm.shape
    width = nh * LANE
    spec = lambda u: pl.BlockSpec((1, rows, width), lambda bi, h, t: (bi, t, u // nh + h))
    par = pl.BlockSpec((1, width), lambda bi, h, t: (0, h))
    return pl.pallas_call(
        functools.partial(_hgrn_kernel, rows=rows, nh=nh),
        grid=(b, D_BRANCH // width, s // rows),
        in_specs=[spec(U_CQ), spec(U_CF), spec(U_CI), spec(U_CG), par, par],
        out_specs=spec(0),
        out_shape=jax.ShapeDtypeStruct((b, s, D_BRANCH), BF16),
        scratch_shapes=[pltpu.VMEM((nh, LANE, LANE), F32)],
        compiler_params=_cp(("parallel", "parallel", "arbitrary")),
        name="hgrn2",
    )(pm, pm, pm, pm, lb.reshape(1, -1), norm_g.reshape(1, -1))


def _dsa_prep_kernel(lo_ref, hi_ref, cos_ref, sin_ref,
                     qo_ref, iqo_ref, ko_ref, vo_ref, iko_ref, wo_ref, *, scale, wscale):
    cos = cos_ref[0]
    sin = sin_ref[0]
    hd = HEAD_DIM
    lo = lo_ref[0]
    hi = hi_ref[0]
    col = lambda u: (u - U_D_LO) * LANE
    q = (_rope(lo[:, col(U_DQ):col(U_DKV)], cos, sin) * scale).astype(BF16)
    iq_in = jnp.concatenate([lo[:, col(U_DIQ):], hi[:, :(U_DIKW - U_D_HI) * LANE]], axis=1)
    iq = _rope(iq_in, cos, sin).astype(BF16)
    for h in range(DSA_HEADS):
        qo_ref[0, h] = q[:, h * hd:(h + 1) * hd]
        iqo_ref[0, h] = iq[:, h * hd:(h + 1) * hd]
    kv = lo[:, col(U_DKV):col(U_DIQ)]
    ko_ref[0] = _rope(kv, cos, sin)[:, :hd].astype(BF16)
    vo_ref[0] = kv[:, hd:].astype(BF16)
    ikw = hi[:, (U_DIKW - U_D_HI) * LANE:]
    iko_ref[0] = _rope(ikw, cos, sin)[:, :hd].astype(BF16)
    wo_ref[0] = ikw[:, hd:hd + IDX_HEADS] * wscale


def _to_key(x):
    x = jnp.where(x == 0.0, 0.0, x)
    bits = pltpu.bitcast(x, jnp.int32)
    return jnp.where(bits < 0, bits ^ jnp.int32(0x7FFFFFFF), bits)


def _dsa_index_kernel(iq_ref, ik_ref, wt_ref, mask_ref, key_ref, *, tq, tk, top_k):
    i = pl.program_id(1)
    nk = key_ref.shape[0]
    nvis = (i * tq + tq - 1) // tk + 1
    krow = lax.broadcasted_iota(jnp.int32, (tk, tq), 0)
    qcol = i * tq + lax.broadcasted_iota(jnp.int32, (tk, tq), 1)
    wt = wt_ref[0]

    def score_block(jb):
        ikb = ik_ref[0, pl.ds(pl.multiple_of(jb * tk, tk), tk), :]
        s_all = _dot(ikb, iq_ref[0].reshape(IDX_HEADS * tq, HEAD_DIM), NT_DIMS)
        acc = jnp.zeros((tk, tq), F32)
        for h in range(IDX_HEADS):
            acc = acc + jnp.maximum(s_all[:, h * tq:(h + 1) * tq], 0.0) * wt[h:h + 1, :]
        vis = ((jb * tk + krow) >> 6) <= (qcol >> 6)
        key_ref[jb] = jnp.where(vis, _to_key(acc), INT_MIN)

    _paired_loop(nvis, score_block)

    def count(pred_fn):
        def blk(jb, cnt):
            ind = jnp.where(pred_fn(key_ref[jb], jb), 1, 0)
            return cnt + jnp.sum(ind.reshape(tk // 8, 8, tq), axis=0)
        cnt = lax.fori_loop(0, nvis, blk, jnp.zeros((8, tq), jnp.int32))
        return jnp.sum(cnt, axis=0, keepdims=True)

    def bit_step(bi, carry):
        prefix, n_ge = carry
        cand = prefix | lax.shift_left(jnp.int32(1), 31 - bi)
        cand_s = cand ^ jnp.int32(INT_MIN)
        c = count(lambda kb, jb: kb >= cand_s)
        keep = c >= top_k
        return jnp.where(keep, cand, prefix), jnp.where(keep, c, n_ge)

    prefix, n_ge = lax.fori_loop(0, 32, bit_step, (jnp.zeros((1, tq), jnp.int32),
                                                   jnp.zeros((1, tq), jnp.int32) + nvis * tk))
    tau = prefix ^ jnp.int32(INT_MIN)
    n_gt = count(lambda kb, jb: kb > tau)
    n_eq = n_ge - n_gt
    need = top_k - n_gt
    tied = (n_eq > need) & (tau != INT_MIN)
    n_cols = nk * tk

    idx_bits = int(n_cols).bit_length()

    def tie_break():
        def idx_step(bi, pre):
            cand = pre | lax.shift_left(jnp.int32(1), idx_bits - 1 - bi)
            c = count(lambda kb, jb: (kb == tau) & ((jb * tk + krow) < cand))
            return jnp.where(c < need, cand, pre)
        cut = lax.fori_loop(0, idx_bits, idx_step, jnp.zeros((1, tq), jnp.int32))
        return jnp.where(tied, cut, n_cols)

    cut = lax.cond(jnp.max(tied.astype(jnp.int32)) > 0, tie_break,
                   lambda: jnp.full((1, tq), n_cols, jnp.int32))

    def write_block(jb, carry):
        kb = key_ref[jb]
        sel = (kb > tau) | ((kb == tau) & ((jb * tk + krow) <= cut))
        sel = sel & (kb != INT_MIN)
        mask_ref[0, jb] = jnp.where(sel, 1.0, 0.0).T.astype(mask_ref.dtype)
        return carry

    def zero_block(jb, carry):
        mask_ref[0, jb] = jnp.zeros((tq, tk), mask_ref.dtype)
        return carry

    lax.fori_loop(0, nvis, write_block, 0)
    lax.fori_loop(nvis, nk, zero_block, 0)


def _dsa_attn_kernel(q_ref, k_ref, v_ref, mask_ref, o_ref, s_ref, m_ref, l_ref, acc_ref, *, tq, tk):
    i = pl.program_id(1)
    nh = DSA_HEADS
    hd = HEAD_DIM
    nvis = (i * tq + tq - 1) // tk + 1
    nchunk = tk // LANE

    m_ref[...] = jnp.full_like(m_ref, NEG)

    def max_step(j):
        kb = k_ref[0, pl.ds(pl.multiple_of(j * tk, tk), tk), :]
        s = _dot(q_ref[0].reshape(nh * tq, hd), kb, NT_DIMS).reshape(nh, tq, tk)
        s = jnp.where((mask_ref[0, j] > 0)[None], s, NEG)
        s_ref[j] = s
        m_ref[...] = jnp.maximum(m_ref[...], _lane_fold(s, jnp.maximum))

    _paired_loop(nvis, max_step)
    m_ref[...] = jnp.broadcast_to(jnp.max(m_ref[...], axis=2, keepdims=True), m_ref.shape)
    l_ref[...] = jnp.zeros_like(l_ref)
    acc_ref[...] = jnp.zeros_like(acc_ref)

    def acc_step(j):
        s = s_ref[j]
        mb = m_ref[...]
        ps = [jnp.exp2(s[:, :, c * LANE:(c + 1) * LANE] - mb) for c in range(nchunk)]
        l_ref[...] += functools.reduce(jnp.add, ps)
        p = jnp.concatenate(ps, axis=2).astype(BF16).reshape(nh * tq, tk)
        vb = v_ref[0, pl.ds(pl.multiple_of(j * tk, tk), tk), :]
        acc_ref[...] += _dot(p, vb).reshape(nh, tq, hd)

    _paired_loop(nvis, acc_step)
    o = acc_ref[...] / jnp.sum(l_ref[...], axis=2, keepdims=True)
    o_ref[0] = jnp.concatenate([o[h] for h in range(nh)], axis=1).astype(o_ref.dtype)


def dsa_attention(pm, cos, sin, ts=512, tq=128, tk=512):
    b, s, _ = pm.shape
    top_k = min(TOPK_MAX, s // 4)
    hd = HEAD_DIM
    nk = s // tk
    tab = pl.BlockSpec((1, ts, LANE), lambda bi, i: (bi, i, 0))
    heads_out = pl.BlockSpec((1, DSA_HEADS, ts, hd), lambda bi, i: (bi, 0, i, 0))
    narrow = lambda w: pl.BlockSpec((1, ts, w), lambda bi, i: (bi, i, 0))
    w_lo = (U_D_HI - U_D_LO) * LANE
    w_hi = N_MAIN - U_D_HI * LANE
    q, iq, k, v, ik, w = pl.pallas_call(
        functools.partial(_dsa_prep_kernel, scale=hd ** -0.5 * LOG2E, wscale=(IDX_HEADS ** -0.5) * (hd ** -0.5)),
        grid=(b, s // ts),
        in_specs=[pl.BlockSpec((1, ts, w_lo), lambda bi, i: (bi, i, U_D_LO * LANE // w_lo)),
                  pl.BlockSpec((1, ts, w_hi), lambda bi, i: (bi, i, U_D_HI * LANE // w_hi)),
                  tab, tab],
        out_specs=[heads_out, heads_out, narrow(hd), narrow(hd), narrow(hd), narrow(IDX_HEADS)],
        out_shape=[jax.ShapeDtypeStruct((b, DSA_HEADS, s, hd), BF16)] * 2
                  + [jax.ShapeDtypeStruct((b, s, hd), BF16)] * 3
                  + [jax.ShapeDtypeStruct((b, s, IDX_HEADS), F32)],
        compiler_params=_cp(("parallel", "parallel")),
        name="dsa_prep",
    )(pm, pm, cos, sin)

    mask_spec = pl.BlockSpec((1, nk, tq, tk), lambda bi, i: (bi, 0, i, 0))
    whole = pl.BlockSpec((1, s, hd), lambda bi, i: (bi, 0, 0))
    mask = pl.pallas_call(
        functools.partial(_dsa_index_kernel, tq=tq, tk=tk, top_k=top_k),
        grid=(b, s // tq),
        in_specs=[pl.BlockSpec((1, IDX_HEADS, tq, hd), lambda bi, i: (bi, 0, i, 0)),
                  whole,
                  pl.BlockSpec((1, IDX_HEADS, tq), lambda bi, i: (bi, 0, i))],
        out_specs=mask_spec,
        out_shape=jax.ShapeDtypeStruct((b, nk, s, tk), BF16),
        scratch_shapes=[pltpu.VMEM((nk, tk, tq), jnp.int32)],
        compiler_params=_cp(("parallel", "parallel")),
        name="dsa_index",
    )(iq, ik, jnp.swapaxes(w, 1, 2))

    return pl.pallas_call(
        functools.partial(_dsa_attn_kernel, tq=tq, tk=tk),
        grid=(b, s // tq),
        in_specs=[pl.BlockSpec((1, DSA_HEADS, tq, hd), lambda bi, i: (bi, 0, i, 0)),
                  whole, whole, mask_spec],
        out_specs=pl.BlockSpec((1, tq, D_BRANCH), lambda bi, i: (bi, i, 0)),
        out_shape=jax.ShapeDtypeStruct((b, s, D_BRANCH), BF16),
        scratch_shapes=[pltpu.VMEM((nk, DSA_HEADS, tq, tk), F32),
                        pltpu.VMEM((DSA_HEADS, tq, LANE), F32), pltpu.VMEM((DSA_HEADS, tq, LANE), F32),
                        pltpu.VMEM((DSA_HEADS, tq, hd), F32)],
        compiler_params=_cp(("parallel", "parallel")),
        name="dsa_attention",
    )(q, k, v, mask)


GATE_COL = 6600


def kernel(x, positions, norm_g, w_in, diff_lambda, diff_subln_g, rwkv_mu, rwkv_w_up, rwkv_a_up, rwkv_g_up, rwkv_w0, rwkv_a0, rwkv_k_k, rwkv_k_a, rwkv_r_k, rwkv_gn_g, rwkv_gn_b, hgrn_lb_logits, hgrn_norm_g, w_branch, w_out, mlp_w1, mlp_w2):
    b, s, d = x.shape
    m = b * s
    depth = w_in.shape[0]
    lb = jax.nn.softmax(hgrn_lb_logits.astype(F32), axis=0)
    lb = jnp.cumsum(lb, axis=0) - lb[0]
    w_in_t = jnp.swapaxes(w_in, 1, 2)
    w_out_b = w_out.astype(BF16)
    w2_b = mlp_w2.astype(BF16)
    cos, sin = rope_tables(positions)
    cos3 = cos.reshape(b, s, LANE)
    sin3 = sin.reshape(b, s, LANE)
    x2 = x.reshape(m, d)
    h = rmsnorm_bf16(x2, norm_g[0, 0])
    for l in range(depth):
        pm = matmul_wt(h, w_in_t, l, 0, N_MAIN, None, F32, name="proj_main").reshape(b, s, N_MAIN)
        gate = matmul_wt(h, w_in_t, l, GATE_COL, N_BRANCH * D_MODEL, "sigmoid", BF16, tn=1024, name="proj_gate")
        y_a = diff_attention(pm, cos3, sin3, diff_lambda[l], diff_subln_g[l], l)
        y_b = rwkv7(pm, rwkv_mu[l], rwkv_w_up[l], rwkv_a_up[l], rwkv_g_up[l], rwkv_w0[l], rwkv_a0[l],
                    rwkv_k_k[l], rwkv_k_a[l], rwkv_r_k[l], rwkv_gn_g[l], rwkv_gn_b[l])
        y_c = hgrn2(pm, lb[l], hgrn_norm_g[l])
        y_d = dsa_attention(pm, cos3, sin3)
        ys = [y.reshape(m, D_BRANCH) for y in (y_a, y_b, y_c, y_d)]
        merged = gated_merge(ys, w_branch, l, gate)
        x2, h2 = matmul_norm_residual(merged, w_out_b, l, x2, norm_g[l, 1], norm_g[l, 2], name="out_proj")
        ff = matmul(h2, mlp_w1, l, "relu2", BF16, tn=1024, name="mlp_up")
        if l + 1 < depth:
            x2, h = matmul_norm_residual(ff, w2_b, l, x2, norm_g[l, 3], norm_g[l + 1, 0], name="mlp_down")
        else:
            x2 = matmul_norm_residual(ff, w2_b, l, x2, norm_g[l, 3], name="mlp_down")
    return x2.reshape(b, s, d)
```

```python
import functools
import math

import numpy as np
import jax
import jax.numpy as jnp
from jax import lax
from jax.experimental import pallas as pl
from jax.experimental.pallas import tpu as pltpu

F32 = jnp.float32
BF16 = jnp.bfloat16

D_MODEL = 2048
D_BRANCH = 512
D_FF = 8192
N_BRANCH = 4
CHUNK = 64
EPS = 1e-6
ROPE_THETA = 10000.0
HEAD_DIM = 64
DIFF_HEADS = 4
RWKV_GN_EPS = 64e-5
DSA_HEADS = 8
IDX_HEADS = 8
TOPK_MAX = 256

LANE = 128
VMEM_LIMIT = 56 * 1024 * 1024

U_AQ, U_AK, U_AV = 0, 4, 8
U_B = 12
U_CQ, U_CF, U_CI, U_CG = 26, 30, 34, 38
U_DQ, U_DKV, U_DIQ, U_DIKW = 42, 46, 47, 51
U_D_LO, U_D_HI = 40, 48
N_MAIN = 52 * LANE
B_WIDTH = 1792
B_BLOCK = 3584

NEG = -1e30
INT_MIN = -2147483648
LOG2E = 1.4426950408889634

NT_DIMS = (((1,), (1,)), ((), ()))
TN_DIMS = (((0,), (0,)), ((), ()))


def _cp(sem, vmem=VMEM_LIMIT):
    return pltpu.CompilerParams(dimension_semantics=sem, vmem_limit_bytes=vmem)


def _dot(a, b, dims=None):
    if dims is None:
        return jnp.dot(a, b, preferred_element_type=F32)
    return lax.dot_general(a, b, dims, preferred_element_type=F32)


def _split2(a):
    hi = a.astype(BF16)
    return hi, (a - hi.astype(F32)).astype(BF16)


def _dot_split(a, b):
    a_hi, a_lo = _split2(a)
    b_hi, b_lo = _split2(b)
    return _dot(a_hi, b_hi) + (_dot(a_hi, b_lo) + _dot(a_lo, b_hi))


def _rmsnorm_kernel(x_ref, g_ref, o_ref):
    x = x_ref[...]
    ms = jnp.mean(x * x, axis=-1, keepdims=True)
    o_ref[...] = (x * lax.rsqrt(ms + EPS) * g_ref[...]).astype(o_ref.dtype)


def rmsnorm_bf16(x, g, tm=512):
    m, d = x.shape
    return pl.pallas_call(
        _rmsnorm_kernel,
        grid=(m // tm,),
        in_specs=[pl.BlockSpec((tm, d), lambda i: (i, 0)),
                  pl.BlockSpec((1, d), lambda i: (0, 0))],
        out_specs=pl.BlockSpec((tm, d), lambda i: (i, 0)),
        out_shape=jax.ShapeDtypeStruct((m, d), BF16),
        compiler_params=_cp(("parallel",)),
        name="rmsnorm",
    )(x, g.reshape(1, d))


def _mm_kernel(x_ref, w_ref, o_ref, *, act, dims):
    w = w_ref[...] if len(w_ref.shape) == 2 else w_ref[0]
    acc = _dot(x_ref[...], w.astype(BF16), dims)
    if act == "sigmoid":
        acc = 0.5 * jnp.tanh(0.5 * acc) + 0.5
    elif act == "relu2":
        acc = jnp.square(jnp.maximum(acc, 0.0))
    o_ref[...] = acc.astype(o_ref.dtype)


def matmul(x, w, layer, act, out_dtype, tm=2048, tn=512, name="mm"):
    m, k = x.shape
    n = w.shape[2]
    tm = min(tm, m)
    return pl.pallas_call(
        functools.partial(_mm_kernel, act=act, dims=None),
        grid=(m // tm, n // tn),
        in_specs=[pl.BlockSpec((tm, k), lambda i, j: (i, 0)),
                  pl.BlockSpec((None, k, tn), lambda i, j: (layer, 0, j))],
        out_specs=pl.BlockSpec((tm, tn), lambda i, j: (i, j)),
        out_shape=jax.ShapeDtypeStruct((m, n), out_dtype),
        compiler_params=_cp(("parallel", "parallel")),
        name=name,
    )(x, w)


def matmul_wt(x, wt, layer, row0, n, act, out_dtype, tm=2048, tn=512, name="mm_wt"):
    m, k = x.shape
    tm = min(tm, m)
    return pl.pallas_call(
        functools.partial(_mm_kernel, act=act, dims=NT_DIMS),
        grid=(m // tm, n // tn),
        in_specs=[pl.BlockSpec((tm, k), lambda i, j: (i, 0)),
                  pl.BlockSpec((pl.Element(1), pl.Element(tn), pl.Element(k)),
                               lambda i, j: (layer, pl.multiple_of(row0 + j * tn, 8), 0))],
        out_specs=pl.BlockSpec((tm, tn), lambda i, j: (i, j)),
        out_shape=jax.ShapeDtypeStruct((m, n), out_dtype),
        compiler_params=_cp(("parallel", "parallel")),
        name=name,
    )(x, wt)


def _norm_residual(y, x_ref, g_ref, o_ref, h_ref):
    ms = jnp.mean(y * y, axis=-1, keepdims=True)
    o = x_ref[...] + y * lax.rsqrt(ms + EPS) * g_ref[0:1, :]
    o_ref[...] = o
    if h_ref is not None:
        ms = jnp.mean(o * o, axis=-1, keepdims=True)
        h_ref[...] = (o * lax.rsqrt(ms + EPS) * g_ref[1:2, :]).astype(h_ref.dtype)


def _mm_norm_res_kernel(a_ref, w_ref, x_ref, g_ref, o_ref, *rest):
    h_ref, acc_ref = (rest[0], rest[1]) if len(rest) == 2 else (None, rest[0])
    kk = pl.program_id(1)

    @pl.when(kk == 0)
    def _():
        acc_ref[...] = jnp.zeros_like(acc_ref)

    acc_ref[...] += _dot(a_ref[...], w_ref[...])

    @pl.when(kk == pl.num_programs(1) - 1)
    def _():
        _norm_residual(acc_ref[...], x_ref, g_ref, o_ref, h_ref)


def _mm_norm_res_fullk_kernel(a_ref, w_ref, x_ref, g_ref, o_ref, *rest):
    _norm_residual(_dot(a_ref[...], w_ref[...]), x_ref, g_ref, o_ref, rest[0] if rest else None)


def matmul_norm_residual(a, w, layer, x, g, g_next=None, tm=512, tk=2048, name="mm_norm_res"):
    m, k = a.shape
    n = w.shape[2]
    gs = jnp.stack([g, g if g_next is None else g_next])
    out_shape = [jax.ShapeDtypeStruct((m, n), F32)]
    if g_next is not None:
        out_shape.append(jax.ShapeDtypeStruct((m, n), BF16))
    if k <= tk:
        row = lambda i: (i, 0)
        outs = pl.pallas_call(
            _mm_norm_res_fullk_kernel,
            grid=(m // tm,),
            in_specs=[pl.BlockSpec((tm, k), row), pl.BlockSpec((None, k, n), lambda i: (layer, 0, 0)),
                      pl.BlockSpec((tm, n), row), pl.BlockSpec((2, n), lambda i: (0, 0))],
            out_specs=[pl.BlockSpec((tm, n), row)] * len(out_shape),
            out_shape=out_shape,
            compiler_params=_cp(("parallel",)),
            name=name,
        )(a, w, x, gs)
    else:
        row = lambda i, kk: (i, 0)
        outs = pl.pallas_call(
            _mm_norm_res_kernel,
            grid=(m // tm, k // tk),
            in_specs=[pl.BlockSpec((tm, tk), lambda i, kk: (i, kk)),
                      pl.BlockSpec((None, tk, n), lambda i, kk: (layer, kk, 0)),
                      pl.BlockSpec((tm, n), row),
                      pl.BlockSpec((2, n), lambda i, kk: (0, 0))],
            out_specs=[pl.BlockSpec((tm, n), row)] * len(out_shape),
            out_shape=out_shape,
            scratch_shapes=[pltpu.VMEM((tm, n), F32)],
            compiler_params=_cp(("parallel", "arbitrary")),
            name=name,
        )(a, w, x, gs)
    return outs if g_next is not None else outs[0]


def _merge_kernel(ya, yb, yc, yd, wb, ga, gb, gc, gd, o_ref):
    acc = None
    for n, (y, g) in enumerate(((ya, ga), (yb, gb), (yc, gc), (yd, gd))):
        t = g[...].astype(F32) * _dot(y[...], wb[n].astype(BF16))
        acc = t if acc is None else acc + t
    o_ref[...] = acc.astype(o_ref.dtype)


def gated_merge(ys, wb, layer, gate, tm=512, tn=1024):
    m = ys[0].shape[0]
    nj = D_MODEL // tn
    y_spec = pl.BlockSpec((tm, D_BRANCH), lambda j, i: (i, 0))
    g_specs = [pl.BlockSpec((tm, tn), functools.partial(lambda j, i, n: (i, n * nj + j), n=n))
               for n in range(N_BRANCH)]
    return pl.pallas_call(
        _merge_kernel,
        grid=(nj, m // tm),
        in_specs=[y_spec] * 4 + [pl.BlockSpec((None, N_BRANCH, D_BRANCH, tn), lambda j, i: (layer, 0, 0, j))]
                 + g_specs,
        out_specs=pl.BlockSpec((tm, tn), lambda j, i: (i, j)),
        out_shape=jax.ShapeDtypeStruct((m, D_MODEL), BF16),
        compiler_params=_cp(("parallel", "parallel")),
        name="gated_merge",
    )(*ys, wb, gate, gate, gate, gate)


def _rope_table_kernel(pos_ref, inv_ref, sgn_ref, cos_ref, sin_ref):
    ang = pos_ref[...].astype(F32) * inv_ref[...]
    cos_ref[...] = jnp.cos(ang)
    sin_ref[...] = jnp.sin(ang) * sgn_ref[...]


def rope_tables(positions, tm=512):
    m = positions.size
    half = HEAD_DIM // 2
    inv = ROPE_THETA ** (-np.arange(0, HEAD_DIM, 2, dtype=np.float32) / HEAD_DIM)
    inv = np.tile(inv.astype(np.float32), 4).reshape(1, LANE)
    sgn = np.tile(np.concatenate([-np.ones(half, np.float32), np.ones(half, np.float32)]), 2).reshape(1, LANE)
    spec = pl.BlockSpec((tm, LANE), lambda i: (i, 0))
    cst = pl.BlockSpec((1, LANE), lambda i: (0, 0))
    return pl.pallas_call(
        _rope_table_kernel,
        grid=(m // tm,),
        in_specs=[pl.BlockSpec((tm, 1), lambda i: (i, 0)), cst, cst],
        out_specs=[spec, spec],
        out_shape=[jax.ShapeDtypeStruct((m, LANE), F32)] * 2,
        compiler_params=_cp(("parallel",)),
        name="rope_tables",
    )(positions.reshape(m, 1), jnp.asarray(inv), jnp.asarray(sgn))


def _rope(x, cos, sin):
    w = x.shape[1]
    n = w // LANE
    if n > 1:
        cos = jnp.concatenate([cos] * n, axis=1)
        sin = jnp.concatenate([sin] * n, axis=1)
    lane = lax.broadcasted_iota(jnp.int32, x.shape, 1)
    up = pltpu.roll(x, w - HEAD_DIM // 2, axis=1)
    dn = pltpu.roll(x, HEAD_DIM // 2, axis=1)
    rot = jnp.where((lane & (HEAD_DIM // 2)) == 0, up, dn)
    return x * cos + rot * sin


def _rope_qkv_kernel(q_ref, k_ref, v_ref, cos_ref, sin_ref, qo_ref, ko_ref, vo_ref, *, scale):
    cos = cos_ref[0]
    sin = sin_ref[0]
    hd = HEAD_DIM
    q = (_rope(q_ref[0], cos, sin) * scale).astype(qo_ref.dtype)
    k = _rope(k_ref[0], cos, sin).astype(ko_ref.dtype)
    for h in range(q.shape[1] // hd):
        qo_ref[0, h] = q[:, h * hd:(h + 1) * hd]
        ko_ref[0, h] = k[:, h * hd:(h + 1) * hd]
    vo_ref[0] = v_ref[0].astype(vo_ref.dtype)


def rope_qkv(pm, cos, sin, uq, uk, uv, scale, ts=512):
    b, s, _ = pm.shape
    wq = D_BRANCH
    nh = wq // HEAD_DIM
    tab = pl.BlockSpec((1, ts, LANE), lambda bi, i: (bi, i, 0))
    col = lambda u: pl.BlockSpec((1, ts, wq), lambda bi, i: (bi, i, u * LANE // wq))
    heads = pl.BlockSpec((1, nh, ts, HEAD_DIM), lambda bi, i: (bi, 0, i, 0))
    return pl.pallas_call(
        functools.partial(_rope_qkv_kernel, scale=scale),
        grid=(b, s // ts),
        in_specs=[col(uq), col(uk), col(uv), tab, tab],
        out_specs=[heads, heads, pl.BlockSpec((1, ts, wq), lambda bi, i: (bi, i, 0))],
        out_shape=[jax.ShapeDtypeStruct((b, nh, s, HEAD_DIM), BF16)] * 2
                  + [jax.ShapeDtypeStruct((b, s, wq), BF16)],
        compiler_params=_cp(("parallel", "parallel")),
        name="rope_qkv",
    )(pm, pm, pm, cos, sin)


def _lane_fold(x, op):
    out = x[..., 0:LANE]
    for c in range(1, x.shape[-1] // LANE):
        out = op(out, x[..., c * LANE:(c + 1) * LANE])
    return out


def _paired_loop(n, step):
    def body(jj, carry):
        step(2 * jj)
        step(2 * jj + 1)
        return carry
    lax.fori_loop(0, n // 2, body, 0)

    @pl.when(n % 2 == 1)
    def _():
        step(n - 1)


def _diff_attn_kernel(q_ref, k_ref, v_ref, lam_ref, g_ref, o_ref, s_ref, m_ref, l_ref, acc_ref, *, tq, tk, lam_init):
    i = pl.program_id(2)
    nfull = (i * tq) // tk
    row = lax.broadcasted_iota(jnp.int32, (tq, tk), 0)
    col = lax.broadcasted_iota(jnp.int32, (tq, tk), 1)
    vis = (col >> 6) <= (row >> 6)

    m_ref[...] = jnp.full_like(m_ref, NEG)

    def max_step(j, masked=False):
        for mm in range(2):
            kb = k_ref[0, mm, pl.ds(pl.multiple_of(j * tk, tk), tk), :]
            s = _dot(q_ref[0, mm], kb, NT_DIMS)
            if masked:
                s = jnp.where(vis, s, NEG)
            s_ref[j, mm] = s
            m_ref[mm] = jnp.maximum(m_ref[mm], _lane_fold(s, jnp.maximum))

    _paired_loop(nfull, max_step)
    max_step(nfull, True)
    for mm in range(2):
        m_ref[mm] = jnp.broadcast_to(jnp.max(m_ref[mm], axis=1, keepdims=True), (tq, LANE))
    l_ref[...] = jnp.zeros_like(l_ref)
    acc_ref[...] = jnp.zeros_like(acc_ref)

    def acc_step(j):
        vb = v_ref[0, pl.ds(pl.multiple_of(j * tk, tk), tk), :]
        for mm in range(2):
            s = s_ref[j, mm]
            mb = m_ref[mm]
            ps = [jnp.exp2(s[:, c * LANE:(c + 1) * LANE] - mb) for c in range(tk // LANE)]
            l_ref[mm] += functools.reduce(jnp.add, ps)
            acc_ref[mm] += _dot(jnp.concatenate(ps, axis=1).astype(BF16), vb)

    _paired_loop(nfull + 1, acc_step)

    lv = lam_ref[...]
    lam = (jnp.exp(jnp.sum(lv[0:1] * lv[1:2], keepdims=True))
           - jnp.exp(jnp.sum(lv[2:3] * lv[3:4], keepdims=True)) + lam_init)
    l0 = jnp.sum(l_ref[0], axis=1, keepdims=True)
    l1 = jnp.sum(l_ref[1], axis=1, keepdims=True)
    o = acc_ref[0] / l0 - lam * (acc_ref[1] / l1)
    ms = jnp.mean(o * o, axis=-1, keepdims=True)
    o = o * lax.rsqrt(ms + EPS) * g_ref[...] * (1.0 - lam_init)
    o_ref[0] = o.astype(o_ref.dtype)


def diff_attention(pm, cos, sin, lam_vecs, subln_g, layer, tq=512):
    b, s, _ = pm.shape
    tk = tq
    qa, ka, va = rope_qkv(pm, cos, sin, U_AQ, U_AK, U_AV, HEAD_DIM ** -0.5 * LOG2E)
    lam_init = 0.8 - 0.6 * math.exp(-0.3 * layer)
    return pl.pallas_call(
        functools.partial(_diff_attn_kernel, tq=tq, tk=tk, lam_init=lam_init),
        grid=(b, DIFF_HEADS, s // tq),
        in_specs=[pl.BlockSpec((1, 2, tq, HEAD_DIM), lambda bi, h, i: (bi, h, i, 0)),
                  pl.BlockSpec((1, 2, s, HEAD_DIM), lambda bi, h, i: (bi, h, 0, 0)),
                  pl.BlockSpec((1, s, LANE), lambda bi, h, i: (bi, 0, h)),
                  pl.BlockSpec((4, HEAD_DIM), lambda bi, h, i: (0, 0)),
                  pl.BlockSpec((1, LANE), lambda bi, h, i: (0, 0))],
        out_specs=pl.BlockSpec((1, tq, LANE), lambda bi, h, i: (bi, i, h)),
        out_shape=jax.ShapeDtypeStruct((b, s, D_BRANCH), BF16),
        scratch_shapes=[pltpu.VMEM((s // tk, 2, tq, tk), F32),
                        pltpu.VMEM((2, tq, LANE), F32), pltpu.VMEM((2, tq, LANE), F32),
                        pltpu.VMEM((2, tq, LANE), F32)],
        compiler_params=_cp(("parallel", "parallel", "parallel")),
        name="diff_attention",
    )(qa, ka, va, lam_vecs, subln_g.reshape(1, LANE))


def _softplus(z):
    return jnp.maximum(z, 0.0) + jnp.log(1.0 + jnp.exp(-jnp.abs(z)))


def _rwkv_prep_kernel(x_ref, prev_ref, mu_ref, wup_ref, aup_ref, gup_ref, w0_ref, a0_ref, kkg_ref, kag_ref,
                      seg_ref, r_ref, k_ref, v_ref, al_ref, be_ref, lw_ref, g_ref):
    i = pl.program_id(1)
    c0 = U_B * LANE
    p = x_ref[0, :, c0:c0 + B_WIDTH]
    row = lax.broadcasted_iota(jnp.int32, p.shape, 0)
    last = prev_ref[0, 7:8, c0:c0 + B_WIDTH]
    last = jnp.where(i == 0, jnp.zeros_like(last), last)
    prev = jnp.where(row == 0, jnp.broadcast_to(last, p.shape), pltpu.roll(p, 1, axis=0))
    ps = p + (prev - p) * mu_ref[...]
    d = D_BRANCH
    r, k, v = ps[:, 0:d], ps[:, d:2 * d], ps[:, 2 * d:3 * d]
    wd = ps[:, 3 * d:3 * d + 64]
    ad = ps[:, 3 * d + 64:3 * d + 128]
    gd = ps[:, 3 * d + 128:3 * d + 256]
    w_log = -_softplus(-(w0_ref[...] + _dot_split(jnp.tanh(wd), wup_ref[...]))) - 0.5
    a = jax.nn.sigmoid(a0_ref[...] + _dot_split(ad, aup_ref[...]))
    g = _dot_split(jax.nn.sigmoid(gd), gup_ref[...])
    kk = k * kkg_ref[...]
    sq_hi, sq_lo = _split2(kk * kk)
    nrm = jnp.sqrt(_dot(sq_hi, seg_ref[...]) + _dot(sq_lo, seg_ref[...]))
    kk = kk / jnp.maximum(nrm, 1e-12)
    r_ref[0] = r
    k_ref[0] = k * (1.0 + (a - 1.0) * kag_ref[...])
    v_ref[0] = v
    al_ref[0] = -kk
    be_ref[0] = kk * a
    lw_ref[0] = -jnp.exp(w_log)
    g_ref[0] = g


def _bdot(a, b, dims):
    return lax.dot_general(a, b, ((dims[0], dims[1]), ((0,), (0,))), preferred_element_type=F32)


BNN = ((2,), (1,))
BNT = ((2,), (2,))


def _rwkv_chunk_kernel(r_ref, k_ref, v_ref, al_ref, be_ref, lw_ref, q_ref, en_ref, wl_ref, *, rows, sub):
    hd = HEAD_DIM
    nch = rows // CHUNK
    nsb = rows // sub
    lw = lw_ref[0]
    rowc = lax.broadcasted_iota(jnp.int32, lw.shape, 0) & (CHUNK - 1)
    cum = lw
    for dd in (1, 2, 4, 8, 16, 32):
        cum = cum + jnp.where(rowc >= dd, pltpu.roll(cum, dd, axis=0), 0.0)
    cl = jnp.concatenate(
        [jnp.broadcast_to(cum[(c + 1) * CHUNK - 1:(c + 1) * CHUNK, :], (CHUNK, LANE)) for c in range(nch)], axis=0)
    e_in = jnp.exp(cum)
    e_out = jnp.exp(-cum)
    e_end = jnp.exp(cl - cum)
    r2, k2, v2, al2, be2 = r_ref[0], k_ref[0], v_ref[0], al_ref[0], be_ref[0]
    at2 = al2 * jnp.exp(cum - lw)
    rt2 = r2 * e_in
    bt2 = be2 * e_out
    kt2 = k2 * e_out
    bh2 = be2 * e_end
    kh2 = k2 * e_end
    wl2 = jnp.exp(cl)

    row = lax.broadcasted_iota(jnp.int32, (sub, sub), 0)
    col = lax.broadcasted_iota(jnp.int32, (sub, sub), 1)
    same = (row >> 6) == (col >> 6)
    m_strict = same & (row > col)
    m_incl = same & (row >= col)
    eye = (row == col).astype(F32)
    m_blk8 = (row >> 3) == (col >> 3)
    m_lvls = [((row >> (sh + 1)) == (col >> (sh + 1))) & ((row >> sh) != (col >> sh)) for sh in (3, 4, 5)]
    wl_ref[0] = jnp.concatenate([wl2[c * CHUNK:c * CHUNK + 1, :] for c in range(nch)], axis=0)

    def stack(a):
        return jnp.stack([a[sb * sub:(sb + 1) * sub, hh * hd:(hh + 1) * hd]
                          for hh in range(2) for sb in range(nsb)])

    at, rt, vb = stack(at2), stack(rt2), stack(v2).astype(BF16)
    lhs = jnp.concatenate([at, rt], axis=1).astype(BF16)
    rhs = jnp.concatenate([stack(bt2), stack(kt2)], axis=1).astype(BF16)
    gm = _bdot(lhs, rhs, BNT)
    n_ab = jnp.where(m_strict[None], gm[:, :sub, :sub], 0.0)
    a_ak = jnp.where(m_strict[None], gm[:, :sub, sub:], 0.0).astype(BF16)
    a_rb = jnp.where(m_incl[None], gm[:, sub:, :sub], 0.0).astype(BF16)
    a_rk = jnp.where(m_incl[None], gm[:, sub:, sub:], 0.0).astype(BF16)
    n8 = jnp.where(m_blk8[None], n_ab, 0.0)
    n8b = n8.astype(BF16)
    n_2 = _bdot(n8b, n8b, BNN)
    n2b = n_2.astype(BF16)
    n_3 = _bdot(n2b, n8b, BNN)
    n_4 = _bdot(n2b, n2b, BNN)
    tinv = eye[None] + n8 + n_2 + n_3
    tinv = tinv + _bdot(tinv.astype(BF16), n_4.astype(BF16), BNN)
    for m_lvl in m_lvls:
        nl = jnp.where(m_lvl[None], n_ab, 0.0).astype(BF16)
        tb = tinv.astype(BF16)
        tinv = tinv + _bdot(_bdot(tb, nl, BNN).astype(BF16), tb, BNN)
    tb = tinv.astype(BF16)
    akv = _bdot(a_ak, vb, BNN)
    pmat = _bdot(tb, jnp.concatenate([at, akv], axis=2).astype(BF16), BNN)
    qmat = _bdot(a_rb, pmat.astype(BF16), BNN) + jnp.concatenate([rt, _bdot(a_rk, vb, BNN)], axis=2)
    pb = pmat.astype(BF16)
    bhb = stack(bh2).astype(BF16)
    khb = stack(kh2).astype(BF16)
    for hh in range(2):
        for sb in range(nsb):
            bi = hh * nsb + sb
            q_ref[0, hh, sb * sub:(sb + 1) * sub, :] = qmat[bi]
            for cc in range(sub // CHUNK):
                cs = slice(cc * CHUNK, (cc + 1) * CHUNK)
                mn = _dot(pb[bi, cs], bhb[bi, cs], TN_DIMS)
                n_c = mn[hd:] + _dot(vb[bi, cs], khb[bi, cs], TN_DIMS)
                en_ref[0, hh, sb * (sub // CHUNK) + cc] = jnp.concatenate([mn[:hd], n_c], axis=0)


def _rwkv_state_kernel(q_ref, en_ref, wl_ref, r_ref, k_ref, v_ref, g_ref, rk_ref, gng_ref, gnb_ref,
                       o_ref, st_ref, *, rows, nhs):
    t = pl.program_id(2)
    hd = HEAD_DIM
    nch = rows // CHUNK

    @pl.when(t == 0)
    def _():
        st_ref[...] = jnp.zeros_like(st_ref)

    wl = wl_ref[0]
    r2, k2, v2, g2 = r_ref[0], k_ref[0], v_ref[0], g_ref[0]
    sts = [st_ref[hh] for hh in range(nhs)]
    ys = [[] for _ in range(nhs)]
    for c in range(nch):
        for hh in range(nhs):
            qc = q_ref[0, hh, c * CHUNK:(c + 1) * CHUNK, :]
            en = en_ref[0, hh, c]
            stb = sts[hh].astype(BF16)
            ys[hh].append(_dot(qc[:, :hd].astype(BF16), stb, NT_DIMS) + qc[:, hd:])
            sts[hh] = (sts[hh] * wl[c:c + 1, hh * hd:(hh + 1) * hd]
                       + _dot(stb, en[:hd].astype(BF16)) + en[hd:])
    outs = []
    for hh in range(nhs):
        sl = slice(hh * hd, (hh + 1) * hd)
        st_ref[hh] = sts[hh]
        v, r, kp = v2[:, sl], r2[:, sl], k2[:, sl]
        y = jnp.concatenate(ys[hh], axis=0)
        mu = jnp.mean(y, axis=-1, keepdims=True)
        var = jnp.mean(jnp.square(y - mu), axis=-1, keepdims=True)
        yn = (y - mu) * lax.rsqrt(var + RWKV_GN_EPS) * gng_ref[:, sl] + gnb_ref[:, sl]
        bonus = jnp.sum(r * kp * rk_ref[hh:hh + 1, :], axis=-1, keepdims=True) * v
        outs.append((yn + bonus) * g2[:, sl])
    o_ref[0] = jnp.concatenate(outs, axis=1).astype(o_ref.dtype)


def rwkv7(pm, mu, w_up, a_up, g_up, w0, a0, kk_gain, ka_gain, r_k, gn_g, gn_b, ts=256, rows=512, rows_c=1024,
          sub=256, nhs=8):
    b, s, _ = pm.shape
    d = D_BRANCH
    seg = np.kron(np.eye(d // HEAD_DIM, dtype=np.float32), np.ones((HEAD_DIM, HEAD_DIM), np.float32))
    row1 = lambda a: a.reshape(1, -1)
    cst = lambda shape: pl.BlockSpec(shape, lambda bi, i: (0,) * len(shape))
    blk = pl.BlockSpec((1, ts, d), lambda bi, i: (bi, i, 0))
    r, k, v, al, be, lw, g = pl.pallas_call(
        _rwkv_prep_kernel,
        grid=(b, s // ts),
        in_specs=[pl.BlockSpec((1, ts, B_BLOCK), lambda bi, i: (bi, i, 0)),
                  pl.BlockSpec((1, 8, B_BLOCK), lambda bi, i: (bi, jnp.maximum(i * (ts // 8) - 1, 0), 0)),
                  cst((1, B_WIDTH)), cst((64, d)), cst((64, d)), cst((128, d)),
                  cst((1, d)), cst((1, d)), cst((1, d)), cst((1, d)), cst((d, d))],
        out_specs=[blk] * 7,
        out_shape=[jax.ShapeDtypeStruct((b, s, d), F32)] * 7,
        compiler_params=_cp(("parallel", "arbitrary")),
        name="rwkv_prep",
    )(pm, pm, row1(mu), w_up, a_up, g_up, row1(w0), row1(a0), row1(kk_gain), row1(ka_gain), jnp.asarray(seg, BF16))

    rows_c = min(rows_c, s)
    hp = pl.BlockSpec((1, rows_c, LANE), lambda bi, h, t: (bi, t, h))
    nh = d // HEAD_DIM
    nch_c = rows_c // CHUNK
    nch = rows // CHUNK
    q_spec = pl.BlockSpec((1, 2, rows_c, LANE), lambda bi, h, t: (bi, h, t, 0))
    en_spec = pl.BlockSpec((1, 2, nch_c, LANE, HEAD_DIM), lambda bi, h, t: (bi, h, t, 0, 0))
    wl_spec = pl.BlockSpec((1, nch_c, LANE), lambda bi, h, t: (bi, t, h))
    qm, en, wl = pl.pallas_call(
        functools.partial(_rwkv_chunk_kernel, rows=rows_c, sub=sub),
        grid=(b, d // LANE, s // rows_c),
        in_specs=[hp] * 6,
        out_specs=[q_spec, en_spec, wl_spec],
        out_shape=[jax.ShapeDtypeStruct((b, nh, s, LANE), F32),
                   jax.ShapeDtypeStruct((b, nh, s // CHUNK, LANE, HEAD_DIM), F32),
                   jax.ShapeDtypeStruct((b, s // CHUNK, d), F32)],
        compiler_params=_cp(("parallel", "parallel", "parallel")),
        name="rwkv_chunk",
    )(r, k, v, al, be, lw)
    wide = nhs * HEAD_DIM
    rows_s = pl.BlockSpec((1, rows, wide), lambda bi, h, t: (bi, t, h))
    par_s = pl.BlockSpec((1, wide), lambda bi, h, t: (0, h))
    return pl.pallas_call(
        functools.partial(_rwkv_state_kernel, rows=rows, nhs=nhs),
        grid=(b, nh // nhs, s // rows),
        in_specs=[pl.BlockSpec((1, nhs, rows, LANE), lambda bi, h, t: (bi, h, t, 0)),
                  pl.BlockSpec((1, nhs, nch, LANE, HEAD_DIM), lambda bi, h, t: (bi, h, t, 0, 0)),
                  pl.BlockSpec((1, nch, wide), lambda bi, h, t: (bi, t, h)),
                  rows_s, rows_s, rows_s, rows_s,
                  pl.BlockSpec((nhs, HEAD_DIM), lambda bi, h, t: (h, 0)), par_s, par_s],
        out_specs=rows_s,
        out_shape=jax.ShapeDtypeStruct((b, s, d), BF16),
        scratch_shapes=[pltpu.VMEM((nhs, HEAD_DIM, HEAD_DIM), F32)],
        compiler_params=_cp(("parallel", "parallel", "arbitrary")),
        name="rwkv_state",
    )(qm, en, wl, r, k, v, g, r_k, row1(gn_g), row1(gn_b))


def _hgrn_kernel(q_ref, f_ref, i_ref, g_ref, lb_ref, ng_ref, o_ref, st_ref, *, rows, nh):
    t = pl.program_id(2)
    sub = 16

    @pl.when(t == 0)
    def _():
        st_ref[...] = jnp.zeros_like(st_ref)

    q = jax.nn.silu(q_ref[0])
    lb = lb_ref[...]
    fg = lb + (1.0 - lb) * jax.nn.sigmoid(f_ref[0])
    lf = jnp.log(fg) * LOG2E
    kf = 1.0 - fg
    iv = i_ref[0]
    rowc = lax.broadcasted_iota(jnp.int32, lf.shape, 0) & (CHUNK - 1)
    bcum = lf
    for dd in (1, 2, 4, 8, 16, 32):
        bcum = bcum + jnp.where(rowc >= dd, pltpu.roll(bcum, dd, axis=0), 0.0)
    ivb = iv.astype(BF16)
    rsub = lax.broadcasted_iota(jnp.int32, (sub, LANE), 0)
    sts = [st_ref[hh] for hh in range(nh)]
    outs = [[] for _ in range(nh)]
    for c in range(rows // CHUNK):
        for hh in range(nh):
            rs = slice(c * CHUNK, (c + 1) * CHUNK)
            ls = slice(hh * LANE, (hh + 1) * LANE)
            bc, qc, kc, ic, icb = bcum[rs, ls], q[rs, ls], kf[rs, ls], iv[rs, ls], ivb[rs, ls]
            o_inter = _dot((qc * jnp.exp2(bc)).astype(BF16), sts[hh].astype(BF16), NT_DIMS)
            blocks = []
            for ib in range(CHUNK // sub):
                r0 = ib * sub
                bi = bc[r0:r0 + sub]
                qi = qc[r0:r0 + sub]
                ki = kc[r0:r0 + sub]
                ii = ic[r0:r0 + sub]
                acc = jnp.zeros((sub, LANE), F32)
                for s_ in range(sub):
                    e = jnp.where(rsub >= s_, jnp.exp2(jnp.minimum(bi - bi[s_:s_ + 1], 0.0)), 0.0)
                    a = jnp.sum(qi * (ki[s_:s_ + 1] * e), axis=-1, keepdims=True)
                    acc = acc + a * ii[s_:s_ + 1]
                if ib > 0:
                    ref_row = bc[r0 - 1:r0]
                    qt = (qi * jnp.exp2(bi - ref_row)).astype(BF16)
                    kt = (kc[0:r0] * jnp.exp2(ref_row - bc[0:r0])).astype(BF16)
                    a_off = _dot(qt, kt, NT_DIMS).astype(BF16)
                    acc = acc + _dot(a_off, icb[0:r0])
                blocks.append(acc)
            outs[hh].append(o_inter + jnp.concatenate(blocks, axis=0))
            bl = bc[CHUNK - 1:CHUNK]
            kdec = (kc * jnp.exp2(bl - bc)).astype(BF16)
            sts[hh] = sts[hh] * jnp.exp2(bl) + _dot(icb, kdec, TN_DIMS)
    res = []
    for hh in range(nh):
        st_ref[hh] = sts[hh]
        o = jnp.concatenate(outs[hh], axis=0)
        ms = jnp.mean(o * o, axis=-1, keepdims=True)
        res.append(o * lax.rsqrt(ms + EPS) * ng_ref[:, hh * LANE:(hh + 1) * LANE])
    o_ref[0] = (jnp.concatenate(res, axis=1) * jax.nn.silu(g_ref[0])).astype(o_ref.dtype)


def hgrn2(pm, lb, norm_g, rows=256, nh=2):
    b, s, _ = pm.shape
    width = nh * LANE
    spec = lambda u: pl.BlockSpec((1, rows, width), lambda bi, h, t: (bi, t, u // nh + h))
    par = pl.BlockSpec((1, width), lambda bi, h, t: (0, h))
    return pl.pallas_call(
        functools.partial(_hgrn_kernel, rows=rows, nh=nh),
        grid=(b, D_BRANCH // width, s // rows),
        in_specs=[spec(U_CQ), spec(U_CF), spec(U_CI), spec(U_CG), par, par],
        out_specs=spec(0),
        out_shape=jax.ShapeDtypeStruct((b, s, D_BRANCH), BF16),
        scratch_shapes=[pltpu.VMEM((nh, LANE, LANE), F32)],
        compiler_params=_cp(("parallel", "parallel", "arbitrary")),
        name="hgrn2",
    )(pm, pm, pm, pm, lb.reshape(1, -1), norm_g.reshape(1, -1))


def _dsa_prep_kernel(lo_ref, hi_ref, cos_ref, sin_ref,
                     qo_ref, iqo_ref, ko_ref, vo_ref, iko_ref, wo_ref, *, scale, wscale):
    cos = cos_ref[0]
    sin = sin_ref[0]
    hd = HEAD_DIM
    lo = lo_ref[0]
    hi = hi_ref[0]
    col = lambda u: (u - U_D_LO) * LANE
    q = (_rope(lo[:, col(U_DQ):col(U_DKV)], cos, sin) * scale).astype(BF16)
    iq_in = jnp.concatenate([lo[:, col(U_DIQ):], hi[:, :(U_DIKW - U_D_HI) * LANE]], axis=1)
    iq = _rope(iq_in, cos, sin).astype(BF16)
    for h in range(DSA_HEADS):
        qo_ref[0, h] = q[:, h * hd:(h + 1) * hd]
        iqo_ref[0, h] = iq[:, h * hd:(h + 1) * hd]
    kv = lo[:, col(U_DKV):col(U_DIQ)]
    ko_ref[0] = _rope(kv, cos, sin)[:, :hd].astype(BF16)
    vo_ref[0] = kv[:, hd:].astype(BF16)
    ikw = hi[:, (U_DIKW - U_D_HI) * LANE:]
    iko_ref[0] = _rope(ikw, cos, sin)[:, :hd].astype(BF16)
    wo_ref[0] = ikw[:, hd:hd + IDX_HEADS] * wscale


def _to_key(x):
    x = jnp.where(x == 0.0, 0.0, x)
    bits = pltpu.bitcast(x, jnp.int32)
    return jnp.where(bits < 0, bits ^ jnp.int32(0x7FFFFFFF), bits)


def _dsa_index_kernel(iq_ref, ik_ref, wt_ref, mask_ref, key_ref, *, tq, tk, top_k):
    i = pl.program_id(1)
    nk = key_ref.shape[0]
    nvis = (i * tq + tq - 1) // tk + 1
    krow = lax.broadcasted_iota(jnp.int32, (tk, tq), 0)
    qcol = i * tq + lax.broadcasted_iota(jnp.int32, (tk, tq), 1)
    wt = wt_ref[0]

    def score_block(jb):
        ikb = ik_ref[0, pl.ds(pl.multiple_of(jb * tk, tk), tk), :]
        s_all = _dot(ikb, iq_ref[0].reshape(IDX_HEADS * tq, HEAD_DIM), NT_DIMS)
        acc = jnp.zeros((tk, tq), F32)
        for h in range(IDX_HEADS):
            acc = acc + jnp.maximum(s_all[:, h * tq:(h + 1) * tq], 0.0) * wt[h:h + 1, :]
        vis = ((jb * tk + krow) >> 6) <= (qcol >> 6)
        key_ref[jb] = jnp.where(vis, _to_key(acc), INT_MIN)

    _paired_loop(nvis, score_block)

    def count(pred_fn):
        def blk(jb, cnt):
            ind = jnp.where(pred_fn(key_ref[jb], jb), 1, 0)
            return cnt + jnp.sum(ind.reshape(tk // 8, 8, tq), axis=0)
        cnt = lax.fori_loop(0, nvis, blk, jnp.zeros((8, tq), jnp.int32))
        return jnp.sum(cnt, axis=0, keepdims=True)

    def bit_step(bi, carry):
        prefix, n_ge = carry
        cand = prefix | lax.shift_left(jnp.int32(1), 31 - bi)
        cand_s = cand ^ jnp.int32(INT_MIN)
        c = count(lambda kb, jb: kb >= cand_s)
        keep = c >= top_k
        return jnp.where(keep, cand, prefix), jnp.where(keep, c, n_ge)

    prefix, n_ge = lax.fori_loop(0, 32, bit_step, (jnp.zeros((1, tq), jnp.int32),
                                                   jnp.zeros((1, tq), jnp.int32) + nvis * tk))
    tau = prefix ^ jnp.int32(INT_MIN)
    n_gt = count(lambda kb, jb: kb > tau)
    n_eq = n_ge - n_gt
    need = top_k - n_gt
    tied = (n_eq > need) & (tau != INT_MIN)
    n_cols = nk * tk

    idx_bits = int(n_cols).bit_length()

    def tie_break():
        def idx_step(bi, pre):
            cand = pre | lax.shift_left(jnp.int32(1), idx_bits - 1 - bi)
            c = count(lambda kb, jb: (kb == tau) & ((jb * tk + krow) < cand))
            return jnp.where(c < need, cand, pre)
        cut = lax.fori_loop(0, idx_bits, idx_step, jnp.zeros((1, tq), jnp.int32))
        return jnp.where(tied, cut, n_cols)

    cut = lax.cond(jnp.max(tied.astype(jnp.int32)) > 0, tie_break,
                   lambda: jnp.full((1, tq), n_cols, jnp.int32))

    def write_block(jb, carry):
        kb = key_ref[jb]
        sel = (kb > tau) | ((kb == tau) & ((jb * tk + krow) <= cut))
        sel = sel & (kb != INT_MIN)
        mask_ref[0, jb] = jnp.where(sel, 1.0, 0.0).T.astype(mask_ref.dtype)
        return carry

    def zero_block(jb, carry):
        mask_ref[0, jb] = jnp.zeros((tq, tk), mask_ref.dtype)
        return carry

    lax.fori_loop(0, nvis, write_block, 0)
    lax.fori_loop(nvis, nk, zero_block, 0)


def _dsa_attn_kernel(q_ref, k_ref, v_ref, mask_ref, o_ref, s_ref, m_ref, l_ref, acc_ref, *, tq, tk):
    i = pl.program_id(1)
    nh = DSA_HEADS
    hd = HEAD_DIM
    nvis = (i * tq + tq - 1) // tk + 1
    nchunk = tk // LANE

    m_ref[...] = jnp.full_like(m_ref, NEG)

    def max_step(j):
        kb = k_ref[0, pl.ds(pl.multiple_of(j * tk, tk), tk), :]
        s = _dot(q_ref[0].reshape(nh * tq, hd), kb, NT_DIMS).reshape(nh, tq, tk)
        s = jnp.where((mask_ref[0, j] > 0)[None], s, NEG)
        s_ref[j] = s
        m_ref[...] = jnp.maximum(m_ref[...], _lane_fold(s, jnp.maximum))

    _paired_loop(nvis, max_step)
    m_ref[...] = jnp.broadcast_to(jnp.max(m_ref[...], axis=2, keepdims=True), m_ref.shape)
    l_ref[...] = jnp.zeros_like(l_ref)
    acc_ref[...] = jnp.zeros_like(acc_ref)

    def acc_step(j):
        s = s_ref[j]
        mb = m_ref[...]
        ps = [jnp.exp2(s[:, :, c * LANE:(c + 1) * LANE] - mb) for c in range(nchunk)]
        l_ref[...] += functools.reduce(jnp.add, ps)
        p = jnp.concatenate(ps, axis=2).astype(BF16).reshape(nh * tq, tk)
        vb = v_ref[0, pl.ds(pl.multiple_of(j * tk, tk), tk), :]
        acc_ref[...] += _dot(p, vb).reshape(nh, tq, hd)

    _paired_loop(nvis, acc_step)
    o = acc_ref[...] / jnp.sum(l_ref[...], axis=2, keepdims=True)
    o_ref[0] = jnp.concatenate([o[h] for h in range(nh)], axis=1).astype(o_ref.dtype)


def dsa_attention(pm, cos, sin, ts=512, tq=128, tk=512):
    b, s, _ = pm.shape
    top_k = min(TOPK_MAX, s // 4)
    hd = HEAD_DIM
    nk = s // tk
    tab = pl.BlockSpec((1, ts, LANE), lambda bi, i: (bi, i, 0))
    heads_out = pl.BlockSpec((1, DSA_HEADS, ts, hd), lambda bi, i: (bi, 0, i, 0))
    narrow = lambda w: pl.BlockSpec((1, ts, w), lambda bi, i: (bi, i, 0))
    w_lo = (U_D_HI - U_D_LO) * LANE
    w_hi = N_MAIN - U_D_HI * LANE
    q, iq, k, v, ik, w = pl.pallas_call(
        functools.partial(_dsa_prep_kernel, scale=hd ** -0.5 * LOG2E, wscale=(IDX_HEADS ** -0.5) * (hd ** -0.5)),
        grid=(b, s // ts),
        in_specs=[pl.BlockSpec((1, ts, w_lo), lambda bi, i: (bi, i, U_D_LO * LANE // w_lo)),
                  pl.BlockSpec((1, ts, w_hi), lambda bi, i: (bi, i, U_D_HI * LANE // w_hi)),
                  tab, tab],
        out_specs=[heads_out, heads_out, narrow(hd), narrow(hd), narrow(hd), narrow(IDX_HEADS)],
        out_shape=[jax.ShapeDtypeStruct((b, DSA_HEADS, s, hd), BF16)] * 2
                  + [jax.ShapeDtypeStruct((b, s, hd), BF16)] * 3
                  + [jax.ShapeDtypeStruct((b, s, IDX_HEADS), F32)],
        compiler_params=_cp(("parallel", "parallel")),
        name="dsa_prep",
    )(pm, pm, cos, sin)

    mask_spec = pl.BlockSpec((1, nk, tq, tk), lambda bi, i: (bi, 0, i, 0))
    whole = pl.BlockSpec((1, s, hd), lambda bi, i: (bi, 0, 0))
    mask = pl.pallas_call(
        functools.partial(_dsa_index_kernel, tq=tq, tk=tk, top_k=top_k),
        grid=(b, s // tq),
        in_specs=[pl.BlockSpec((1, IDX_HEADS, tq, hd), lambda bi, i: (bi, 0, i, 0)),
                  whole,
                  pl.BlockSpec((1, IDX_HEADS, tq), lambda bi, i: (bi, 0, i))],
        out_specs=mask_spec,
        out_shape=jax.ShapeDtypeStruct((b, nk, s, tk), BF16),
        scratch_shapes=[pltpu.VMEM((nk, tk, tq), jnp.int32)],
        compiler_params=_cp(("parallel", "parallel")),
        name="dsa_index",
    )(iq, ik, jnp.swapaxes(w, 1, 2))

    return pl.pallas_call(
        functools.partial(_dsa_attn_kernel, tq=tq, tk=tk),
        grid=(b, s // tq),
        in_specs=[pl.BlockSpec((1, DSA_HEADS, tq, hd), lambda bi, i: (bi, 0, i, 0)),
                  whole, whole, mask_spec],
        out_specs=pl.BlockSpec((1, tq, D_BRANCH), lambda bi, i: (bi, i, 0)),
        out_shape=jax.ShapeDtypeStruct((b, s, D_BRANCH), BF16),
        scratch_shapes=[pltpu.VMEM((nk, DSA_HEADS, tq, tk), F32),
                        pltpu.VMEM((DSA_HEADS, tq, LANE), F32), pltpu.VMEM((DSA_HEADS, tq, LANE), F32),
                        pltpu.VMEM((DSA_HEADS, tq, hd), F32)],
        compiler_params=_cp(("parallel", "parallel")),
        name="dsa_attention",
    )(q, k, v, mask)


GATE_COL = 6600


def kernel(x, positions, norm_g, w_in, diff_lambda, diff_subln_g, rwkv_mu, rwkv_w_up, rwkv_a_up, rwkv_g_up, rwkv_w0, rwkv_a0, rwkv_k_k, rwkv_k_a, rwkv_r_k, rwkv_gn_g, rwkv_gn_b, hgrn_lb_logits, hgrn_norm_g, w_branch, w_out, mlp_w1, mlp_w2):
    b, s, d = x.shape
    m = b * s
    depth = w_in.shape[0]
    lb = jax.nn.softmax(hgrn_lb_logits.astype(F32), axis=0)
    lb = jnp.cumsum(lb, axis=0) - lb[0]
    w_in_t = jnp.swapaxes(w_in, 1, 2)
    w_out_b = w_out.astype(BF16)
    w2_b = mlp_w2.astype(BF16)
    cos, sin = rope_tables(positions)
    cos3 = cos.reshape(b, s, LANE)
    sin3 = sin.reshape(b, s, LANE)
    x2 = x.reshape(m, d)
    h = rmsnorm_bf16(x2, norm_g[0, 0])
    for l in range(depth):
        pm = matmul_wt(h, w_in_t, l, 0, N_MAIN, None, F32, name="proj_main").reshape(b, s, N_MAIN)
        gate = matmul_wt(h, w_in_t, l, GATE_COL, N_BRANCH * D_MODEL, "sigmoid", BF16, tn=1024, name="proj_gate")
        y_a = diff_attention(pm, cos3, sin3, diff_lambda[l], diff_subln_g[l], l)
        y_b = rwkv7(pm, rwkv_mu[l], rwkv_w_up[l], rwkv_a_up[l], rwkv_g_up[l], rwkv_w0[l], rwkv_a0[l],
                    rwkv_k_k[l], rwkv_k_a[l], rwkv_r_k[l], rwkv_gn_g[l], rwkv_gn_b[l])
        y_c = hgrn2(pm, lb[l], hgrn_norm_g[l])
        y_d = dsa_attention(pm, cos3, sin3)
        ys = [y.reshape(m, D_BRANCH) for y in (y_a, y_b, y_c, y_d)]
        merged = gated_merge(ys, w_branch, l, gate)
        x2, h2 = matmul_norm_residual(merged, w_out_b, l, x2, norm_g[l, 1], norm_g[l, 2], name="out_proj")
        ff = matmul(h2, mlp_w1, l, "relu2", BF16, tn=1024, name="mlp_up")
        if l + 1 < depth:
            x2, h = matmul_norm_residual(ff, w2_b, l, x2, norm_g[l, 3], norm_g[l + 1, 0], name="mlp_down")
        else:
            x2 = matmul_norm_residual(ff, w2_b, l, x2, norm_g[l, 3], name="mlp_down")
    return x2.reshape(b, s, d)
```

```python
import functools
import math

import numpy as np
import jax
import jax.numpy as jnp
from jax import lax
from jax.experimental import pallas as pl
from jax.experimental.pallas import tpu as pltpu

F32 = jnp.float32
BF16 = jnp.bfloat16

D_MODEL = 2048
D_BRANCH = 512
D_FF = 8192
N_BRANCH = 4
CHUNK = 64
EPS = 1e-6
ROPE_THETA = 10000.0
HEAD_DIM = 64
DIFF_HEADS = 4
RWKV_GN_EPS = 64e-5
DSA_HEADS = 8
IDX_HEADS = 8
TOPK_MAX = 256

LANE = 128
VMEM_LIMIT = 56 * 1024 * 1024

U_AQ, U_AK, U_AV = 0, 4, 8
U_B = 12
U_CQ, U_CF, U_CI, U_CG = 26, 30, 34, 38
U_DQ, U_DKV, U_DIQ, U_DIKW = 42, 46, 47, 51
U_D_LO, U_D_HI = 40, 48
N_MAIN = 52 * LANE
B_WIDTH = 1792
B_BLOCK = 3584

NEG = -1e30
INT_MIN = -2147483648
LOG2E = 1.4426950408889634

NT_DIMS = (((1,), (1,)), ((), ()))
TN_DIMS = (((0,), (0,)), ((), ()))


def _cp(sem, vmem=VMEM_LIMIT):
    return pltpu.CompilerParams(dimension_semantics=sem, vmem_limit_bytes=vmem)


def _dot(a, b, dims=None):
    if dims is None:
        return jnp.dot(a, b, preferred_element_type=F32)
    return lax.dot_general(a, b, dims, preferred_element_type=F32)


def _split2(a):
    hi = a.astype(BF16)
    return hi, (a - hi.astype(F32)).astype(BF16)


def _dot_split(a, b):
    a_hi, a_lo = _split2(a)
    b_hi, b_lo = _split2(b)
    return _dot(a_hi, b_hi) + (_dot(a_hi, b_lo) + _dot(a_lo, b_hi))


def _rmsnorm_kernel(x_ref, g_ref, o_ref):
    x = x_ref[...]
    ms = jnp.mean(x * x, axis=-1, keepdims=True)
    o_ref[...] = (x * lax.rsqrt(ms + EPS) * g_ref[...]).astype(o_ref.dtype)


def rmsnorm_bf16(x, g, tm=512):
    m, d = x.shape
    return pl.pallas_call(
        _rmsnorm_kernel,
        grid=(m // tm,),
        in_specs=[pl.BlockSpec((tm, d), lambda i: (i, 0)),
                  pl.BlockSpec((1, d), lambda i: (0, 0))],
        out_specs=pl.BlockSpec((tm, d), lambda i: (i, 0)),
        out_shape=jax.ShapeDtypeStruct((m, d), BF16),
        compiler_params=_cp(("parallel",)),
        name="rmsnorm",
    )(x, g.reshape(1, d))


def _mm_kernel(x_ref, w_ref, o_ref, *, act, dims):
    w = w_ref[...] if len(w_ref.shape) == 2 else w_ref[0]
    acc = _dot(x_ref[...], w.astype(BF16), dims)
    if act == "sigmoid":
        acc = 0.5 * jnp.tanh(0.5 * acc) + 0.5
    elif act == "relu2":
        acc = jnp.square(jnp.maximum(acc, 0.0))
    o_ref[...] = acc.astype(o_ref.dtype)


def matmul(x, w, layer, act, out_dtype, tm=2048, tn=512, name="mm"):
    m, k = x.shape
    n = w.shape[2]
    tm = min(tm, m)
    return pl.pallas_call(
        functools.partial(_mm_kernel, act=act, dims=None),
        grid=(m // tm, n // tn),
        in_specs=[pl.BlockSpec((tm, k), lambda i, j: (i, 0)),
                  pl.BlockSpec((None, k, tn), lambda i, j: (layer, 0, j))],
        out_specs=pl.BlockSpec((tm, tn), lambda i, j: (i, j)),
        out_shape=jax.ShapeDtypeStruct((m, n), out_dtype),
        compiler_params=_cp(("parallel", "parallel")),
        name=name,
    )(x, w)


def matmul_wt(x, wt, layer, row0, n, act, out_dtype, tm=2048, tn=512, name="mm_wt"):
    m, k = x.shape
    tm = min(tm, m)
    return pl.pallas_call(
        functools.partial(_mm_kernel, act=act, dims=NT_DIMS),
        grid=(m // tm, n // tn),
        in_specs=[pl.BlockSpec((tm, k), lambda i, j: (i, 0)),
                  pl.BlockSpec((pl.Element(1), pl.Element(tn), pl.Element(k)),
                               lambda i, j: (layer, pl.multiple_of(row0 + j * tn, 8), 0))],
        out_specs=pl.BlockSpec((tm, tn), lambda i, j: (i, j)),
        out_shape=jax.ShapeDtypeStruct((m, n), out_dtype),
        compiler_params=_cp(("parallel", "parallel")),
        name=name,
    )(x, wt)


def _norm_residual(y, x_ref, g_ref, o_ref, h_ref):
    ms = jnp.mean(y * y, axis=-1, keepdims=True)
    o = x_ref[...] + y * lax.rsqrt(ms + EPS) * g_ref[0:1, :]
    o_ref[...] = o
    if h_ref is not None:
        ms = jnp.mean(o * o, axis=-1, keepdims=True)
        h_ref[...] = (o * lax.rsqrt(ms + EPS) * g_ref[1:2, :]).astype(h_ref.dtype)


def _mm_norm_res_kernel(a_ref, w_ref, x_ref, g_ref, o_ref, *rest):
    h_ref, acc_ref = (rest[0], rest[1]) if len(rest) == 2 else (None, rest[0])
    kk = pl.program_id(1)

    @pl.when(kk == 0)
    def _():
        acc_ref[...] = jnp.zeros_like(acc_ref)

    acc_ref[...] += _dot(a_ref[...], w_ref[...])

    @pl.when(kk == pl.num_programs(1) - 1)
    def _():
        _norm_residual(acc_ref[...], x_ref, g_ref, o_ref, h_ref)


def _mm_norm_res_fullk_kernel(a_ref, w_ref, x_ref, g_ref, o_ref, *rest):
    _norm_residual(_dot(a_ref[...], w_ref[...]), x_ref, g_ref, o_ref, rest[0] if rest else None)


def matmul_norm_residual(a, w, layer, x, g, g_next=None, tm=512, tk=2048, name="mm_norm_res"):
    m, k = a.shape
    n = w.shape[2]
    gs = jnp.stack([g, g if g_next is None else g_next])
    out_shape = [jax.ShapeDtypeStruct((m, n), F32)]
    if g_next is not None:
        out_shape.append(jax.ShapeDtypeStruct((m, n), BF16))
    if k <= tk:
        row = lambda i: (i, 0)
        outs = pl.pallas_call(
            _mm_norm_res_fullk_kernel,
            grid=(m // tm,),
            in_specs=[pl.BlockSpec((tm, k), row), pl.BlockSpec((None, k, n), lambda i: (layer, 0, 0)),
                      pl.BlockSpec((tm, n), row), pl.BlockSpec((2, n), lambda i: (0, 0))],
            out_specs=[pl.BlockSpec((tm, n), row)] * len(out_shape),
            out_shape=out_shape,
            compiler_params=_cp(("parallel",)),
            name=name,
        )(a, w, x, gs)
    else:
        row = lambda i, kk: (i, 0)
        outs = pl.pallas_call(
            _mm_norm_res_kernel,
            grid=(m // tm, k // tk),
            in_specs=[pl.BlockSpec((tm, tk), lambda i, kk: (i, kk)),
                      pl.BlockSpec((None, tk, n), lambda i, kk: (layer, kk, 0)),
                      pl.BlockSpec((tm, n), row),
                      pl.BlockSpec((2, n), lambda i, kk: (0, 0))],
            out_specs=[pl.BlockSpec((tm, n), row)] * len(out_shape),
            out_shape=out_shape,
            scratch_shapes=[pltpu.VMEM((tm, n), F32)],
            compiler_params=_cp(("parallel", "arbitrary")),
            name=name,
        )(a, w, x, gs)
    return outs if g_next is not None else outs[0]


def _merge_kernel(ya, yb, yc, yd, wb, ga, gb, gc, gd, o_ref):
    acc = None
    for n, (y, g) in enumerate(((ya, ga), (yb, gb), (yc, gc), (yd, gd))):
        t = g[...].astype(F32) * _dot(y[...], wb[n].astype(BF16))
        acc = t if acc is None else acc + t
    o_ref[...] = acc.astype(o_ref.dtype)


def gated_merge(ys, wb, layer, gate, tm=512, tn=1024):
    m = ys[0].shape[0]
    nj = D_MODEL // tn
    y_spec = pl.BlockSpec((tm, D_BRANCH), lambda j, i: (i, 0))
    g_specs = [pl.BlockSpec((tm, tn), functools.partial(lambda j, i, n: (i, n * nj + j), n=n))
               for n in range(N_BRANCH)]
    return pl.pallas_call(
        _merge_kernel,
        grid=(nj, m // tm),
        in_specs=[y_spec] * 4 + [pl.BlockSpec((None, N_BRANCH, D_BRANCH, tn), lambda j, i: (layer, 0, 0, j))]
                 + g_specs,
        out_specs=pl.BlockSpec((tm, tn), lambda j, i: (i, j)),
        out_shape=jax.ShapeDtypeStruct((m, D_MODEL), BF16),
        compiler_params=_cp(("parallel", "parallel")),
        name="gated_merge",
    )(*ys, wb, gate, gate, gate, gate)


def _rope_table_kernel(pos_ref, inv_ref, sgn_ref, cos_ref, sin_ref):
    ang = pos_ref[...].astype(F32) * inv_ref[...]
    cos_ref[...] = jnp.cos(ang)
    sin_ref[...] = jnp.sin(ang) * sgn_ref[...]


def rope_tables(positions, tm=512):
    m = positions.size
    half = HEAD_DIM // 2
    inv = ROPE_THETA ** (-np.arange(0, HEAD_DIM, 2, dtype=np.float32) / HEAD_DIM)
    inv = np.tile(inv.astype(np.float32), 4).reshape(1, LANE)
    sgn = np.tile(np.concatenate([-np.ones(half, np.float32), np.ones(half, np.float32)]), 2).reshape(1, LANE)
    spec = pl.BlockSpec((tm, LANE), lambda i: (i, 0))
    cst = pl.BlockSpec((1, LANE), lambda i: (0, 0))
    return pl.pallas_call(
        _rope_table_kernel,
        grid=(m // tm,),
        in_specs=[pl.BlockSpec((tm, 1), lambda i: (i, 0)), cst, cst],
        out_specs=[spec, spec],
        out_shape=[jax.ShapeDtypeStruct((m, LANE), F32)] * 2,
        compiler_params=_cp(("parallel",)),
        name="rope_tables",
    )(positions.reshape(m, 1), jnp.asarray(inv), jnp.asarray(sgn))


def _rope(x, cos, sin):
    w = x.shape[1]
    n = w // LANE
    if n > 1:
        cos = jnp.concatenate([cos] * n, axis=1)
        sin = jnp.concatenate([sin] * n, axis=1)
    lane = lax.broadcasted_iota(jnp.int32, x.shape, 1)
    up = pltpu.roll(x, w - HEAD_DIM // 2, axis=1)
    dn = pltpu.roll(x, HEAD_DIM // 2, axis=1)
    rot = jnp.where((lane & (HEAD_DIM // 2)) == 0, up, dn)
    return x * cos + rot * sin


def _rope_qkv_kernel(q_ref, k_ref, v_ref, cos_ref, sin_ref, qo_ref, ko_ref, vo_ref, *, scale):
    cos = cos_ref[0]
    sin = sin_ref[0]
    hd = HEAD_DIM
    q = (_rope(q_ref[0], cos, sin) * scale).astype(qo_ref.dtype)
    k = _rope(k_ref[0], cos, sin).astype(ko_ref.dtype)
    for h in range(q.shape[1] // hd):
        qo_ref[0, h] = q[:, h * hd:(h + 1) * hd]
        ko_ref[0, h] = k[:, h * hd:(h + 1) * hd]
    vo_ref[0] = v_ref[0].astype(vo_ref.dtype)


def rope_qkv(pm, cos, sin, uq, uk, uv, scale, ts=512):
    b, s, _ = pm.shape
    wq = D_BRANCH
    nh = wq // HEAD_DIM
    tab = pl.BlockSpec((1, ts, LANE), lambda bi, i: (bi, i, 0))
    col = lambda u: pl.BlockSpec((1, ts, wq), lambda bi, i: (bi, i, u * LANE // wq))
    heads = pl.BlockSpec((1, nh, ts, HEAD_DIM), lambda bi, i: (bi, 0, i, 0))
    return pl.pallas_call(
        functools.partial(_rope_qkv_kernel, scale=scale),
        grid=(b, s // ts),
        in_specs=[col(uq), col(uk), col(uv), tab, tab],
        out_specs=[heads, heads, pl.BlockSpec((1, ts, wq), lambda bi, i: (bi, i, 0))],
        out_shape=[jax.ShapeDtypeStruct((b, nh, s, HEAD_DIM), BF16)] * 2
                  + [jax.ShapeDtypeStruct((b, s, wq), BF16)],
        compiler_params=_cp(("parallel", "parallel")),
        name="rope_qkv",
    )(pm, pm, pm, cos, sin)


def _lane_fold(x, op):
    out = x[..., 0:LANE]
    for c in range(1, x.shape[-1] // LANE):
        out = op(out, x[..., c * LANE:(c + 1) * LANE])
    return out


def _paired_loop(n, step):
    def body(jj, carry):
        step(2 * jj)
        step(2 * jj + 1)
        return carry
    lax.fori_loop(0, n // 2, body, 0)

    @pl.when(n % 2 == 1)
    def _():
        step(n - 1)


def _diff_attn_kernel(q_ref, k_ref, v_ref, lam_ref, g_ref, o_ref, s_ref, m_ref, l_ref, acc_ref, *, tq, tk, lam_init):
    i = pl.program_id(2)
    nfull = (i * tq) // tk
    row = lax.broadcasted_iota(jnp.int32, (tq, tk), 0)
    col = lax.broadcasted_iota(jnp.int32, (tq, tk), 1)
    vis = (col >> 6) <= (row >> 6)

    m_ref[...] = jnp.full_like(m_ref, NEG)

    def max_step(j, masked=False):
        for mm in range(2):
            kb = k_ref[0, mm, pl.ds(pl.multiple_of(j * tk, tk), tk), :]
            s = _dot(q_ref[0, mm], kb, NT_DIMS)
            if masked:
                s = jnp.where(vis, s, NEG)
            s_ref[j, mm] = s
            m_ref[mm] = jnp.maximum(m_ref[mm], _lane_fold(s, jnp.maximum))

    _paired_loop(nfull, max_step)
    max_step(nfull, True)
    for mm in range(2):
        m_ref[mm] = jnp.broadcast_to(jnp.max(m_ref[mm], axis=1, keepdims=True), (tq, LANE))
    l_ref[...] = jnp.zeros_like(l_ref)
    acc_ref[...] = jnp.zeros_like(acc_ref)

    def acc_step(j):
        vb = v_ref[0, pl.ds(pl.multiple_of(j * tk, tk), tk), :]
        for mm in range(2):
            s = s_ref[j, mm]
            mb = m_ref[mm]
            ps = [jnp.exp2(s[:, c * LANE:(c + 1) * LANE] - mb) for c in range(tk // LANE)]
            l_ref[mm] += functools.reduce(jnp.add, ps)
            acc_ref[mm] += _dot(jnp.concatenate(ps, axis=1).astype(BF16), vb)

    _paired_loop(nfull + 1, acc_step)

    lv = lam_ref[...]
    lam = (jnp.exp(jnp.sum(lv[0:1] * lv[1:2], keepdims=True))
           - jnp.exp(jnp.sum(lv[2:3] * lv[3:4], keepdims=True)) + lam_init)
    l0 = jnp.sum(l_ref[0], axis=1, keepdims=True)
    l1 = jnp.sum(l_ref[1], axis=1, keepdims=True)
    o = acc_ref[0] / l0 - lam * (acc_ref[1] / l1)
    ms = jnp.mean(o * o, axis=-1, keepdims=True)
    o = o * lax.rsqrt(ms + EPS) * g_ref[...] * (1.0 - lam_init)
    o_ref[0] = o.astype(o_ref.dtype)


def diff_attention(pm, cos, sin, lam_vecs, subln_g, layer, tq=512):
    b, s, _ = pm.shape
    tk = tq
    qa, ka, va = rope_qkv(pm, cos, sin, U_AQ, U_AK, U_AV, HEAD_DIM ** -0.5 * LOG2E)
    lam_init = 0.8 - 0.6 * math.exp(-0.3 * layer)
    return pl.pallas_call(
        functools.partial(_diff_attn_kernel, tq=tq, tk=tk, lam_init=lam_init),
        grid=(b, DIFF_HEADS, s // tq),
        in_specs=[pl.BlockSpec((1, 2, tq, HEAD_DIM), lambda bi, h, i: (bi, h, i, 0)),
                  pl.BlockSpec((1, 2, s, HEAD_DIM), lambda bi, h, i: (bi, h, 0, 0)),
                  pl.BlockSpec((1, s, LANE), lambda bi, h, i: (bi, 0, h)),
                  pl.BlockSpec((4, HEAD_DIM), lambda bi, h, i: (0, 0)),
                  pl.BlockSpec((1, LANE), lambda bi, h, i: (0, 0))],
        out_specs=pl.BlockSpec((1, tq, LANE), lambda bi, h, i: (bi, i, h)),
        out_shape=jax.ShapeDtypeStruct((b, s, D_BRANCH), BF16),
        scratch_shapes=[pltpu.VMEM((s // tk, 2, tq, tk), F32),
                        pltpu.VMEM((2, tq, LANE), F32), pltpu.VMEM((2, tq, LANE), F32),
                        pltpu.VMEM((2, tq, LANE), F32)],
        compiler_params=_cp(("parallel", "parallel", "parallel")),
        name="diff_attention",
    )(qa, ka, va, lam_vecs, subln_g.reshape(1, LANE))


def _softplus(z):
    return jnp.maximum(z, 0.0) + jnp.log(1.0 + jnp.exp(-jnp.abs(z)))


def _rwkv_prep_kernel(x_ref, prev_ref, mu_ref, wup_ref, aup_ref, gup_ref, w0_ref, a0_ref, kkg_ref, kag_ref,
                      seg_ref, r_ref, k_ref, v_ref, al_ref, be_ref, lw_ref, g_ref):
    i = pl.program_id(1)
    c0 = U_B * LANE
    p = x_ref[0, :, c0:c0 + B_WIDTH]
    row = lax.broadcasted_iota(jnp.int32, p.shape, 0)
    last = prev_ref[0, 7:8, c0:c0 + B_WIDTH]
    last = jnp.where(i == 0, jnp.zeros_like(last), last)
    prev = jnp.where(row == 0, jnp.broadcast_to(last, p.shape), pltpu.roll(p, 1, axis=0))
    ps = p + (prev - p) * mu_ref[...]
    d = D_BRANCH
    r, k, v = ps[:, 0:d], ps[:, d:2 * d], ps[:, 2 * d:3 * d]
    wd = ps[:, 3 * d:3 * d + 64]
    ad = ps[:, 3 * d + 64:3 * d + 128]
    gd = ps[:, 3 * d + 128:3 * d + 256]
    w_log = -_softplus(-(w0_ref[...] + _dot_split(jnp.tanh(wd), wup_ref[...]))) - 0.5
    a = jax.nn.sigmoid(a0_ref[...] + _dot_split(ad, aup_ref[...]))
    g = _dot_split(jax.nn.sigmoid(gd), gup_ref[...])
    kk = k * kkg_ref[...]
    sq_hi, sq_lo = _split2(kk * kk)
    nrm = jnp.sqrt(_dot(sq_hi, seg_ref[...]) + _dot(sq_lo, seg_ref[...]))
    kk = kk / jnp.maximum(nrm, 1e-12)
    r_ref[0] = r
    k_ref[0] = k * (1.0 + (a - 1.0) * kag_ref[...])
    v_ref[0] = v
    al_ref[0] = -kk
    be_ref[0] = kk * a
    lw_ref[0] = -jnp.exp(w_log)
    g_ref[0] = g


def _bdot(a, b, dims):
    return lax.dot_general(a, b, ((dims[0], dims[1]), ((0,), (0,))), preferred_element_type=F32)


BNN = ((2,), (1,))
BNT = ((2,), (2,))


def _rwkv_chunk_kernel(r_ref, k_ref, v_ref, al_ref, be_ref, lw_ref, q_ref, en_ref, wl_ref, *, rows, sub):
    hd = HEAD_DIM
    nch = rows // CHUNK
    nsb = rows // sub
    lw = lw_ref[0]
    rowc = lax.broadcasted_iota(jnp.int32, lw.shape, 0) & (CHUNK - 1)
    cum = lw
    for dd in (1, 2, 4, 8, 16, 32):
        cum = cum + jnp.where(rowc >= dd, pltpu.roll(cum, dd, axis=0), 0.0)
    cl = jnp.concatenate(
        [jnp.broadcast_to(cum[(c + 1) * CHUNK - 1:(c + 1) * CHUNK, :], (CHUNK, LANE)) for c in range(nch)], axis=0)
    e_in = jnp.exp(cum)
    e_out = jnp.exp(-cum)
    e_end = jnp.exp(cl - cum)
    r2, k2, v2, al2, be2 = r_ref[0], k_ref[0], v_ref[0], al_ref[0], be_ref[0]
    at2 = al2 * jnp.exp(cum - lw)
    rt2 = r2 * e_in
    bt2 = be2 * e_out
    kt2 = k2 * e_out
    bh2 = be2 * e_end
    kh2 = k2 * e_end
    wl2 = jnp.exp(cl)

    row = lax.broadcasted_iota(jnp.int32, (sub, sub), 0)
    col = lax.broadcasted_iota(jnp.int32, (sub, sub), 1)
    same = (row >> 6) == (col >> 6)
    m_strict = same & (row > col)
    m_incl = same & (row >= col)
    eye = (row == col).astype(F32)
    m_blk8 = (row >> 3) == (col >> 3)
    m_lvls = [((row >> (sh + 1)) == (col >> (sh + 1))) & ((row >> sh) != (col >> sh)) for sh in (3, 4, 5)]
    wl_ref[0] = jnp.concatenate([wl2[c * CHUNK:c * CHUNK + 1, :] for c in range(nch)], axis=0)

    def stack(a):
        return jnp.stack([a[sb * sub:(sb + 1) * sub, hh * hd:(hh + 1) * hd]
                          for hh in range(2) for sb in range(nsb)])

    at, rt, vb = stack(at2), stack(rt2), stack(v2).astype(BF16)
    lhs = jnp.concatenate([at, rt], axis=1).astype(BF16)
    rhs = jnp.concatenate([stack(bt2), stack(kt2)], axis=1).astype(BF16)
    gm = _bdot(lhs, rhs, BNT)
    n_ab = jnp.where(m_strict[None], gm[:, :sub, :sub], 0.0)
    a_ak = jnp.where(m_strict[None], gm[:, :sub, sub:], 0.0).astype(BF16)
    a_rb = jnp.where(m_incl[None], gm[:, sub:, :sub], 0.0).astype(BF16)
    a_rk = jnp.where(m_incl[None], gm[:, sub:, sub:], 0.0).astype(BF16)
    n8 = jnp.where(m_blk8[None], n_ab, 0.0)
    n8b = n8.astype(BF16)
    n_2 = _bdot(n8b, n8b, BNN)
    n2b = n_2.astype(BF16)
    n_3 = _bdot(n2b, n8b, BNN)
    n_4 = _bdot(n2b, n2b, BNN)
    tinv = eye[None] + n8 + n_2 + n_3
    tinv = tinv + _bdot(tinv.astype(BF16), n_4.astype(BF16), BNN)
    for m_lvl in m_lvls:
        nl = jnp.where(m_lvl[None], n_ab, 0.0).astype(BF16)
        tb = tinv.astype(BF16)
        tinv = tinv + _bdot(_bdot(tb, nl, BNN).astype(BF16), tb, BNN)
    tb = tinv.astype(BF16)
    akv = _bdot(a_ak, vb, BNN)
    pmat = _bdot(tb, jnp.concatenate([at, akv], axis=2).astype(BF16), BNN)
    qmat = _bdot(a_rb, pmat.astype(BF16), BNN) + jnp.concatenate([rt, _bdot(a_rk, vb, BNN)], axis=2)
    pb = pmat.astype(BF16)
    bhb = stack(bh2).astype(BF16)
    khb = stack(kh2).astype(BF16)
    for hh in range(2):
        for sb in range(nsb):
            bi = hh * nsb + sb
            q_ref[0, hh, sb * sub:(sb + 1) * sub, :] = qmat[bi]
            for cc in range(sub // CHUNK):
                cs = slice(cc * CHUNK, (cc + 1) * CHUNK)
                mn = _dot(pb[bi, cs], bhb[bi, cs], TN_DIMS)
                n_c = mn[hd:] + _dot(vb[bi, cs], khb[bi, cs], TN_DIMS)
                en_ref[0, hh, sb * (sub // CHUNK) + cc] = jnp.concatenate([mn[:hd], n_c], axis=0)


def _rwkv_state_kernel(q_ref, en_ref, wl_ref, r_ref, k_ref, v_ref, g_ref, rk_ref, gng_ref, gnb_ref,
                       o_ref, st_ref, *, rows, nhs):
    t = pl.program_id(2)
    hd = HEAD_DIM
    nch = rows // CHUNK

    @pl.when(t == 0)
    def _():
        st_ref[...] = jnp.zeros_like(st_ref)

    wl = wl_ref[0]
    r2, k2, v2, g2 = r_ref[0], k_ref[0], v_ref[0], g_ref[0]
    sts = [st_ref[hh] for hh in range(nhs)]
    ys = [[] for _ in range(nhs)]
    for c in range(nch):
        for hh in range(nhs):
            qc = q_ref[0, hh, c * CHUNK:(c + 1) * CHUNK, :]
            en = en_ref[0, hh, c]
            stb = sts[hh].astype(BF16)
            ys[hh].append(_dot(qc[:, :hd].astype(BF16), stb, NT_DIMS) + qc[:, hd:])
            sts[hh] = (sts[hh] * wl[c:c + 1, hh * hd:(hh + 1) * hd]
                       + _dot(stb, en[:hd].astype(BF16)) + en[hd:])
    outs = []
    for hh in range(nhs):
        sl = slice(hh * hd, (hh + 1) * hd)
        st_ref[hh] = sts[hh]
        v, r, kp = v2[:, sl], r2[:, sl], k2[:, sl]
        y = jnp.concatenate(ys[hh], axis=0)
        mu = jnp.mean(y, axis=-1, keepdims=True)
        var = jnp.mean(jnp.square(y - mu), axis=-1, keepdims=True)
        yn = (y - mu) * lax.rsqrt(var + RWKV_GN_EPS) * gng_ref[:, sl] + gnb_ref[:, sl]
        bonus = jnp.sum(r * kp * rk_ref[hh:hh + 1, :], axis=-1, keepdims=True) * v
        outs.append((yn + bonus) * g2[:, sl])
    o_ref[0] = jnp.concatenate(outs, axis=1).astype(o_ref.dtype)


def rwkv7(pm, mu, w_up, a_up, g_up, w0, a0, kk_gain, ka_gain, r_k, gn_g, gn_b, ts=256, rows=512, rows_c=1024,
          sub=256, nhs=8):
    b, s, _ = pm.shape
    d = D_BRANCH
    seg = np.kron(np.eye(d // HEAD_DIM, dtype=np.float32), np.ones((HEAD_DIM, HEAD_DIM), np.float32))
    row1 = lambda a: a.reshape(1, -1)
    cst = lambda shape: pl.BlockSpec(shape, lambda bi, i: (0,) * len(shape))
    blk = pl.BlockSpec((1, ts, d), lambda bi, i: (bi, i, 0))
    r, k, v, al, be, lw, g = pl.pallas_call(
        _rwkv_prep_kernel,
        grid=(b, s // ts),
        in_specs=[pl.BlockSpec((1, ts, B_BLOCK), lambda bi, i: (bi, i, 0)),
                  pl.BlockSpec((1, 8, B_BLOCK), lambda bi, i: (bi, jnp.maximum(i * (ts // 8) - 1, 0), 0)),
                  cst((1, B_WIDTH)), cst((64, d)), cst((64, d)), cst((128, d)),
                  cst((1, d)), cst((1, d)), cst((1, d)), cst((1, d)), cst((d, d))],
        out_specs=[blk] * 7,
        out_shape=[jax.ShapeDtypeStruct((b, s, d), F32)] * 7,
        compiler_params=_cp(("parallel", "arbitrary")),
        name="rwkv_prep",
    )(pm, pm, row1(mu), w_up, a_up, g_up, row1(w0), row1(a0), row1(kk_gain), row1(ka_gain), jnp.asarray(seg, BF16))

    rows_c = min(rows_c, s)
    hp = pl.BlockSpec((1, rows_c, LANE), lambda bi, h, t: (bi, t, h))
    nh = d // HEAD_DIM
    nch_c = rows_c // CHUNK
    nch = rows // CHUNK
    q_spec = pl.BlockSpec((1, 2, rows_c, LANE), lambda bi, h, t: (bi, h, t, 0))
    en_spec = pl.BlockSpec((1, 2, nch_c, LANE, HEAD_DIM), lambda bi, h, t: (bi, h, t, 0, 0))
    wl_spec = pl.BlockSpec((1, nch_c, LANE), lambda bi, h, t: (bi, t, h))
    qm, en, wl = pl.pallas_call(
        functools.partial(_rwkv_chunk_kernel, rows=rows_c, sub=sub),
        grid=(b, d // LANE, s // rows_c),
        in_specs=[hp] * 6,
        out_specs=[q_spec, en_spec, wl_spec],
        out_shape=[jax.ShapeDtypeStruct((b, nh, s, LANE), F32),
                   jax.ShapeDtypeStruct((b, nh, s // CHUNK, LANE, HEAD_DIM), F32),
                   jax.ShapeDtypeStruct((b, s // CHUNK, d), F32)],
        compiler_params=_cp(("parallel", "parallel", "parallel")),
        name="rwkv_chunk",
    )(r, k, v, al, be, lw)
    wide = nhs * HEAD_DIM
    rows_s = pl.BlockSpec((1, rows, wide), lambda bi, h, t: (bi, t, h))
    par_s = pl.BlockSpec((1, wide), lambda bi, h, t: (0, h))
    return pl.pallas_call(
        functools.partial(_rwkv_state_kernel, rows=rows, nhs=nhs),
        grid=(b, nh // nhs, s // rows),
        in_specs=[pl.BlockSpec((1, nhs, rows, LANE), lambda bi, h, t: (bi, h, t, 0)),
                  pl.BlockSpec((1, nhs, nch, LANE, HEAD_DIM), lambda bi, h, t: (bi, h, t, 0, 0)),
                  pl.BlockSpec((1, nch, wide), lambda bi, h, t: (bi, t, h)),
                  rows_s, rows_s, rows_s, rows_s,
                  pl.BlockSpec((nhs, HEAD_DIM), lambda bi, h, t: (h, 0)), par_s, par_s],
        out_specs=rows_s,
        out_shape=jax.ShapeDtypeStruct((b, s, d), BF16),
        scratch_shapes=[pltpu.VMEM((nhs, HEAD_DIM, HEAD_DIM), F32)],
        compiler_params=_cp(("parallel", "parallel", "arbitrary")),
        name="rwkv_state",
    )(qm, en, wl, r, k, v, g, r_k, row1(gn_g), row1(gn_b))


def _hgrn_kernel(q_ref, f_ref, i_ref, g_ref, lb_ref, ng_ref, o_ref, st_ref, *, rows, nh):
    t = pl.program_id(2)
    sub = 16

    @pl.when(t == 0)
    def _():
        st_ref[...] = jnp.zeros_like(st_ref)

    q = jax.nn.silu(q_ref[0])
    lb = lb_ref[...]
    fg = lb + (1.0 - lb) * jax.nn.sigmoid(f_ref[0])
    lf = jnp.log(fg) * LOG2E
    kf = 1.0 - fg
    iv = i_ref[0]
    rowc = lax.broadcasted_iota(jnp.int32, lf.shape, 0) & (CHUNK - 1)
    bcum = lf
    for dd in (1, 2, 4, 8, 16, 32):
        bcum = bcum + jnp.where(rowc >= dd, pltpu.roll(bcum, dd, axis=0), 0.0)
    ivb = iv.astype(BF16)
    rsub = lax.broadcasted_iota(jnp.int32, (sub, LANE), 0)
    sts = [st_ref[hh] for hh in range(nh)]
    outs = [[] for _ in range(nh)]
    for c in range(rows // CHUNK):
        for hh in range(nh):
            rs = slice(c * CHUNK, (c + 1) * CHUNK)
            ls = slice(hh * LANE, (hh + 1) * LANE)
            bc, qc, kc, ic, icb = bcum[rs, ls], q[rs, ls], kf[rs, ls], iv[rs, ls], ivb[rs, ls]
            o_inter = _dot((qc * jnp.exp2(bc)).astype(BF16), sts[hh].astype(BF16), NT_DIMS)
            blocks = []
            for ib in range(CHUNK // sub):
                r0 = ib * sub
                bi = bc[r0:r0 + sub]
                qi = qc[r0:r0 + sub]
                ki = kc[r0:r0 + sub]
                ii = ic[r0:r0 + sub]
                acc = jnp.zeros((sub, LANE), F32)
                for s_ in range(sub):
                    e = jnp.where(rsub >= s_, jnp.exp2(jnp.minimum(bi - bi[s_:s_ + 1], 0.0)), 0.0)
                    a = jnp.sum(qi * (ki[s_:s_ + 1] * e), axis=-1, keepdims=True)
                    acc = acc + a * ii[s_:s_ + 1]
                if ib > 0:
                    ref_row = bc[r0 - 1:r0]
                    qt = (qi * jnp.exp2(bi - ref_row)).astype(BF16)
                    kt = (kc[0:r0] * jnp.exp2(ref_row - bc[0:r0])).astype(BF16)
                    a_off = _dot(qt, kt, NT_DIMS).astype(BF16)
                    acc = acc + _dot(a_off, icb[0:r0])
                blocks.append(acc)
            outs[hh].append(o_inter + jnp.concatenate(blocks, axis=0))
            bl = bc[CHUNK - 1:CHUNK]
            kdec = (kc * jnp.exp2(bl - bc)).astype(BF16)
            sts[hh] = sts[hh] * jnp.exp2(bl) + _dot(icb, kdec, TN_DIMS)
    res = []
    for hh in range(nh):
        st_ref[hh] = sts[hh]
        o = jnp.concatenate(outs[hh], axis=0)
        ms = jnp.mean(o * o, axis=-1, keepdims=True)
        res.append(o * lax.rsqrt(ms + EPS) * ng_ref[:, hh * LANE:(hh + 1) * LANE])
    o_ref[0] = (jnp.concatenate(res, axis=1) * jax.nn.silu(g_ref[0])).astype(o_ref.dtype)


def hgrn2(pm, lb, norm_g, rows=256, nh=2):
    b, s, _ = pm.shape
    width = nh * LANE
    spec = lambda u: pl.BlockSpec((1, rows, width), lambda bi, h, t: (bi, t, u // nh + h))
    par = pl.BlockSpec((1, width), lambda bi, h, t: (0, h))
    return pl.pallas_call(
        functools.partial(_hgrn_kernel, rows=rows, nh=nh),
        grid=(b, D_BRANCH // width, s // rows),
        in_specs=[spec(U_CQ), spec(U_CF), spec(U_CI), spec(U_CG), par, par],
        out_specs=spec(0),
        out_shape=jax.ShapeDtypeStruct((b, s, D_BRANCH), BF16),
        scratch_shapes=[pltpu.VMEM((nh, LANE, LANE), F32)],
        compiler_params=_cp(("parallel", "parallel", "arbitrary")),
        name="hgrn2",
    )(pm, pm, pm, pm, lb.reshape(1, -1), norm_g.reshape(1, -1))


def _dsa_prep_kernel(lo_ref, hi_ref, cos_ref, sin_ref,
                     qo_ref, iqo_ref, ko_ref, vo_ref, iko_ref, wo_ref, *, scale, wscale):
    cos = cos_ref[0]
    sin = sin_ref[0]
    hd = HEAD_DIM
    lo = lo_ref[0]
    hi = hi_ref[0]
    col = lambda u: (u - U_D_LO) * LANE
    q = (_rope(lo[:, col(U_DQ):col(U_DKV)], cos, sin) * scale).astype(BF16)
    iq_in = jnp.concatenate([lo[:, col(U_DIQ):], hi[:, :(U_DIKW - U_D_HI) * LANE]], axis=1)
    iq = _rope(iq_in, cos, sin).astype(BF16)
    for h in range(DSA_HEADS):
        qo_ref[0, h] = q[:, h * hd:(h + 1) * hd]
        iqo_ref[0, h] = iq[:, h * hd:(h + 1) * hd]
    kv = lo[:, col(U_DKV):col(U_DIQ)]
    ko_ref[0] = _rope(kv, cos, sin)[:, :hd].astype(BF16)
    vo_ref[0] = kv[:, hd:].astype(BF16)
    ikw = hi[:, (U_DIKW - U_D_HI) * LANE:]
    iko_ref[0] = _rope(ikw, cos, sin)[:, :hd].astype(BF16)
    wo_ref[0] = ikw[:, hd:hd + IDX_HEADS] * wscale


def _to_key(x):
    x = jnp.where(x == 0.0, 0.0, x)
    bits = pltpu.bitcast(x, jnp.int32)
    return jnp.where(bits < 0, bits ^ jnp.int32(0x7FFFFFFF), bits)


def _dsa_index_kernel(iq_ref, ik_ref, wt_ref, mask_ref, key_ref, *, tq, tk, top_k):
    i = pl.program_id(1)
    nk = key_ref.shape[0]
    nvis = (i * tq + tq - 1) // tk + 1
    krow = lax.broadcasted_iota(jnp.int32, (tk, tq), 0)
    qcol = i * tq + lax.broadcasted_iota(jnp.int32, (tk, tq), 1)
    wt = wt_ref[0]

    def score_block(jb):
        ikb = ik_ref[0, pl.ds(pl.multiple_of(jb * tk, tk), tk), :]
        s_all = _dot(ikb, iq_ref[0].reshape(IDX_HEADS * tq, HEAD_DIM), NT_DIMS)
        acc = jnp.zeros((tk, tq), F32)
        for h in range(IDX_HEADS):
            acc = acc + jnp.maximum(s_all[:, h * tq:(h + 1) * tq], 0.0) * wt[h:h + 1, :]
        vis = ((jb * tk + krow) >> 6) <= (qcol >> 6)
        key_ref[jb] = jnp.where(vis, _to_key(acc), INT_MIN)

    _paired_loop(nvis, score_block)

    v_sub = (((i * tq + tq - 1) >> 6) + 1) * CHUNK // LANE
    nfull = (v_sub * LANE) // tk
    n_part = v_sub - nfull * (tk // LANE)
    krow_s = krow[:LANE]

    def count(pred_fn):
        def blk(jb, cnt):
            ind = jnp.where(pred_fn(key_ref[jb], jb * tk + krow), 1, 0)
            return cnt + jnp.sum(ind.reshape(tk // 8, 8, tq), axis=0)

        def part(r, cnt):
            r0 = pl.multiple_of(r * LANE, LANE)
            ind = jnp.where(pred_fn(key_ref[nfull, pl.ds(r0, LANE), :], nfull * tk + r0 + krow_s), 1, 0)
            return cnt + jnp.sum(ind.reshape(LANE // 8, 8, tq), axis=0)

        cnt = lax.fori_loop(0, nfull, blk, jnp.zeros((8, tq), jnp.int32))
        cnt = lax.fori_loop(0, n_part, part, cnt)
        return jnp.sum(cnt, axis=0, keepdims=True)

    def bit_step(bi, carry):
        prefix, n_ge = carry
        cand = prefix | lax.shift_left(jnp.int32(1), 31 - bi)
        cand_s = cand ^ jnp.int32(INT_MIN)
        c = count(lambda kb, kidx: kb >= cand_s)
        keep = c >= top_k
        return jnp.where(keep, cand, prefix), jnp.where(keep, c, n_ge)

    prefix, n_ge = lax.fori_loop(0, 32, bit_step, (jnp.zeros((1, tq), jnp.int32),
                                                   jnp.zeros((1, tq), jnp.int32) + v_sub * LANE))
    tau = prefix ^ jnp.int32(INT_MIN)
    n_gt = count(lambda kb, kidx: kb > tau)
    n_eq = n_ge - n_gt
    need = top_k - n_gt
    tied = (n_eq > need) & (tau != INT_MIN)
    n_cols = nk * tk

    idx_bits = int(n_cols).bit_length()

    def tie_break():
        def idx_step(bi, pre):
            cand = pre | lax.shift_left(jnp.int32(1), idx_bits - 1 - bi)
            c = count(lambda kb, kidx: (kb == tau) & (kidx < cand))
            return jnp.where(c < need, cand, pre)
        cut = lax.fori_loop(0, idx_bits, idx_step, jnp.zeros((1, tq), jnp.int32))
        return jnp.where(tied, cut, n_cols)

    cut = lax.cond(jnp.max(tied.astype(jnp.int32)) > 0, tie_break,
                   lambda: jnp.full((1, tq), n_cols, jnp.int32))

    def write_block(jb, carry):
        kb = key_ref[jb]
        sel = (kb > tau) | ((kb == tau) & ((jb * tk + krow) <= cut))
        sel = sel & (kb != INT_MIN)
        mask_ref[0, jb] = jnp.where(sel, 1.0, 0.0).T.astype(mask_ref.dtype)
        return carry

    def zero_block(jb, carry):
        mask_ref[0, jb] = jnp.zeros((tq, tk), mask_ref.dtype)
        return carry

    lax.fori_loop(0, nvis, write_block, 0)
    lax.fori_loop(nvis, nk, zero_block, 0)


def _dsa_attn_kernel(q_ref, k_ref, v_ref, mask_ref, o_ref, s_ref, m_ref, l_ref, acc_ref, *, tq, tk):
    i = pl.program_id(1)
    nh = DSA_HEADS
    hd = HEAD_DIM
    nvis = (i * tq + tq - 1) // tk + 1
    nchunk = tk // LANE

    m_ref[...] = jnp.full_like(m_ref, NEG)

    def max_step(j):
        kb = k_ref[0, pl.ds(pl.multiple_of(j * tk, tk), tk), :]
        s = _dot(q_ref[0].reshape(nh * tq, hd), kb, NT_DIMS).reshape(nh, tq, tk)
        s = jnp.where((mask_ref[0, j] > 0)[None], s, NEG)
        s_ref[j] = s
        m_ref[...] = jnp.maximum(m_ref[...], _lane_fold(s, jnp.maximum))

    _paired_loop(nvis, max_step)
    m_ref[...] = jnp.broadcast_to(jnp.max(m_ref[...], axis=2, keepdims=True), m_ref.shape)
    l_ref[...] = jnp.zeros_like(l_ref)
    acc_ref[...] = jnp.zeros_like(acc_ref)

    def acc_step(j):
        s = s_ref[j]
        mb = m_ref[...]
        ps = [jnp.exp2(s[:, :, c * LANE:(c + 1) * LANE] - mb) for c in range(nchunk)]
        l_ref[...] += functools.reduce(jnp.add, ps)
        p = jnp.concatenate(ps, axis=2).astype(BF16).reshape(nh * tq, tk)
        vb = v_ref[0, pl.ds(pl.multiple_of(j * tk, tk), tk), :]
        acc_ref[...] += _dot(p, vb).reshape(nh, tq, hd)

    _paired_loop(nvis, acc_step)
    o = acc_ref[...] / jnp.sum(l_ref[...], axis=2, keepdims=True)
    o_ref[0] = jnp.concatenate([o[h] for h in range(nh)], axis=1).astype(o_ref.dtype)


def dsa_attention(pm, cos, sin, ts=512, tq=128, tk=512):
    b, s, _ = pm.shape
    top_k = min(TOPK_MAX, s // 4)
    hd = HEAD_DIM
    nk = s // tk
    tab = pl.BlockSpec((1, ts, LANE), lambda bi, i: (bi, i, 0))
    heads_out = pl.BlockSpec((1, DSA_HEADS, ts, hd), lambda bi, i: (bi, 0, i, 0))
    narrow = lambda w: pl.BlockSpec((1, ts, w), lambda bi, i: (bi, i, 0))
    w_lo = (U_D_HI - U_D_LO) * LANE
    w_hi = N_MAIN - U_D_HI * LANE
    q, iq, k, v, ik, w = pl.pallas_call(
        functools.partial(_dsa_prep_kernel, scale=hd ** -0.5 * LOG2E, wscale=(IDX_HEADS ** -0.5) * (hd ** -0.5)),
        grid=(b, s // ts),
        in_specs=[pl.BlockSpec((1, ts, w_lo), lambda bi, i: (bi, i, U_D_LO * LANE // w_lo)),
                  pl.BlockSpec((1, ts, w_hi), lambda bi, i: (bi, i, U_D_HI * LANE // w_hi)),
                  tab, tab],
        out_specs=[heads_out, heads_out, narrow(hd), narrow(hd), narrow(hd), narrow(IDX_HEADS)],
        out_shape=[jax.ShapeDtypeStruct((b, DSA_HEADS, s, hd), BF16)] * 2
                  + [jax.ShapeDtypeStruct((b, s, hd), BF16)] * 3
                  + [jax.ShapeDtypeStruct((b, s, IDX_HEADS), F32)],
        compiler_params=_cp(("parallel", "parallel")),
        name="dsa_prep",
    )(pm, pm, cos, sin)

    mask_spec = pl.BlockSpec((1, nk, tq, tk), lambda bi, i: (bi, 0, i, 0))
    whole = pl.BlockSpec((1, s, hd), lambda bi, i: (bi, 0, 0))
    mask = pl.pallas_call(
        functools.partial(_dsa_index_kernel, tq=tq, tk=tk, top_k=top_k),
        grid=(b, s // tq),
        in_specs=[pl.BlockSpec((1, IDX_HEADS, tq, hd), lambda bi, i: (bi, 0, i, 0)),
                  whole,
                  pl.BlockSpec((1, IDX_HEADS, tq), lambda bi, i: (bi, 0, i))],
        out_specs=mask_spec,
        out_shape=jax.ShapeDtypeStruct((b, nk, s, tk), BF16),
        scratch_shapes=[pltpu.VMEM((nk, tk, tq), jnp.int32)],
        compiler_params=_cp(("parallel", "parallel")),
        name="dsa_index",
    )(iq, ik, jnp.swapaxes(w, 1, 2))

    return pl.pallas_call(
        functools.partial(_dsa_attn_kernel, tq=tq, tk=tk),
        grid=(b, s // tq),
        in_specs=[pl.BlockSpec((1, DSA_HEADS, tq, hd), lambda bi, i: (bi, 0, i, 0)),
                  whole, whole, mask_spec],
        out_specs=pl.BlockSpec((1, tq, D_BRANCH), lambda bi, i: (bi, i, 0)),
        out_shape=jax.ShapeDtypeStruct((b, s, D_BRANCH), BF16),
        scratch_shapes=[pltpu.VMEM((nk, DSA_HEADS, tq, tk), F32),
                        pltpu.VMEM((DSA_HEADS, tq, LANE), F32), pltpu.VMEM((DSA_HEADS, tq, LANE), F32),
                        pltpu.VMEM((DSA_HEADS, tq, hd), F32)],
        compiler_params=_cp(("parallel", "parallel")),
        name="dsa_attention",
    )(q, k, v, mask)


GATE_COL = 6600


def kernel(x, positions, norm_g, w_in, diff_lambda, diff_subln_g, rwkv_mu, rwkv_w_up, rwkv_a_up, rwkv_g_up, rwkv_w0, rwkv_a0, rwkv_k_k, rwkv_k_a, rwkv_r_k, rwkv_gn_g, rwkv_gn_b, hgrn_lb_logits, hgrn_norm_g, w_branch, w_out, mlp_w1, mlp_w2):
    b, s, d = x.shape
    m = b * s
    depth = w_in.shape[0]
    lb = jax.nn.softmax(hgrn_lb_logits.astype(F32), axis=0)
    lb = jnp.cumsum(lb, axis=0) - lb[0]
    w_in_t = jnp.swapaxes(w_in, 1, 2)
    w_out_b = w_out.astype(BF16)
    w2_b = mlp_w2.astype(BF16)
    cos, sin = rope_tables(positions)
    cos3 = cos.reshape(b, s, LANE)
    sin3 = sin.reshape(b, s, LANE)
    x2 = x.reshape(m, d)
    h = rmsnorm_bf16(x2, norm_g[0, 0])
    for l in range(depth):
        pm = matmul_wt(h, w_in_t, l, 0, N_MAIN, None, F32, name="proj_main").reshape(b, s, N_MAIN)
        gate = matmul_wt(h, w_in_t, l, GATE_COL, N_BRANCH * D_MODEL, "sigmoid", BF16, tn=1024, name="proj_gate")
        y_a = diff_attention(pm, cos3, sin3, diff_lambda[l], diff_subln_g[l], l)
        y_b = rwkv7(pm, rwkv_mu[l], rwkv_w_up[l], rwkv_a_up[l], rwkv_g_up[l], rwkv_w0[l], rwkv_a0[l],
                    rwkv_k_k[l], rwkv_k_a[l], rwkv_r_k[l], rwkv_gn_g[l], rwkv_gn_b[l])
        y_c = hgrn2(pm, lb[l], hgrn_norm_g[l])
        y_d = dsa_attention(pm, cos3, sin3)
        ys = [y.reshape(m, D_BRANCH) for y in (y_a, y_b, y_c, y_d)]
        merged = gated_merge(ys, w_branch, l, gate)
        x2, h2 = matmul_norm_residual(merged, w_out_b, l, x2, norm_g[l, 1], norm_g[l, 2], name="out_proj")
        ff = matmul(h2, mlp_w1, l, "relu2", BF16, tn=1024, name="mlp_up")
        if l + 1 < depth:
            x2, h = matmul_norm_residual(ff, w2_b, l, x2, norm_g[l, 3], norm_g[l + 1, 0], name="mlp_down")
        else:
            x2 = matmul_norm_residual(ff, w2_b, l, x2, norm_g[l, 3], name="mlp_down")
    return x2.reshape(b, s, d)
```
